```python
import math
import jax
import jax.numpy as jnp
from jax import lax
import numpy as np

D_MODEL = 2048
BATCH = 2
SEQ = 4096
DEPTH = 2

GRID_W = 64
CTX_LEN = 256
EPS = 1e-6
ROPE_THETA = 10000.0

A_HEADS = 4
A_DK = 64
A_DV = 128
A_WIDTH = A_HEADS * A_DV
A_QK_COLS = A_HEADS * 2 * A_DK
Q_BLOCK = 128

B_WIDTH = 512

C_HEADS = 8
C_KV_HEADS = 2
C_GROUP = C_HEADS // C_KV_HEADS
C_DH = 128
C_WIDTH = C_HEADS * C_DH
C_KV_COLS = C_KV_HEADS * C_DH
WINDOW = 128
BAND_BLOCK = 128

MIX_WIDTH = A_WIDTH + B_WIDTH + C_WIDTH
GROUP_COLS = (A_QK_COLS, A_QK_COLS, A_WIDTH, B_WIDTH, B_WIDTH, B_WIDTH, C_WIDTH, C_KV_COLS, C_KV_COLS)
IN_COLS = sum(GROUP_COLS)
SPLIT_AT = tuple(int(v) for v in np.cumsum(GROUP_COLS)[:-1])

D_FF = 5632
N_EXPERTS = 8
TOP_K = 2
D_FF_EXPERT = 7168
N_DENSE = (DEPTH + 1) // 2
N_MOE = DEPTH // 2

kernel_name = 'hybrid_parallel_heads_diffusion_block'


def rms_norm(x, g):
    xf = x.astype(jnp.float32)
    y = xf * lax.rsqrt(jnp.mean(xf * xf, axis=-1, keepdims=True) + EPS)
    return (y * g.astype(jnp.float32)).astype(x.dtype)


def axial_rope_tables(n_tok, dh):
    rows = n_tok // GRID_W
    row = jnp.repeat(jnp.arange(rows, dtype=jnp.float32), GRID_W)
    col = jnp.tile(jnp.arange(GRID_W, dtype=jnp.float32), rows)
    half = dh // 2
    inv = ROPE_THETA ** (-jnp.arange(0, half, 2, dtype=jnp.float32) / half)
    ar = row[:, None] * inv[None, :]
    ac = col[:, None] * inv[None, :]
    ang = jnp.concatenate([ar, ar, ac, ac], axis=-1)
    return jnp.cos(ang), jnp.sin(ang)


def apply_rope(x, cos, sin):
    x1, x2, x3, x4 = jnp.split(x, 4, axis=-1)
    rot = jnp.concatenate([-x2, x1, -x4, x3], axis=-1)
    bshape = (x.shape[1],) + (1,) * (x.ndim - 3) + (x.shape[-1],)
    out = x.astype(jnp.float32) * cos.reshape(bshape) + rot.astype(jnp.float32) * sin.reshape(bshape)
    return out.astype(x.dtype)


def split_heads(p, qn_a, kn_a, qn_c, kn_c):
    b, n = p.shape[0], p.shape[1]
    a_q, a_k, a_v, b_b, b_c, b_x, c_q, c_k, c_v = jnp.split(p, SPLIT_AT, axis=-1)
    a_q = rms_norm(a_q.reshape(b, n, A_HEADS, 2, A_DK), qn_a)
    a_k = rms_norm(a_k.reshape(b, n, A_HEADS, 2, A_DK), kn_a)
    a_v = a_v.reshape(b, n, A_HEADS, A_DV)
    c_q = rms_norm(c_q.reshape(b, n, C_KV_HEADS, C_GROUP, C_DH), qn_c)
    c_k = rms_norm(c_k.reshape(b, n, C_KV_HEADS, C_DH), kn_c)
    c_v = c_v.reshape(b, n, C_KV_HEADS, C_DH)
    return a_q, a_k, a_v, b_b, b_c, b_x, c_q, c_k, c_v


def diff_lambda(lq1, lk1, lq2, lk2, lam_init):
    f = lambda u, v: jnp.exp(jnp.sum(u.astype(jnp.float32) * v.astype(jnp.float32)))
    return f(lq1, lk1) - f(lq2, lk2) + lam_init


def diff_attend(q, keys, vals, lam):
    s = jnp.einsum('bqhcd,bkhcd->bhcqk', q, keys, preferred_element_type=jnp.float32) * (A_DK ** -0.5)
    p = jax.nn.softmax(s, axis=-1)
    w = p[:, :, 0] - lam * p[:, :, 1]
    return jnp.einsum('bhqk,bkhd->bqhd', w.astype(vals.dtype), vals)


def diff_attention_latent(q, keys, vals, lam):
    b, s = q.shape[0], q.shape[1]
    nb = s // Q_BLOCK
    qb = jnp.moveaxis(q.reshape((b, nb, Q_BLOCK) + q.shape[2:]), 1, 0)
    out = lax.map(lambda qi: diff_attend(qi, keys, vals, lam), qb)
    return jnp.moveaxis(out, 0, 1).reshape(b, s, A_HEADS, A_DV)


def short_conv3(u, w):
    up = jnp.pad(u, ((0, 0), (1, 1), (0, 0)))
    return up[:, :-2] * w[0] + up[:, 1:-1] * w[1] + up[:, 2:] * w[2]


def banded(t):
    b, s = t.shape[0], t.shape[1]
    nb = s // BAND_BLOCK
    tb = t.reshape((b, nb, BAND_BLOCK) + t.shape[2:])
    tp = jnp.pad(tb, ((0, 0), (1, 1), (0, 0), (0, 0), (0, 0)))
    return jnp.concatenate([tp[:, :-2], tp[:, 1:-1], tp[:, 2:]], axis=2)


def band_mask(nb):
    a = jnp.arange(BAND_BLOCK)[:, None]
    j = jnp.arange(3 * BAND_BLOCK)[None, :]
    in_win = jnp.abs(j - a - BAND_BLOCK) <= WINDOW
    blk = jnp.arange(nb)[:, None] - 1 + (jnp.arange(3 * BAND_BLOCK) // BAND_BLOCK)[None, :]
    valid = (blk >= 0) & (blk < nb)
    return in_win[None] & valid[:, None, :]


def softmax_with_sink(s, sink_b):
    sk = jnp.broadcast_to(sink_b, s.shape[:-1] + (1,))
    return jax.nn.softmax(jnp.concatenate([s, sk], axis=-1), axis=-1)[..., :-1]


def window_attention_latent(q, k, v, k_ctx, v_ctx, sink):
    b, s = q.shape[0], q.shape[1]
    nb = s // BAND_BLOCK
    scale = C_DH ** -0.5
    qb = q.reshape(b, nb, BAND_BLOCK, C_KV_HEADS, C_GROUP, C_DH)
    kb, vb = banded(k), banded(v)
    s_loc = jnp.einsum('bnqkgd,bnjkd->bnkgqj', qb, kb, preferred_element_type=jnp.float32) * scale
    s_loc = jnp.where(band_mask(nb)[None, :, None, None], s_loc, -jnp.inf)
    s_ctx = jnp.einsum('bnqkgd,bjkd->bnkgqj', qb, k_ctx, preferred_element_type=jnp.float32) * scale
    sink_b = sink.astype(jnp.float32).reshape(1, 1, C_KV_HEADS, C_GROUP, 1, 1)
    p = softmax_with_sink(jnp.concatenate([s_loc, s_ctx], axis=-1), sink_b)
    n_loc = 3 * BAND_BLOCK
    o = (jnp.einsum('bnkgqj,bnjkd->bnqkgd', p[..., :n_loc].astype(v.dtype), vb)
         + jnp.einsum('bnkgqj,bjkd->bnqkgd', p[..., n_loc:].astype(v.dtype), v_ctx))
    return o.reshape(b, s, C_WIDTH)


def window_attention_context(q, k, v, sink):
    b, n = q.shape[0], q.shape[1]
    s = jnp.einsum('bqkgd,bjkd->bkgqj', q, k, preferred_element_type=jnp.float32) * (C_DH ** -0.5)
    p = softmax_with_sink(s, sink.astype(jnp.float32).reshape(1, C_KV_HEADS, C_GROUP, 1, 1))
    o = jnp.einsum('bkgqj,bjkd->bqkgd', p.astype(v.dtype), v)
    return o.reshape(b, n, C_WIDTH)


def swiglu(h, w1, w3, w2):
    return (jax.nn.silu(h @ w1) * (h @ w3)) @ w2


def moe_swiglu(h, w_r, w1, w3, w2):
    logits = (h @ w_r).astype(jnp.float32)
    top_v, top_i = lax.top_k(logits, TOP_K)
    gates = jax.nn.softmax(top_v, axis=-1)
    weight = jnp.sum(jax.nn.one_hot(top_i, N_EXPERTS, dtype=jnp.float32) * gates[..., None], axis=-2)
    y = jnp.zeros_like(h)
    for e in range(N_EXPERTS):
        y = y + weight[..., e:e + 1].astype(h.dtype) * swiglu(h, w1[e], w3[e], w2[e])
    return y


def setup_inputs(seed: int = 0) -> dict:
    key = jax.random.key(seed)
    ks = iter(jax.random.split(key, 32))
    f32 = jnp.float32
    D = D_MODEL

    def nrm(shape, scale):
        return jax.random.normal(next(ks), shape, f32) * scale

    def gain(shape):
        return 1.0 + 0.02 * jax.random.normal(next(ks), shape, f32)

    return {
        'x': nrm((BATCH, SEQ, D), 1.0),
        'c': nrm((BATCH, D), 1.0),
        'ctx': nrm((BATCH, CTX_LEN, D), 1.0),
        'c_ctx': nrm((D,), 1.0),
        'ada_w': nrm((DEPTH, D, 6 * D), 0.5 * D ** -0.5),
        'ada_b': nrm((DEPTH, 6 * D), 0.02),
        'norm_mix_g': gain((DEPTH, D)),
        'norm_ffn_g': gain((DEPTH, D)),
        'w_in': nrm((DEPTH, D, IN_COLS), D ** -0.5),
        'w_out': nrm((DEPTH, MIX_WIDTH, D), MIX_WIDTH ** -0.5),
        'a_q_norm': gain((DEPTH, A_DK)),
        'a_k_norm': gain((DEPTH, A_DK)),
        'a_lam_q1': nrm((DEPTH, A_DK), 0.1),
        'a_lam_k1': nrm((DEPTH, A_DK), 0.1),
        'a_lam_q2': nrm((DEPTH, A_DK), 0.1),
        'a_lam_k2': nrm((DEPTH, A_DK), 0.1),
        'a_subln_g': gain((DEPTH, A_DV)),
        'b_conv_w': nrm((DEPTH, 3, B_WIDTH), 3 ** -0.5),
        'b_out_g': gain((DEPTH, B_WIDTH)),
        'c_q_norm': gain((DEPTH, C_DH)),
        'c_k_norm': gain((DEPTH, C_DH)),
        'c_sink': nrm((DEPTH, C_HEADS), 0.5),
        'c_out_g': gain((DEPTH, C_WIDTH)),
        'ffn_w1': nrm((N_DENSE, D, D_FF), D ** -0.5),
        'ffn_w3': nrm((N_DENSE, D, D_FF), D ** -0.5),
        'ffn_w2': nrm((N_DENSE, D_FF, D), D_FF ** -0.5),
        'router_w': nrm((N_MOE, D, N_EXPERTS), D ** -0.5),
        'moe_w1': nrm((N_MOE, N_EXPERTS, D, D_FF_EXPERT), D ** -0.5),
        'moe_w3': nrm((N_MOE, N_EXPERTS, D, D_FF_EXPERT), D ** -0.5),
        'moe_w2': nrm((N_MOE, N_EXPERTS, D_FF_EXPERT, D), D_FF_EXPERT ** -0.5),
    }


def reference(x, c, ctx, c_ctx, ada_w, ada_b, norm_mix_g, norm_ffn_g, w_in, w_out,
              a_q_norm, a_k_norm, a_lam_q1, a_lam_k1, a_lam_q2, a_lam_k2, a_subln_g,
              b_conv_w, b_out_g, c_q_norm, c_k_norm, c_sink, c_out_g,
              ffn_w1, ffn_w3, ffn_w2, router_w, moe_w1, moe_w3, moe_w2):
    B, S, _ = x.shape
    L = ctx.shape[1]
    cos_a, sin_a = axial_rope_tables(S, A_DK)
    cos_c, sin_c = axial_rope_tables(S, C_DH)
    xc = ctx
    for l in range(DEPTH):
        last = l == DEPTH - 1
        mod = jax.nn.silu(c) @ ada_w[l] + ada_b[l]
        mod_c = jax.nn.silu(c_ctx) @ ada_w[l] + ada_b[l]
        sh_a, sc_a, g_a, sh_f, sc_f, g_f = [m[:, None, :] for m in jnp.split(mod, 6, axis=-1)]
        csh_a, csc_a, cg_a, csh_f, csc_f, cg_f = jnp.split(mod_c, 6, axis=-1)

        h = rms_norm(x, norm_mix_g[l]) * (1 + sc_a) + sh_a
        hc = rms_norm(xc, norm_mix_g[l]) * (1 + csc_a) + csh_a
        proj = jnp.concatenate([hc, h], axis=1) @ w_in[l]
        norms = (a_q_norm[l], a_k_norm[l], c_q_norm[l], c_k_norm[l])
        aq, ak, av, bb, bc, bx, cq, ck, cv = split_heads(proj[:, L:], *norms)
        aqc, akc, avc, bbc, bcc, bxc, cqc, ckc, cvc = split_heads(proj[:, :L], *norms)
        aq, ak = apply_rope(aq, cos_a, sin_a), apply_rope(ak, cos_a, sin_a)
        cq, ck = apply_rope(cq, cos_c, sin_c), apply_rope(ck, cos_c, sin_c)

        lam_init = 0.8 - 0.6 * math.exp(-0.3 * l)
        lam = diff_lambda(a_lam_q1[l], a_lam_k1[l], a_lam_q2[l], a_lam_k2[l], lam_init)

        ya = diff_attention_latent(aq, jnp.concatenate([ak, akc], axis=1),
                                   jnp.concatenate([av, avc], axis=1), lam)
        ya = (rms_norm(ya, a_subln_g[l]) * (1 - lam_init)).reshape(B, S, A_WIDTH)
        yb = rms_norm(bb * short_conv3(bc * bx, b_conv_w[l]), b_out_g[l])
        yc = rms_norm(window_attention_latent(cq, ck, cv, ckc, cvc, c_sink[l]), c_out_g[l])
        x = x + g_a * (jnp.concatenate([ya, yb, yc], axis=-1) @ w_out[l])

        if not last:
            ya_c = (rms_norm(diff_attend(aqc, akc, avc, lam), a_subln_g[l]) * (1 - lam_init)).reshape(B, L, A_WIDTH)
            yb_c = rms_norm(bbc * short_conv3(bcc * bxc, b_conv_w[l]), b_out_g[l])
            yc_c = rms_norm(window_attention_context(cqc, ckc, cvc, c_sink[l]), c_out_g[l])
            xc = xc + cg_a * (jnp.concatenate([ya_c, yb_c, yc_c], axis=-1) @ w_out[l])

        h2 = rms_norm(x, norm_ffn_g[l]) * (1 + sc_f) + sh_f
        if not last:
            h2c = rms_norm(xc, norm_ffn_g[l]) * (1 + csc_f) + csh_f
            tokens = jnp.concatenate([h2c, h2], axis=1)
        else:
            tokens = h2
        i = l // 2
        if l % 2 == 0:
            y = swiglu(tokens, ffn_w1[i], ffn_w3[i], ffn_w2[i])
        else:
            y = moe_swiglu(tokens, router_w[i], moe_w1[i], moe_w3[i], moe_w2[i])
        if not last:
            xc = xc + cg_f * y[:, :L]
            x = x + g_f * y[:, L:]
        else:
            x = x + g_f * y
    return x
```

```python
import functools
import math

import jax
import jax.numpy as jnp
from jax import lax
from jax.experimental import pallas as pl
from jax.experimental.pallas import tpu as pltpu

F32 = jnp.float32
BF16 = jnp.bfloat16
U32 = jnp.uint32

D_MODEL = 2048
BATCH = 2
SEQ = 4096
DEPTH = 2
GRID_W = 64
CTX_LEN = 256
EPS = 1e-6
ROPE_THETA = 10000.0

A_HEADS = 4
A_DK = 64
A_DV = 128
A_WIDTH = A_HEADS * A_DV
B_WIDTH = 512
C_HEADS = 8
C_KV_HEADS = 2
C_GROUP = C_HEADS // C_KV_HEADS
C_DH = 128
C_WIDTH = C_HEADS * C_DH
C_KV_COLS = C_KV_HEADS * C_DH
WINDOW = 128
MIX_WIDTH = A_WIDTH + B_WIDTH + C_WIDTH
IN_COLS = 3 * A_WIDTH + 3 * B_WIDTH + C_WIDTH + 2 * C_KV_COLS
D_FF = 5632
N_EXPERTS = 8
D_FF_EXPERT = 7168

COL_AQ, COL_AK, COL_AV = 0, 512, 1024
COL_BB, COL_BC, COL_BX = 1536, 2048, 2560
COL_CQ, COL_CK, COL_CV = 3072, 4096, 4352

LANES = 128
N_TOK = CTX_LEN + SEQ
TM = 256
N_TILES = N_TOK // TM
CBLK = 128
N_CBLK = N_TOK // CBLK
A_CHUNK = 512
FFN_TM = N_TOK // 4
FFN_TF = 256
MOE_TM = 512
MOE_TF = 512
MOE_ROWS = 2 * BATCH * SEQ + N_EXPERTS * MOE_TM
MOE_TILES = MOE_ROWS // MOE_TM
ROUTE_TM = 512
COMB_TM = 256
DISP_TM = 256
VMEM_LIMIT = 56 * 1024 * 1024
NEG = -1e30

NT_DIMS = (((1,), (1,)), ((), ()))


def _cparams(*sem):
    return pltpu.CompilerParams(dimension_semantics=sem, vmem_limit_bytes=VMEM_LIMIT)


def _dot(a, b):
    return jnp.dot(a, b, preferred_element_type=F32)


def _rms(x, width):
    return x * lax.rsqrt(jnp.sum(x * x, axis=-1, keepdims=True) * (1.0 / width) + EPS)


ADA_TN = 1024


def _ada_kernel(c_ref, w_ref, b_ref, o_ref):
    a = c_ref[...]
    a = a / (1.0 + jnp.exp(-a))
    o_ref[0] = _dot(a.astype(BF16), w_ref[0].astype(BF16)) + b_ref[0]


def _ada_call(cond, ada_w, ada_b):
    n = 6 * D_MODEL
    return pl.pallas_call(
        _ada_kernel,
        grid=(DEPTH, n // ADA_TN),
        in_specs=[
            pl.BlockSpec((8, D_MODEL), lambda l, j: (0, 0)),
            pl.BlockSpec((1, D_MODEL, ADA_TN), lambda l, j: (l, 0, j)),
            pl.BlockSpec((1, 1, ADA_TN), lambda l, j: (l, 0, j)),
        ],
        out_specs=pl.BlockSpec((1, 8, ADA_TN), lambda l, j: (l, 0, j)),
        out_shape=jax.ShapeDtypeStruct((DEPTH, 8, n), F32),
        compiler_params=_cparams("arbitrary", "arbitrary"),
        name="ada_mod",
    )(cond, ada_w, ada_b.reshape(DEPTH, 1, n))


def _rope128(n, cos, s_odd, s_even, quarter):
    return n * cos + pltpu.roll(n, quarter, 1) * s_odd + pltpu.roll(n, LANES - quarter, 1) * s_even


def _in_kernel(*refs, with_y):
    if with_y:
        x_ref, y_ref, gf_ref = refs[:3]
        refs = refs[3:]
    else:
        x_ref = refs[0]
        refs = refs[1:]
    (g_ref, sc_ref, sh_ref, w_ref, aqn_ref, akn_ref, cqn_ref, ckn_ref,
     ca_ref, sa1_ref, sa2_ref, cc_ref, sc1_ref, sc2_ref) = refs[:14]
    outs = refs[14:]
    if with_y:
        xo_ref, outs = outs[0], outs[1:]
    qa_ref, ka_ref, va_ref, bb_ref, u_ref, qc_ref, kc_ref, vc_ref = outs

    x = x_ref[0]
    if with_y:
        x = x + gf_ref[0] * y_ref[0].astype(F32)
        xo_ref[0] = x
    h = _rms(x, D_MODEL) * g_ref[...]
    h = h * (1.0 + sc_ref[0]) + sh_ref[0]
    hb = h.astype(BF16)

    lane = lax.broadcasted_iota(jnp.int32, (TM, LANES), 1)
    lo_half = lane < A_DK

    def head_a(col0, j, gain_ref, scale):
        c0 = col0 + j * LANES
        p = _dot(hb, w_ref[:, c0:c0 + LANES])
        p2 = p * p
        s_all = jnp.sum(p2, axis=-1, keepdims=True)
        s_lo = jnp.sum(jnp.where(lo_half, p2, 0.0), axis=-1, keepdims=True)
        ms = jnp.where(lo_half, s_lo, s_all - s_lo) * (1.0 / A_DK)
        n = p * lax.rsqrt(ms + EPS) * gain_ref[...]
        r = _rope128(n, ca_ref[...], sa1_ref[...], sa2_ref[...], A_DK // 4)
        return (r * scale).astype(BF16)

    def head_c(col0, j, gain_ref, scale):
        c0 = col0 + j * LANES
        p = _dot(hb, w_ref[:, c0:c0 + LANES])
        n = _rms(p, C_DH) * gain_ref[...]
        r = _rope128(n, cc_ref[...], sc1_ref[...], sc2_ref[...], C_DH // 4)
        return (r * scale).astype(BF16)

    for j in range(A_WIDTH // LANES):
        sl = slice(j * LANES, (j + 1) * LANES)
        qa_ref[0, :, sl] = head_a(COL_AQ, j, aqn_ref, A_DK ** -0.5)
        ka_ref[0, :, sl] = head_a(COL_AK, j, akn_ref, 1.0)
    va_ref[0] = _dot(hb, w_ref[:, COL_AV:COL_AV + A_WIDTH]).astype(BF16)
    bb_ref[0] = _dot(hb, w_ref[:, COL_BB:COL_BB + B_WIDTH]).astype(BF16)
    u = _dot(hb, w_ref[:, COL_BC:COL_BC + B_WIDTH]) * _dot(hb, w_ref[:, COL_BX:COL_BX + B_WIDTH])
    u_ref[0] = u.astype(BF16)
    for j in range(C_HEADS):
        qc_ref[0, :, j * LANES:(j + 1) * LANES] = head_c(COL_CQ, j, cqn_ref, C_DH ** -0.5)
    for j in range(C_KV_HEADS):
        kc_ref[0, :, j * LANES:(j + 1) * LANES] = head_c(COL_CK, j, ckn_ref, 1.0)
    vc_ref[0] = _dot(hb, w_ref[:, COL_CV:COL_CV + C_KV_COLS]).astype(BF16)


def _mod_row(b, t):
    return jnp.where(t == 0, BATCH, b)


def _in_call(x, y, gf, g, sc, sh, w, gains, tables):
    with_y = y is not None
    tile = lambda width: pl.BlockSpec((1, TM, width), lambda b, t: (b, t, 0))
    modv = pl.BlockSpec((1, 1, D_MODEL), lambda b, t: (_mod_row(b, t), 0, 0))
    full2 = lambda a: pl.BlockSpec(a.shape, lambda b, t: (0, 0))
    rows = pl.BlockSpec((TM, LANES), lambda b, t: (t, 0))
    in_specs = [tile(D_MODEL)]
    args = [x]
    if with_y:
        in_specs += [tile(D_MODEL), modv]
        args += [y, gf]
    in_specs += [full2(g), modv, modv,
                 pl.BlockSpec(w.shape, lambda b, t: (0, 0), pipeline_mode=pl.Buffered(1))]
    args += [g, sc, sh, w]
    in_specs += [full2(a) for a in gains]
    args += list(gains)
    in_specs += [rows] * 6
    args += list(tables)
    widths = (A_WIDTH, A_WIDTH, A_WIDTH, B_WIDTH, B_WIDTH, C_WIDTH, C_KV_COLS, C_KV_COLS)
    out_specs = [tile(wd) for wd in widths]
    out_shape = [jax.ShapeDtypeStruct((BATCH, N_TOK, wd), BF16) for wd in widths]
    if with_y:
        out_specs = [tile(D_MODEL)] + out_specs
        out_shape = [jax.ShapeDtypeStruct((BATCH, N_TOK, D_MODEL), F32)] + out_shape
    return pl.pallas_call(
        functools.partial(_in_kernel, with_y=with_y),
        grid=(BATCH, N_TILES),
        in_specs=in_specs,
        out_specs=out_specs,
        out_shape=out_shape,
        compiler_params=_cparams("arbitrary", "arbitrary"),
        name="in_proj",
    )(*args)


def _attn_a_kernel(q_ref, k_ref, v_ref, lq1_ref, lk1_ref, lq2_ref, lk2_ref, g_ref, o_ref,
                   m_sc, l_sc, acc_sc, *, lam_init, t_off):
    t = pl.program_id(2) + t_off
    q = q_ref[0]
    lane = lax.broadcasted_iota(jnp.int32, (TM, LANES), 1)
    zero = jnp.zeros_like(q)
    qs = jnp.concatenate([jnp.where(lane < A_DK, q, zero), jnp.where(lane >= A_DK, q, zero)], axis=0)

    m_sc[...] = jnp.full(m_sc.shape, NEG, F32)
    l_sc[...] = jnp.zeros(l_sc.shape, F32)
    acc_sc[...] = jnp.zeros(acc_sc.shape, F32)

    def step(k, v):
        s = lax.dot_general(qs, k, NT_DIMS, preferred_element_type=F32)
        m_prev = m_sc[...]
        m_new = jnp.maximum(m_prev, jnp.max(s, axis=-1, keepdims=True))
        alpha = jnp.exp(m_prev - m_new)
        p = jnp.exp(s - m_new)
        l_sc[...] = alpha * l_sc[...] + jnp.sum(p, axis=-1, keepdims=True)
        acc_sc[...] = alpha * acc_sc[...] + _dot(p.astype(BF16), v)
        m_sc[...] = m_new

    step(k_ref[0, 0:CTX_LEN, :], v_ref[0, 0:CTX_LEN, :])

    def body(i, carry):
        off = pl.multiple_of(CTX_LEN + i * A_CHUNK, 256)
        step(k_ref[0, pl.ds(off, A_CHUNK), :], v_ref[0, pl.ds(off, A_CHUNK), :])
        return carry

    lax.fori_loop(0, jnp.where(t == 0, 0, SEQ // A_CHUNK), body, 0)

    lam = (jnp.exp(jnp.sum(lq1_ref[...] * lk1_ref[...], keepdims=True))
           - jnp.exp(jnp.sum(lq2_ref[...] * lk2_ref[...], keepdims=True)) + lam_init)
    o = acc_sc[...] / l_sc[...]
    o = o[:TM] - lam * o[TM:]
    o_ref[0] = (_rms(o, A_DV) * g_ref[...] * (1.0 - lam_init)).astype(BF16)


def _attn_a_call(qa, ka, va, lam_vecs, subln_g, lam_init, with_ctx):
    t_off = 0 if with_ctx else 1
    nt = N_TILES - t_off
    vec = pl.BlockSpec((1, A_DK), lambda b, h, t: (0, 0))
    return pl.pallas_call(
        functools.partial(_attn_a_kernel, lam_init=lam_init, t_off=t_off),
        grid=(BATCH, A_HEADS, nt),
        in_specs=[
            pl.BlockSpec((1, TM, LANES), lambda b, h, t: (b, t + t_off, h)),
            pl.BlockSpec((1, N_TOK, LANES), lambda b, h, t: (b, 0, h)),
            pl.BlockSpec((1, N_TOK, LANES), lambda b, h, t: (b, 0, h)),
            vec, vec, vec, vec,
            pl.BlockSpec((1, A_DV), lambda b, h, t: (0, 0)),
        ],
        out_specs=pl.BlockSpec((1, TM, LANES), lambda b, h, t: (b, t, h)),
        out_shape=jax.ShapeDtypeStruct((BATCH, nt * TM, A_WIDTH), BF16),
        scratch_shapes=[pltpu.VMEM((2 * TM, 1), F32), pltpu.VMEM((2 * TM, 1), F32),
                        pltpu.VMEM((2 * TM, A_DV), F32)],
        compiler_params=_cparams("arbitrary", "arbitrary", "arbitrary"),
        name="attn_diff",
    )(qa, ka, va, *lam_vecs, subln_g)


def _attn_c_kernel(sink_ref, q_ref, kp_ref, k0_ref, kn_ref, kx_ref, vp_ref, v0_ref, vn_ref, vx_ref,
                   o_ref, *, t_off):
    kvh = pl.program_id(1)
    t = pl.program_id(2) + t_off
    rows = C_GROUP * CBLK
    q4 = jnp.concatenate([q_ref[0, :, g * LANES:(g + 1) * LANES] for g in range(C_GROUP)], axis=0)

    def scores(k_ref):
        return lax.dot_general(q4, k_ref[0], NT_DIMS, preferred_element_type=F32)

    i = lax.broadcasted_iota(jnp.int32, (rows, CBLK), 0) & (CBLK - 1)
    j = lax.broadcasted_iota(jnp.int32, (rows, CBLK), 1)
    n = t - CTX_LEN // CBLK
    n_last = SEQ // CBLK - 1
    far = 4 * CBLK
    off_p = jnp.where(n >= 1, 0, far)
    off_0 = jnp.where(n >= 0, 0, far)
    off_n = jnp.where((n >= 0) & (n < n_last), 0, far)
    s_p = jnp.where(j >= i + off_p, scores(kp_ref), NEG)
    s_0 = jnp.where(j >= off_0, scores(k0_ref), NEG)
    s_n = jnp.where(j + off_n <= i, scores(kn_ref), NEG)
    s_x = scores(kx_ref)

    g_row = lax.broadcasted_iota(jnp.int32, (rows, 1), 0) // CBLK
    sink = jnp.zeros((rows, 1), F32)
    for g in range(C_GROUP):
        sink = jnp.where(g_row == g, sink_ref[kvh * C_GROUP + g], sink)

    rmax = lambda s: jnp.max(s, axis=-1, keepdims=True)
    m = jnp.maximum(jnp.maximum(jnp.maximum(rmax(s_p), rmax(s_0)), jnp.maximum(rmax(s_n), rmax(s_x))), sink)
    l = jnp.exp(sink - m)
    o = jnp.zeros((rows, C_DH), F32)
    for s, v_ref in ((s_p, vp_ref), (s_0, v0_ref), (s_n, vn_ref), (s_x, vx_ref)):
        p = jnp.exp(s - m)
        l = l + jnp.sum(p, axis=-1, keepdims=True)
        o = o + _dot(p.astype(BF16), v_ref[0])
    o = o / l
    for g in range(C_GROUP):
        o_ref[0, :, g * LANES:(g + 1) * LANES] = o[g * CBLK:(g + 1) * CBLK].astype(BF16)


def _attn_c_call(sink, qc, kc, vc, with_ctx):
    t_off = 0 if with_ctx else CTX_LEN // CBLK
    nb = N_CBLK - t_off
    prev = pl.BlockSpec((1, CBLK, LANES), lambda b, k, t, s: (b, jnp.maximum(t + t_off - 1, 0), k))
    own = pl.BlockSpec((1, CBLK, LANES), lambda b, k, t, s: (b, t + t_off, k))
    nxt = pl.BlockSpec((1, CBLK, LANES), lambda b, k, t, s: (b, jnp.minimum(t + t_off + 1, N_CBLK - 1), k))
    ctx = pl.BlockSpec((1, CTX_LEN, LANES), lambda b, k, t, s: (b, 0, k))
    qi = pl.BlockSpec((1, CBLK, C_GROUP * C_DH), lambda b, k, t, s: (b, t + t_off, k))
    qo = pl.BlockSpec((1, CBLK, C_GROUP * C_DH), lambda b, k, t, s: (b, t, k))
    return pl.pallas_call(
        functools.partial(_attn_c_kernel, t_off=t_off),
        grid_spec=pltpu.PrefetchScalarGridSpec(
            num_scalar_prefetch=1,
            grid=(BATCH, C_KV_HEADS, nb),
            in_specs=[qi, prev, own, nxt, ctx, prev, own, nxt, ctx],
            out_specs=qo,
        ),
        out_shape=jax.ShapeDtypeStruct((BATCH, nb * CBLK, C_WIDTH), BF16),
        compiler_params=_cparams("arbitrary", "arbitrary", "arbitrary"),
        name="attn_window",
    )(sink, qc, kc, kc, kc, kc, vc, vc, vc, vc)


HALO = 16


def _pack_bf16_pairs(h):
    hb = h.astype(BF16).astype(F32)
    half = h.shape[-1] // 2
    lo = lax.bitcast_convert_type(hb[:, :half], U32)
    hi = lax.bitcast_convert_type(hb[:, half:], U32)
    return (hi & jnp.uint32(0xFFFF0000)) | (lo >> 16)


def _unpack_bf16_pairs(p):
    lo = lax.bitcast_convert_type(p << 16, F32)
    hi = lax.bitcast_convert_type(p & jnp.uint32(0xFFFF0000), F32)
    return lo, hi


def _out_kernel(*refs, t_off, route):
    (x_ref, oa_ref, bb_ref, u_ref, up_ref, un_ref, oc_ref, w_ref, ga_ref, bg_ref, cg_ref, cw_ref,
     gf_ref, scf_ref, shf_ref) = refs[:15]
    refs = refs[15:]
    if route:
        wr_ref, xo_ref, h2_ref, lg_ref = refs
    else:
        xo_ref, h2_ref = refs
    t = pl.program_id(1) + t_off

    u = u_ref[0].astype(F32)
    row = lax.broadcasted_iota(jnp.int32, (TM, 1), 0)
    prev_ok = jnp.where(t >= 2, 1.0, 0.0)
    next_ok = jnp.where((t >= 1) & (t < N_TILES - 1), 1.0, 0.0)
    prev_row = up_ref[0, HALO - 1:HALO, :].astype(F32) * prev_ok
    next_row = un_ref[0, 0:1, :].astype(F32) * next_ok
    u_prev = jnp.where(row == 0, prev_row, pltpu.roll(u, 1, 0))
    u_next = jnp.where(row == TM - 1, next_row, pltpu.roll(u, TM - 1, 0))
    conv = u_prev * cw_ref[0:1, :] + u * cw_ref[1:2, :] + u_next * cw_ref[2:3, :]
    yb = _rms(bb_ref[0].astype(F32) * conv, B_WIDTH) * bg_ref[...]
    yc = _rms(oc_ref[0].astype(F32), C_WIDTH) * cg_ref[...]
    y = (_dot(oa_ref[0], w_ref[0:A_WIDTH, :])
         + _dot(yb.astype(BF16), w_ref[A_WIDTH:A_WIDTH + B_WIDTH, :])
         + _dot(yc.astype(BF16), w_ref[A_WIDTH + B_WIDTH:, :]))
    x = x_ref[0] + ga_ref[0] * y
    xo_ref[0] = x
    h2 = _rms(x, D_MODEL) * gf_ref[...]
    h2 = h2 * (1.0 + scf_ref[0]) + shf_ref[0]
    if route:
        h2_ref[0] = _pack_bf16_pairs(h2)
        lg_ref[0] = jnp.dot(h2, wr_ref[...], preferred_element_type=F32, precision=lax.Precision.HIGHEST)
    else:
        h2_ref[0] = h2.astype(BF16)


def _out_call(x, oa, bb, u, oc, w, ga, bg, cg, cw, gfn, scf, shf, wr):
    route = wr is not None
    t_off = 1 if route else 0
    nt = N_TILES - t_off
    hpt = TM // HALO
    tile = lambda width: pl.BlockSpec((1, TM, width), lambda b, t: (b, t + t_off, 0))
    otile = lambda width: pl.BlockSpec((1, TM, width), lambda b, t: (b, t, 0))
    modv = pl.BlockSpec((1, 1, D_MODEL), lambda b, t: (_mod_row(b, t + t_off), 0, 0))
    full2 = lambda a: pl.BlockSpec(a.shape, lambda b, t: (0, 0))
    halo_p = pl.BlockSpec((1, HALO, B_WIDTH), lambda b, t: (b, jnp.maximum((t + t_off) * hpt - 1, 0), 0))
    halo_n = pl.BlockSpec((1, HALO, B_WIDTH),
                          lambda b, t: (b, jnp.minimum((t + t_off + 1) * hpt, N_TOK // HALO - 1), 0))
    in_specs = [tile(D_MODEL), otile(A_WIDTH), tile(B_WIDTH), tile(B_WIDTH), halo_p, halo_n, otile(C_WIDTH),
                pl.BlockSpec(w.shape, lambda b, t: (0, 0), pipeline_mode=pl.Buffered(1)),
                modv, full2(bg), full2(cg), full2(cw), full2(gfn), modv, modv]
    args = [x, oa, bb, u, u, u, oc, w, ga, bg, cg, cw, gfn, scf, shf]
    rows_out = nt * TM
    if route:
        in_specs.append(full2(wr))
        args.append(wr)
        out_specs = [otile(D_MODEL), otile(D_MODEL // 2), otile(LANES)]
        out_shape = [jax.ShapeDtypeStruct((BATCH, rows_out, D_MODEL), F32),
                     jax.ShapeDtypeStruct((BATCH, rows_out, D_MODEL // 2), U32),
                     jax.ShapeDtypeStruct((BATCH, rows_out, LANES), F32)]
    else:
        out_specs = [otile(D_MODEL), otile(D_MODEL)]
        out_shape = [jax.ShapeDtypeStruct((BATCH, rows_out, D_MODEL), F32),
                     jax.ShapeDtypeStruct((BATCH, rows_out, D_MODEL), BF16)]
    return pl.pallas_call(
        functools.partial(_out_kernel, t_off=t_off, route=route),
        grid=(BATCH, nt),
        in_specs=in_specs,
        out_specs=out_specs,
        out_shape=out_shape,
        compiler_params=_cparams("arbitrary", "arbitrary"),
        name="mix_out",
    )(*args)


def _swiglu_tile(h, w1, w3, w2):
    a = _dot(h, w1.astype(BF16))
    b = _dot(h, w3.astype(BF16))
    z = a / (1.0 + jnp.exp(-a)) * b
    return _dot(z.astype(BF16), w2.astype(BF16))


def _ffn_kernel(h_ref, w1_ref, w3_ref, w2_ref, y_ref, acc_ref):
    f = pl.program_id(1)
    contrib = _swiglu_tile(h_ref[...], w1_ref[0], w3_ref[0], w2_ref[0])

    @pl.when(f == 0)
    def _():
        acc_ref[...] = contrib

    @pl.when(f > 0)
    def _():
        acc_ref[...] += contrib

    @pl.when(f == pl.num_programs(1) - 1)
    def _():
        y_ref[...] = acc_ref[...].astype(BF16)


def _ffn_call(h, w1, w3, w2):
    rows = h.shape[0]
    return pl.pallas_call(
        _ffn_kernel,
        grid=(rows // FFN_TM, D_FF // FFN_TF),
        in_specs=[
            pl.BlockSpec((FFN_TM, D_MODEL), lambda i, f: (i, 0)),
            pl.BlockSpec((1, D_MODEL, FFN_TF), lambda i, f: (0, 0, f)),
            pl.BlockSpec((1, D_MODEL, FFN_TF), lambda i, f: (0, 0, f)),
            pl.BlockSpec((1, FFN_TF, D_MODEL), lambda i, f: (0, f, 0)),
        ],
        out_specs=pl.BlockSpec((FFN_TM, D_MODEL), lambda i, f: (i, 0)),
        out_shape=jax.ShapeDtypeStruct((rows, D_MODEL), BF16),
        scratch_shapes=[pltpu.VMEM((FFN_TM, D_MODEL), F32)],
        compiler_params=_cparams("arbitrary", "arbitrary"),
        name="ffn_dense",
    )(h, w1, w3, w2)


def _route_kernel(lg_ref, meta_ref, cnt_ref, carry_sc):
    @pl.when(pl.program_id(0) == 0)
    def _():
        carry_sc[...] = jnp.zeros(carry_sc.shape, F32)

    lane = lax.broadcasted_iota(jnp.int32, (ROUTE_TM, LANES), 1)
    lane_f = lane.astype(F32)
    lg = jnp.where(lane < N_EXPERTS, lg_ref[...], -jnp.inf)
    m1 = jnp.max(lg, axis=-1, keepdims=True)
    i1 = jnp.min(jnp.where(lg == m1, lane_f, float(LANES)), axis=-1, keepdims=True)
    oh1 = lane_f == i1
    lg2 = jnp.where(oh1, -jnp.inf, lg)
    m2 = jnp.max(lg2, axis=-1, keepdims=True)
    i2 = jnp.min(jnp.where(lg2 == m2, lane_f, float(LANES)), axis=-1, keepdims=True)
    oh2 = lane_f == i2
    e = jnp.exp(m2 - m1)
    g1 = 1.0 / (1.0 + e)
    g2 = e / (1.0 + e)
    oh = jnp.where(oh1, 1.0, 0.0) + jnp.where(oh2, 1.0, 0.0)
    r = lax.broadcasted_iota(jnp.int32, (ROUTE_TM, ROUTE_TM), 0)
    c = lax.broadcasted_iota(jnp.int32, (ROUTE_TM, ROUTE_TM), 1)
    tri = jnp.where(c < r, 1.0, 0.0).astype(BF16)
    before = _dot(tri, oh.astype(BF16)) + carry_sc[...]
    rank1 = jnp.sum(jnp.where(oh1, before, 0.0), axis=-1, keepdims=True)
    rank2 = jnp.sum(jnp.where(oh2, before, 0.0), axis=-1, keepdims=True)
    meta = jnp.zeros((ROUTE_TM, LANES), F32)
    for k, val in enumerate((i1, i2, g1, g2, rank1, rank2)):
        meta = jnp.where(lane == k, val, meta)
    meta_ref[...] = meta
    carry_sc[...] += jnp.sum(oh, axis=0, keepdims=True)
    cnt_ref[...] = carry_sc[...]


def _route_call(logits):
    rows = logits.shape[0]
    return pl.pallas_call(
        _route_kernel,
        grid=(rows // ROUTE_TM,),
        in_specs=[pl.BlockSpec((ROUTE_TM, LANES), lambda i: (i, 0))],
        out_specs=[pl.BlockSpec((ROUTE_TM, LANES), lambda i: (i, 0)),
                   pl.BlockSpec((1, LANES), lambda i: (0, 0))],
        out_shape=[jax.ShapeDtypeStruct((rows, LANES), F32), jax.ShapeDtypeStruct((1, LANES), F32)],
        scratch_shapes=[pltpu.VMEM((1, LANES), F32)],
        compiler_params=_cparams("arbitrary"),
        name="moe_route",
    )(logits)


def _row_copy(src, src_row, dst, dst_row, sem):
    return pltpu.make_async_copy(src.at[pl.ds(src_row, 1)], dst.at[pl.ds(dst_row, 1)], sem)


def _dispatch_kernel(pos_ref, h_ref, xs_in_ref, xs_ref, sem):
    del xs_in_ref
    base = pl.program_id(0) * DISP_TM

    def start(r, carry):
        for k in range(2):
            _row_copy(h_ref, base + r, xs_ref, pos_ref[2 * (base + r) + k], sem).start()
        return carry

    lax.fori_loop(0, DISP_TM, start, 0)

    def wait(r, carry):
        for k in range(2):
            _row_copy(h_ref, 0, xs_ref, 0, sem).wait()
        return carry

    lax.fori_loop(0, DISP_TM, wait, 0)


def _dispatch_call(pos, h_packed):
    zeros = jnp.zeros((MOE_ROWS, D_MODEL // 2), U32)
    return pl.pallas_call(
        _dispatch_kernel,
        grid_spec=pltpu.PrefetchScalarGridSpec(
            num_scalar_prefetch=1,
            grid=(h_packed.shape[0] // DISP_TM,),
            in_specs=[pl.BlockSpec(memory_space=pl.ANY), pl.BlockSpec(memory_space=pl.ANY)],
            out_specs=pl.BlockSpec(memory_space=pl.ANY),
            scratch_shapes=[pltpu.SemaphoreType.DMA(())],
        ),
        out_shape=jax.ShapeDtypeStruct((MOE_ROWS, D_MODEL // 2), U32),
        input_output_aliases={2: 0},
        compiler_params=_cparams("arbitrary"),
        name="moe_dispatch",
    )(pos, h_packed, zeros)


def _moe_kernel(te_ref, na_ref, x_ref, w1_ref, w3_ref, w2_ref, y_ref, xb_sc, acc_sc):
    i = pl.program_id(0)
    f = pl.program_id(1)
    last = pl.num_programs(1) - 1
    active = i < na_ref[0]

    @pl.when(active & (f == 0))
    def _():
        lo, hi = _unpack_bf16_pairs(x_ref[...])
        xb_sc[:, :D_MODEL // 2] = lo.astype(BF16)
        xb_sc[:, D_MODEL // 2:] = hi.astype(BF16)

    @pl.when(active)
    def _():
        contrib = _swiglu_tile(xb_sc[...], w1_ref[0, 0], w3_ref[0, 0], w2_ref[0, 0])

        @pl.when(f == 0)
        def _():
            acc_sc[...] = contrib

        @pl.when(f > 0)
        def _():
            acc_sc[...] += contrib

    @pl.when(active & (f == last))
    def _():
        y_ref[...] = _pack_bf16_pairs(acc_sc[...])

    @pl.when(jnp.logical_not(active) & (f == last))
    def _():
        y_ref[...] = jnp.zeros(y_ref.shape, U32)


def _moe_call(tile_expert, n_active, xs, w1, w3, w2):
    n_f = D_FF_EXPERT // MOE_TF

    def f_eff(i, f, na):
        return jnp.where(i < na[0], f, n_f - 1)

    return pl.pallas_call(
        _moe_kernel,
        grid_spec=pltpu.PrefetchScalarGridSpec(
            num_scalar_prefetch=2,
            grid=(MOE_TILES, n_f),
            in_specs=[
                pl.BlockSpec((MOE_TM, D_MODEL // 2), lambda i, f, te, na: (i, 0)),
                pl.BlockSpec((1, 1, D_MODEL, MOE_TF), lambda i, f, te, na: (0, te[i], 0, f_eff(i, f, na))),
                pl.BlockSpec((1, 1, D_MODEL, MOE_TF), lambda i, f, te, na: (0, te[i], 0, f_eff(i, f, na))),
                pl.BlockSpec((1, 1, MOE_TF, D_MODEL), lambda i, f, te, na: (0, te[i], f_eff(i, f, na), 0)),
            ],
            out_specs=pl.BlockSpec((MOE_TM, D_MODEL // 2), lambda i, f, te, na: (i, 0)),
            scratch_shapes=[pltpu.VMEM((MOE_TM, D_MODEL), BF16), pltpu.VMEM((MOE_TM, D_MODEL), F32)],
        ),
        out_shape=jax.ShapeDtypeStruct((MOE_ROWS, D_MODEL // 2), U32),
        compiler_params=_cparams("arbitrary", "arbitrary"),
        name="moe_experts",
    )(tile_expert, n_active, xs, w1, w3, w2)


def _combine_kernel(pos_ref, x_ref, meta_ref, gf_ref, ys_ref, o_ref, buf, sem):
    base = pl.program_id(0) * COMB_TM

    def copy(r, k, src_row):
        return pltpu.make_async_copy(ys_ref.at[pl.ds(src_row, 1)], buf.at[k, pl.ds(r, 1)], sem)

    def start(r, carry):
        for k in range(2):
            copy(r, k, pos_ref[2 * (base + r) + k]).start()
        return carry

    lax.fori_loop(0, COMB_TM, start, 0)

    def wait(r, carry):
        for k in range(2):
            copy(r, k, 0).wait()
        return carry

    lax.fori_loop(0, COMB_TM, wait, 0)

    half = D_MODEL // 2
    lo0, hi0 = _unpack_bf16_pairs(buf[0])
    lo1, hi1 = _unpack_bf16_pairs(buf[1])
    g1 = meta_ref[:, 2:3]
    g2 = meta_ref[:, 3:4]
    gf = gf_ref[0]
    o_ref[:, :half] = x_ref[:, :half] + gf[:, :half] * (g1 * lo0 + g2 * lo1)
    o_ref[:, half:] = x_ref[:, half:] + gf[:, half:] * (g1 * hi0 + g2 * hi1)


def _combine_call(pos, x, meta, gf, ys):
    rows = x.shape[0]
    tiles_per_batch = SEQ // COMB_TM
    return pl.pallas_call(
        _combine_kernel,
        grid_spec=pltpu.PrefetchScalarGridSpec(
            num_scalar_prefetch=1,
            grid=(rows // COMB_TM,),
            in_specs=[
                pl.BlockSpec((COMB_TM, D_MODEL), lambda i, p: (i, 0)),
                pl.BlockSpec((COMB_TM, LANES), lambda i, p: (i, 0)),
                pl.BlockSpec((1, 1, D_MODEL), lambda i, p: (i // tiles_per_batch, 0, 0)),
                pl.BlockSpec(memory_space=pl.ANY),
            ],
            out_specs=pl.BlockSpec((COMB_TM, D_MODEL), lambda i, p: (i, 0)),
            scratch_shapes=[pltpu.VMEM((2, COMB_TM, D_MODEL // 2), U32), pltpu.SemaphoreType.DMA(())],
        ),
        out_shape=jax.ShapeDtypeStruct((rows, D_MODEL), F32),
        compiler_params=_cparams("arbitrary"),
        name="moe_combine",
    )(pos, x, meta, gf, ys)


def _moe_layer(x, h_packed, logits, gf, w1, w3, w2):
    meta, cnt = _route_call(logits)
    expert = meta[:, 0:2].astype(jnp.int32)
    rank = meta[:, 4:6].astype(jnp.int32)
    counts = cnt[0, :N_EXPERTS].astype(jnp.int32)
    padded = (counts + MOE_TM - 1) // MOE_TM * MOE_TM
    seg_end = jnp.cumsum(padded)
    seg_start = seg_end - padded
    ids = jnp.arange(N_EXPERTS, dtype=jnp.int32)
    pos = rank + jnp.sum(jnp.where(expert[..., None] == ids, seg_start, 0), axis=-1)
    pos = pos.reshape(-1)
    tile_start = jnp.arange(MOE_TILES, dtype=jnp.int32) * MOE_TM
    tile_expert = jnp.minimum(jnp.sum(tile_start[:, None] >= seg_end[None, :], axis=-1), N_EXPERTS - 1)
    n_active = (seg_end[-1:] // MOE_TM).astype(jnp.int32)
    xs = _dispatch_call(pos, h_packed)
    ys = _moe_call(tile_expert.astype(jnp.int32), n_active, xs, w1, w3, w2)
    return _combine_call(pos, x, meta, gf, ys)


def _rope_tables(dh):
    rows = SEQ // GRID_W
    row = jnp.repeat(jnp.arange(rows, dtype=F32), GRID_W)
    col = jnp.tile(jnp.arange(GRID_W, dtype=F32), rows)
    half = dh // 2
    inv = ROPE_THETA ** (-jnp.arange(0, half, 2, dtype=F32) / half)
    ar = row[:, None] * inv[None, :]
    ac = col[:, None] * inv[None, :]
    ang = jnp.concatenate([ar, ar, ac, ac], axis=-1)
    ang = jnp.concatenate([jnp.zeros((CTX_LEN, dh), F32), ang], axis=0)
    ang = jnp.tile(ang, (1, LANES // dh))
    odd = ((jnp.arange(LANES) // (dh // 4)) % 2 == 1)[None, :]
    sin = jnp.sin(ang)
    return jnp.cos(ang), jnp.where(odd, sin, 0.0), jnp.where(odd, 0.0, -sin)


def kernel(x, c, ctx, c_ctx, ada_w, ada_b, norm_mix_g, norm_ffn_g, w_in, w_out, a_q_norm, a_k_norm,
           a_lam_q1, a_lam_k1, a_lam_q2, a_lam_k2, a_subln_g, b_conv_w, b_out_g, c_q_norm, c_k_norm,
           c_sink, c_out_g, ffn_w1, ffn_w3, ffn_w2, router_w, moe_w1, moe_w3, moe_w2):
    assert x.shape == (BATCH, SEQ, D_MODEL) and ctx.shape == (BATCH, CTX_LEN, D_MODEL)
    cond = jnp.concatenate([c, c_ctx[None, :], jnp.zeros((8 - BATCH - 1, D_MODEL), F32)], axis=0)
    mod = _ada_call(cond, ada_w, ada_b)
    tables = _rope_tables(A_DK) + _rope_tables(C_DH)
    xs = jnp.concatenate([ctx, x], axis=1)
    y_prev = gf_prev = None
    out = None
    for l in range(DEPTH):
        last = l == DEPTH - 1
        sh_a, sc_a, g_a, sh_f, sc_f, g_f = [
            mod[l, :BATCH + 1, k * D_MODEL:(k + 1) * D_MODEL].reshape(BATCH + 1, 1, D_MODEL) for k in range(6)]
        gains = (jnp.tile(a_q_norm[l], LANES // A_DK)[None, :], jnp.tile(a_k_norm[l], LANES // A_DK)[None, :],
                 c_q_norm[l][None, :], c_k_norm[l][None, :])
        res = _in_call(xs, y_prev, gf_prev, norm_mix_g[l][None, :], sc_a, sh_a, w_in[l].astype(BF16),
                       gains, tables)
        if y_prev is not None:
            xs, res = res[0], res[1:]
        qa, ka, va, bb, u, qc, kc, vc = res
        lam_init = 0.8 - 0.6 * math.exp(-0.3 * l)
        lam_vecs = [v[l][None, :] for v in (a_lam_q1, a_lam_k1, a_lam_q2, a_lam_k2)]
        oa = _attn_a_call(qa, ka, va, lam_vecs, a_subln_g[l][None, :], lam_init, with_ctx=not last)
        oc = _attn_c_call(c_sink[l], qc, kc, vc, with_ctx=not last)
        wr = None
        if last:
            wr = jnp.pad(router_w[l // 2], ((0, 0), (0, LANES - N_EXPERTS)))
        res = _out_call(xs, oa, bb, u, oc, w_out[l].astype(BF16), g_a, b_out_g[l][None, :],
                        c_out_g[l][None, :], b_conv_w[l], norm_ffn_g[l][None, :], sc_f, sh_f, wr)
        if not last:
            xs, h2 = res
            y = _ffn_call(h2.reshape(BATCH * N_TOK, D_MODEL), ffn_w1[l // 2:l // 2 + 1],
                          ffn_w3[l // 2:l // 2 + 1], ffn_w2[l // 2:l // 2 + 1])
            y_prev, gf_prev = y.reshape(BATCH, N_TOK, D_MODEL), g_f
        else:
            x_lat, h_packed, logits = res
            i = l // 2
            out = _moe_layer(x_lat.reshape(BATCH * SEQ, D_MODEL), h_packed.reshape(BATCH * SEQ, D_MODEL // 2),
                             logits.reshape(BATCH * SEQ, LANES), g_f[:BATCH],
                             moe_w1[i:i + 1], moe_w3[i:i + 1], moe_w2[i:i + 1])
    return out.reshape(BATCH, SEQ, D_MODEL)
```

```python
import functools
import math

import jax
import jax.numpy as jnp
from jax import lax
from jax.experimental import pallas as pl
from jax.experimental.pallas import tpu as pltpu

F32 = jnp.float32
BF16 = jnp.bfloat16
U32 = jnp.uint32

D_MODEL = 2048
BATCH = 2
SEQ = 4096
DEPTH = 2
GRID_W = 64
CTX_LEN = 256
EPS = 1e-6
ROPE_THETA = 10000.0

A_HEADS = 4
A_DK = 64
A_DV = 128
A_WIDTH = A_HEADS * A_DV
B_WIDTH = 512
C_HEADS = 8
C_KV_HEADS = 2
C_GROUP = C_HEADS // C_KV_HEADS
C_DH = 128
C_WIDTH = C_HEADS * C_DH
C_KV_COLS = C_KV_HEADS * C_DH
WINDOW = 128
MIX_WIDTH = A_WIDTH + B_WIDTH + C_WIDTH
IN_COLS = 3 * A_WIDTH + 3 * B_WIDTH + C_WIDTH + 2 * C_KV_COLS
D_FF = 5632
N_EXPERTS = 8
D_FF_EXPERT = 7168

COL_AQ, COL_AK, COL_AV = 0, 512, 1024
COL_BB, COL_BC, COL_BX = 1536, 2048, 2560
COL_CQ, COL_CK, COL_CV = 3072, 4096, 4352

LANES = 128
N_TOK = CTX_LEN + SEQ
TM = 256
N_TILES = N_TOK // TM
CBLK = 128
N_CBLK = N_TOK // CBLK
A_CHUNK = 1024
A_SPLIT = 4
LOG2E = 1.4426950408889634
FFN_TM = N_TOK // 4
FFN_TF = 256
FFN_SPLIT = 4
MOE_SUB = 256
MOE_WIN = 1024
MOE_TF = 256
MOE_ROWS = 2 * BATCH * SEQ + N_EXPERTS * MOE_SUB
MOE_NWIN = MOE_ROWS // MOE_WIN
MOE_PAIRS = MOE_NWIN + N_EXPERTS
MOE_GATHER_STEPS = 8
MOE_GATHER_ROWS = MOE_WIN // MOE_GATHER_STEPS
ROUTE_TM = 512
COMB_TM = 256
VMEM_LIMIT = 56 * 1024 * 1024
NEG = -1e30

NT_DIMS = (((1,), (1,)), ((), ()))


def _cparams(*sem):
    return pltpu.CompilerParams(dimension_semantics=sem, vmem_limit_bytes=VMEM_LIMIT)


def _dot(a, b):
    return jnp.dot(a, b, preferred_element_type=F32)


def _rms(x, width):
    return x * lax.rsqrt(jnp.sum(x * x, axis=-1, keepdims=True) * (1.0 / width) + EPS)


ADA_TN = 1024


def _ada_kernel(c_ref, w_ref, b_ref, o_ref):
    a = c_ref[...]
    a = a / (1.0 + jnp.exp(-a))
    o_ref[0] = _dot(a.astype(BF16), w_ref[0].astype(BF16)) + b_ref[0]


def _ada_call(cond, ada_w, ada_b):
    n = 6 * D_MODEL
    return pl.pallas_call(
        _ada_kernel,
        grid=(DEPTH, n // ADA_TN),
        in_specs=[
            pl.BlockSpec((8, D_MODEL), lambda l, j: (0, 0)),
            pl.BlockSpec((1, D_MODEL, ADA_TN), lambda l, j: (l, 0, j)),
            pl.BlockSpec((1, 1, ADA_TN), lambda l, j: (l, 0, j)),
        ],
        out_specs=pl.BlockSpec((1, 8, ADA_TN), lambda l, j: (l, 0, j)),
        out_shape=jax.ShapeDtypeStruct((DEPTH, 8, n), F32),
        compiler_params=_cparams("arbitrary", "arbitrary"),
        name="ada_mod",
    )(cond, ada_w, ada_b.reshape(DEPTH, 1, n))


def _rope128(n, cos, s_odd, s_even, quarter):
    return n * cos + pltpu.roll(n, quarter, 1) * s_odd + pltpu.roll(n, LANES - quarter, 1) * s_even


def _in_kernel(*refs, with_y):
    if with_y:
        x_ref, y_ref, gf_ref = refs[:3]
        refs = refs[3:]
    else:
        x_ref = refs[0]
        refs = refs[1:]
    (g_ref, sc_ref, sh_ref, w_ref, aqn_ref, akn_ref, cqn_ref, ckn_ref,
     ca_ref, sa1_ref, sa2_ref, cc_ref, sc1_ref, sc2_ref) = refs[:14]
    outs = refs[14:]
    if with_y:
        xo_ref, outs = outs[0], outs[1:]
    qa_ref, ka_ref, va_ref, bb_ref, u_ref, qc_ref, kc_ref, vc_ref = outs

    x = x_ref[0]
    if with_y:
        x = x + gf_ref[0] * y_ref[0].astype(F32)
        xo_ref[0] = x
    h = _rms(x, D_MODEL) * g_ref[...]
    h = h * (1.0 + sc_ref[0]) + sh_ref[0]
    hb = h.astype(BF16)

    lane = lax.broadcasted_iota(jnp.int32, (TM, LANES), 1)
    lo_half = lane < A_DK

    def head_a(col0, j, gain_ref, scale):
        c0 = col0 + j * LANES
        p = _dot(hb, w_ref[:, c0:c0 + LANES])
        p2 = p * p
        s_all = jnp.sum(p2, axis=-1, keepdims=True)
        s_lo = jnp.sum(jnp.where(lo_half, p2, 0.0), axis=-1, keepdims=True)
        ms = jnp.where(lo_half, s_lo, s_all - s_lo) * (1.0 / A_DK)
        n = p * lax.rsqrt(ms + EPS) * gain_ref[...]
        r = _rope128(n, ca_ref[...], sa1_ref[...], sa2_ref[...], A_DK // 4)
        return (r * scale).astype(BF16)

    def head_c(col0, j, gain_ref, scale):
        c0 = col0 + j * LANES
        p = _dot(hb, w_ref[:, c0:c0 + LANES])
        n = _rms(p, C_DH) * gain_ref[...]
        r = _rope128(n, cc_ref[...], sc1_ref[...], sc2_ref[...], C_DH // 4)
        return (r * scale).astype(BF16)

    for j in range(A_WIDTH // LANES):
        sl = slice(j * LANES, (j + 1) * LANES)
        qa_ref[0, :, sl] = head_a(COL_AQ, j, aqn_ref, A_DK ** -0.5 * LOG2E)
        ka_ref[0, :, sl] = head_a(COL_AK, j, akn_ref, 1.0)
    va_ref[0] = _dot(hb, w_ref[:, COL_AV:COL_AV + A_WIDTH]).astype(BF16)
    bb_ref[0] = _dot(hb, w_ref[:, COL_BB:COL_BB + B_WIDTH]).astype(BF16)
    u = _dot(hb, w_ref[:, COL_BC:COL_BC + B_WIDTH]) * _dot(hb, w_ref[:, COL_BX:COL_BX + B_WIDTH])
    u_ref[0] = u.astype(BF16)
    for j in range(C_HEADS):
        qc_ref[0, :, j * LANES:(j + 1) * LANES] = head_c(COL_CQ, j, cqn_ref, C_DH ** -0.5)
    for j in range(C_KV_HEADS):
        kc_ref[0, :, j * LANES:(j + 1) * LANES] = head_c(COL_CK, j, ckn_ref, 1.0)
    vc_ref[0] = _dot(hb, w_ref[:, COL_CV:COL_CV + C_KV_COLS]).astype(BF16)


def _mod_row(b, t):
    return jnp.where(t == 0, BATCH, b)


def _in_call(x, y, gf, g, sc, sh, w, gains, tables):
    with_y = y is not None
    tile = lambda width: pl.BlockSpec((1, TM, width), lambda b, t: (b, t, 0))
    modv = pl.BlockSpec((1, 1, D_MODEL), lambda b, t: (_mod_row(b, t), 0, 0))
    full2 = lambda a: pl.BlockSpec(a.shape, lambda b, t: (0, 0))
    rows = pl.BlockSpec((TM, LANES), lambda b, t: (t, 0))
    in_specs = [tile(D_MODEL)]
    args = [x]
    if with_y:
        in_specs += [tile(D_MODEL), modv]
        args += [y, gf]
    in_specs += [full2(g), modv, modv,
                 pl.BlockSpec(w.shape, lambda b, t: (0, 0), pipeline_mode=pl.Buffered(1))]
    args += [g, sc, sh, w]
    in_specs += [full2(a) for a in gains]
    args += list(gains)
    in_specs += [rows] * 6
    args += list(tables)
    widths = (A_WIDTH, A_WIDTH, A_WIDTH, B_WIDTH, B_WIDTH, C_WIDTH, C_KV_COLS, C_KV_COLS)
    out_specs = [tile(wd) for wd in widths]
    out_shape = [jax.ShapeDtypeStruct((BATCH, N_TOK, wd), BF16) for wd in widths]
    if with_y:
        out_specs = [tile(D_MODEL)] + out_specs
        out_shape = [jax.ShapeDtypeStruct((BATCH, N_TOK, D_MODEL), F32)] + out_shape
    return pl.pallas_call(
        functools.partial(_in_kernel, with_y=with_y),
        grid=(BATCH, N_TILES),
        in_specs=in_specs,
        out_specs=out_specs,
        out_shape=out_shape,
        compiler_params=_cparams("arbitrary", "arbitrary"),
        name="in_proj",
    )(*args)


def _attn_a_kernel(q_ref, k_ref, v_ref, lq1_ref, lk1_ref, lq2_ref, lk2_ref, g_ref, o_ref,
                   m_sc, acc_sc, *, lam_init, t_off):
    t = pl.program_id(2) + t_off
    q = q_ref[0]
    lane = lax.broadcasted_iota(jnp.int32, (TM, LANES), 1)
    zero = jnp.zeros_like(q)
    qs = jnp.concatenate([jnp.where(lane < A_DK, q, zero), jnp.where(lane >= A_DK, q, zero)], axis=0)

    m_sc[...] = jnp.full(m_sc.shape, NEG, F32)
    acc_sc[...] = jnp.zeros(acc_sc.shape, F32)

    def step(k, v):
        v1 = jnp.concatenate([v, jnp.ones_like(v)], axis=1)
        grp = 2 * TM // A_SPLIT
        for g in range(A_SPLIT):
            rows = slice(g * grp, (g + 1) * grp)
            s = lax.dot_general(qs[rows], k, NT_DIMS, preferred_element_type=F32)
            m_prev = m_sc[rows, :]
            m_new = jnp.maximum(m_prev, jnp.max(s, axis=-1, keepdims=True))
            alpha = jnp.exp2(m_prev - m_new)
            p = jnp.exp2(s - jnp.tile(m_new, (1, s.shape[1] // LANES)))
            acc_sc[rows, :] = jnp.tile(alpha, (1, 2)) * acc_sc[rows, :] + _dot(p.astype(BF16), v1)
            m_sc[rows, :] = m_new

    step(k_ref[0, 0:CTX_LEN, :], v_ref[0, 0:CTX_LEN, :])

    def body(i, carry):
        off = pl.multiple_of(CTX_LEN + i * A_CHUNK, 256)
        step(k_ref[0, pl.ds(off, A_CHUNK), :], v_ref[0, pl.ds(off, A_CHUNK), :])
        return carry

    lax.fori_loop(0, jnp.where(t == 0, 0, SEQ // A_CHUNK), body, 0)

    lam = (jnp.exp(jnp.sum(lq1_ref[...] * lk1_ref[...], keepdims=True))
           - jnp.exp(jnp.sum(lq2_ref[...] * lk2_ref[...], keepdims=True)) + lam_init)
    acc = acc_sc[...]
    o = acc[:, :A_DV] / acc[:, A_DV:]
    o = o[:TM] - lam * o[TM:]
    o_ref[0] = (_rms(o, A_DV) * g_ref[...] * (1.0 - lam_init)).astype(BF16)


def _attn_a_call(qa, ka, va, lam_vecs, subln_g, lam_init, with_ctx):
    t_off = 0 if with_ctx else 1
    nt = N_TILES - t_off
    vec = pl.BlockSpec((1, A_DK), lambda b, h, t: (0, 0))
    return pl.pallas_call(
        functools.partial(_attn_a_kernel, lam_init=lam_init, t_off=t_off),
        grid=(BATCH, A_HEADS, nt),
        in_specs=[
            pl.BlockSpec((1, TM, LANES), lambda b, h, t: (b, t + t_off, h)),
            pl.BlockSpec((1, N_TOK, LANES), lambda b, h, t: (b, 0, h)),
            pl.BlockSpec((1, N_TOK, LANES), lambda b, h, t: (b, 0, h)),
            vec, vec, vec, vec,
            pl.BlockSpec((1, A_DV), lambda b, h, t: (0, 0)),
        ],
        out_specs=pl.BlockSpec((1, TM, LANES), lambda b, h, t: (b, t, h)),
        out_shape=jax.ShapeDtypeStruct((BATCH, nt * TM, A_WIDTH), BF16),
        scratch_shapes=[pltpu.VMEM((2 * TM, LANES), F32), pltpu.VMEM((2 * TM, 2 * A_DV), F32)],
        compiler_params=_cparams("arbitrary", "arbitrary", "arbitrary"),
        name="attn_diff",
    )(qa, ka, va, *lam_vecs, subln_g)


def _attn_c_kernel(sink_ref, q_ref, kp_ref, k0_ref, kn_ref, kx_ref, vp_ref, v0_ref, vn_ref, vx_ref,
                   o_ref, *, t_off):
    kvh = pl.program_id(1)
    t = pl.program_id(2) + t_off
    rows = C_GROUP * CBLK
    q4 = jnp.concatenate([q_ref[0, :, g * LANES:(g + 1) * LANES] for g in range(C_GROUP)], axis=0)

    def scores(k_ref):
        return lax.dot_general(q4, k_ref[0], NT_DIMS, preferred_element_type=F32)

    i = lax.broadcasted_iota(jnp.int32, (rows, CBLK), 0) & (CBLK - 1)
    j = lax.broadcasted_iota(jnp.int32, (rows, CBLK), 1)
    n = t - CTX_LEN // CBLK
    n_last = SEQ // CBLK - 1
    far = 4 * CBLK
    off_p = jnp.where(n >= 1, 0, far)
    off_0 = jnp.where(n >= 0, 0, far)
    off_n = jnp.where((n >= 0) & (n < n_last), 0, far)
    s_p = jnp.where(j >= i + off_p, scores(kp_ref), NEG)
    s_0 = jnp.where(j >= off_0, scores(k0_ref), NEG)
    s_n = jnp.where(j + off_n <= i, scores(kn_ref), NEG)
    s_x = scores(kx_ref)

    g_row = lax.broadcasted_iota(jnp.int32, (rows, 1), 0) // CBLK
    sink = jnp.zeros((rows, 1), F32)
    for g in range(C_GROUP):
        sink = jnp.where(g_row == g, sink_ref[kvh * C_GROUP + g], sink)

    rmax = lambda s: jnp.max(s, axis=-1, keepdims=True)
    m = jnp.maximum(jnp.maximum(jnp.maximum(rmax(s_p), rmax(s_0)), jnp.maximum(rmax(s_n), rmax(s_x))), sink)
    l = jnp.exp(sink - m)
    o = jnp.zeros((rows, C_DH), F32)
    for s, v_ref in ((s_p, vp_ref), (s_0, v0_ref), (s_n, vn_ref), (s_x, vx_ref)):
        p = jnp.exp(s - m)
        l = l + jnp.sum(p, axis=-1, keepdims=True)
        o = o + _dot(p.astype(BF16), v_ref[0])
    o = o / l
    for g in range(C_GROUP):
        o_ref[0, :, g * LANES:(g + 1) * LANES] = o[g * CBLK:(g + 1) * CBLK].astype(BF16)


def _attn_c_call(sink, qc, kc, vc, with_ctx):
    t_off = 0 if with_ctx else CTX_LEN // CBLK
    nb = N_CBLK - t_off
    prev = pl.BlockSpec((1, CBLK, LANES), lambda b, k, t, s: (b, jnp.maximum(t + t_off - 1, 0), k))
    own = pl.BlockSpec((1, CBLK, LANES), lambda b, k, t, s: (b, t + t_off, k))
    nxt = pl.BlockSpec((1, CBLK, LANES), lambda b, k, t, s: (b, jnp.minimum(t + t_off + 1, N_CBLK - 1), k))
    ctx = pl.BlockSpec((1, CTX_LEN, LANES), lambda b, k, t, s: (b, 0, k))
    qi = pl.BlockSpec((1, CBLK, C_GROUP * C_DH), lambda b, k, t, s: (b, t + t_off, k))
    qo = pl.BlockSpec((1, CBLK, C_GROUP * C_DH), lambda b, k, t, s: (b, t, k))
    return pl.pallas_call(
        functools.partial(_attn_c_kernel, t_off=t_off),
        grid_spec=pltpu.PrefetchScalarGridSpec(
            num_scalar_prefetch=1,
            grid=(BATCH, C_KV_HEADS, nb),
            in_specs=[qi, prev, own, nxt, ctx, prev, own, nxt, ctx],
            out_specs=qo,
        ),
        out_shape=jax.ShapeDtypeStruct((BATCH, nb * CBLK, C_WIDTH), BF16),
        compiler_params=_cparams("arbitrary", "arbitrary", "arbitrary"),
        name="attn_window",
    )(sink, qc, kc, kc, kc, kc, vc, vc, vc, vc)


HALO = 16


def _pack_bf16_pairs(h):
    hb = h.astype(BF16).astype(F32)
    half = h.shape[-1] // 2
    lo = lax.bitcast_convert_type(hb[:, :half], U32)
    hi = lax.bitcast_convert_type(hb[:, half:], U32)
    return (hi & jnp.uint32(0xFFFF0000)) | (lo >> 16)


def _unpack_bf16_pairs(p):
    lo = lax.bitcast_convert_type(p << 16, F32)
    hi = lax.bitcast_convert_type(p & jnp.uint32(0xFFFF0000), F32)
    return lo, hi


def _out_kernel(*refs, t_off, route):
    (x_ref, oa_ref, bb_ref, u_ref, up_ref, un_ref, oc_ref, w_ref, ga_ref, bg_ref, cg_ref, cw_ref,
     gf_ref, scf_ref, shf_ref) = refs[:15]
    refs = refs[15:]
    if route:
        wr_ref, xo_ref, h2_ref, lg_ref = refs
    else:
        xo_ref, h2_ref = refs
    t = pl.program_id(1) + t_off

    u = u_ref[0].astype(F32)
    row = lax.broadcasted_iota(jnp.int32, (TM, 1), 0)
    prev_ok = jnp.where(t >= 2, 1.0, 0.0)
    next_ok = jnp.where((t >= 1) & (t < N_TILES - 1), 1.0, 0.0)
    prev_row = up_ref[0, HALO - 1:HALO, :].astype(F32) * prev_ok
    next_row = un_ref[0, 0:1, :].astype(F32) * next_ok
    u_prev = jnp.where(row == 0, prev_row, pltpu.roll(u, 1, 0))
    u_next = jnp.where(row == TM - 1, next_row, pltpu.roll(u, TM - 1, 0))
    conv = u_prev * cw_ref[0:1, :] + u * cw_ref[1:2, :] + u_next * cw_ref[2:3, :]
    yb = _rms(bb_ref[0].astype(F32) * conv, B_WIDTH) * bg_ref[...]
    yc = _rms(oc_ref[0].astype(F32), C_WIDTH) * cg_ref[...]
    y = (_dot(oa_ref[0], w_ref[0:A_WIDTH, :])
         + _dot(yb.astype(BF16), w_ref[A_WIDTH:A_WIDTH + B_WIDTH, :])
         + _dot(yc.astype(BF16), w_ref[A_WIDTH + B_WIDTH:, :]))
    x = x_ref[0] + ga_ref[0] * y
    xo_ref[0] = x
    h2 = _rms(x, D_MODEL) * gf_ref[...]
    h2 = h2 * (1.0 + scf_ref[0]) + shf_ref[0]
    if route:
        h2_ref[0] = _pack_bf16_pairs(h2)
        lg_ref[0] = jnp.dot(h2, wr_ref[...], preferred_element_type=F32, precision=lax.Precision.HIGHEST)
    else:
        h2_ref[0] = h2.astype(BF16)


def _out_call(x, oa, bb, u, oc, w, ga, bg, cg, cw, gfn, scf, shf, wr):
    route = wr is not None
    t_off = 1 if route else 0
    nt = N_TILES - t_off
    hpt = TM // HALO
    tile = lambda width: pl.BlockSpec((1, TM, width), lambda b, t: (b, t + t_off, 0))
    otile = lambda width: pl.BlockSpec((1, TM, width), lambda b, t: (b, t, 0))
    modv = pl.BlockSpec((1, 1, D_MODEL), lambda b, t: (_mod_row(b, t + t_off), 0, 0))
    full2 = lambda a: pl.BlockSpec(a.shape, lambda b, t: (0, 0))
    halo_p = pl.BlockSpec((1, HALO, B_WIDTH), lambda b, t: (b, jnp.maximum((t + t_off) * hpt - 1, 0), 0))
    halo_n = pl.BlockSpec((1, HALO, B_WIDTH),
                          lambda b, t: (b, jnp.minimum((t + t_off + 1) * hpt, N_TOK // HALO - 1), 0))
    in_specs = [tile(D_MODEL), otile(A_WIDTH), tile(B_WIDTH), tile(B_WIDTH), halo_p, halo_n, otile(C_WIDTH),
                pl.BlockSpec(w.shape, lambda b, t: (0, 0), pipeline_mode=pl.Buffered(1)),
                modv, full2(bg), full2(cg), full2(cw), full2(gfn), modv, modv]
    args = [x, oa, bb, u, u, u, oc, w, ga, bg, cg, cw, gfn, scf, shf]
    rows_out = nt * TM
    if route:
        in_specs.append(full2(wr))
        args.append(wr)
        out_specs = [otile(D_MODEL), otile(D_MODEL // 2), otile(LANES)]
        out_shape = [jax.ShapeDtypeStruct((BATCH, rows_out, D_MODEL), F32),
                     jax.ShapeDtypeStruct((BATCH, rows_out, D_MODEL // 2), U32),
                     jax.ShapeDtypeStruct((BATCH, rows_out, LANES), F32)]
    else:
        out_specs = [otile(D_MODEL), otile(D_MODEL)]
        out_shape = [jax.ShapeDtypeStruct((BATCH, rows_out, D_MODEL), F32),
                     jax.ShapeDtypeStruct((BATCH, rows_out, D_MODEL), BF16)]
    return pl.pallas_call(
        functools.partial(_out_kernel, t_off=t_off, route=route),
        grid=(BATCH, nt),
        in_specs=in_specs,
        out_specs=out_specs,
        out_shape=out_shape,
        compiler_params=_cparams("arbitrary", "arbitrary"),
        name="mix_out",
    )(*args)


def _swiglu_accumulate(h_ref, acc_ref, row_blocks, w1, w3, w2):
    for rows in row_blocks:
        h = h_ref[rows, :]
        a = _dot(h, w1)
        b = _dot(h, w3)
        z = a / (1.0 + jnp.exp(-a)) * b
        acc_ref[rows, :] += _dot(z.astype(BF16), w2)


def _ffn_kernel(h_ref, w1_ref, w3_ref, w2_ref, y_ref, acc_ref):
    f = pl.program_id(1)

    @pl.when(f == 0)
    def _():
        acc_ref[...] = jnp.zeros(acc_ref.shape, F32)

    sub = FFN_TM // FFN_SPLIT
    _swiglu_accumulate(h_ref, acc_ref, [slice(i * sub, (i + 1) * sub) for i in range(FFN_SPLIT)],
                       w1_ref[0].astype(BF16), w3_ref[0].astype(BF16), w2_ref[0].astype(BF16))

    @pl.when(f == pl.num_programs(1) - 1)
    def _():
        y_ref[...] = acc_ref[...].astype(BF16)


def _ffn_call(h, w1, w3, w2):
    rows = h.shape[0]
    return pl.pallas_call(
        _ffn_kernel,
        grid=(rows // FFN_TM, D_FF // FFN_TF),
        in_specs=[
            pl.BlockSpec((FFN_TM, D_MODEL), lambda i, f: (i, 0)),
            pl.BlockSpec((1, D_MODEL, FFN_TF), lambda i, f: (0, 0, f)),
            pl.BlockSpec((1, D_MODEL, FFN_TF), lambda i, f: (0, 0, f)),
            pl.BlockSpec((1, FFN_TF, D_MODEL), lambda i, f: (0, f, 0)),
        ],
        out_specs=pl.BlockSpec((FFN_TM, D_MODEL), lambda i, f: (i, 0)),
        out_shape=jax.ShapeDtypeStruct((rows, D_MODEL), BF16),
        scratch_shapes=[pltpu.VMEM((FFN_TM, D_MODEL), F32)],
        compiler_params=_cparams("arbitrary", "arbitrary"),
        name="ffn_dense",
    )(h, w1, w3, w2)


def _route_kernel(lg_ref, meta_ref, cnt_ref, carry_sc):
    @pl.when(pl.program_id(0) == 0)
    def _():
        carry_sc[...] = jnp.zeros(carry_sc.shape, F32)

    lane = lax.broadcasted_iota(jnp.int32, (ROUTE_TM, LANES), 1)
    lane_f = lane.astype(F32)
    lg = jnp.where(lane < N_EXPERTS, lg_ref[...], -jnp.inf)
    m1 = jnp.max(lg, axis=-1, keepdims=True)
    i1 = jnp.min(jnp.where(lg == m1, lane_f, float(LANES)), axis=-1, keepdims=True)
    oh1 = lane_f == i1
    lg2 = jnp.where(oh1, -jnp.inf, lg)
    m2 = jnp.max(lg2, axis=-1, keepdims=True)
    i2 = jnp.min(jnp.where(lg2 == m2, lane_f, float(LANES)), axis=-1, keepdims=True)
    oh2 = lane_f == i2
    e = jnp.exp(m2 - m1)
    g1 = 1.0 / (1.0 + e)
    g2 = e / (1.0 + e)
    oh = jnp.where(oh1, 1.0, 0.0) + jnp.where(oh2, 1.0, 0.0)
    r = lax.broadcasted_iota(jnp.int32, (ROUTE_TM, ROUTE_TM), 0)
    c = lax.broadcasted_iota(jnp.int32, (ROUTE_TM, ROUTE_TM), 1)
    tri = jnp.where(c < r, 1.0, 0.0).astype(BF16)
    before = _dot(tri, oh.astype(BF16)) + carry_sc[...]
    rank1 = jnp.sum(jnp.where(oh1, before, 0.0), axis=-1, keepdims=True)
    rank2 = jnp.sum(jnp.where(oh2, before, 0.0), axis=-1, keepdims=True)
    meta = jnp.zeros((ROUTE_TM, LANES), F32)
    for k, val in enumerate((i1, i2, g1, g2, rank1, rank2)):
        meta = jnp.where(lane == k, val, meta)
    meta_ref[...] = meta
    carry_sc[...] += jnp.sum(oh, axis=0, keepdims=True)
    cnt_ref[...] = carry_sc[...]


def _route_call(logits):
    rows = logits.shape[0]
    return pl.pallas_call(
        _route_kernel,
        grid=(rows // ROUTE_TM,),
        in_specs=[pl.BlockSpec((ROUTE_TM, LANES), lambda i: (i, 0))],
        out_specs=[pl.BlockSpec((ROUTE_TM, LANES), lambda i: (i, 0)),
                   pl.BlockSpec((1, LANES), lambda i: (0, 0))],
        out_shape=[jax.ShapeDtypeStruct((rows, LANES), F32), jax.ShapeDtypeStruct((1, LANES), F32)],
        scratch_shapes=[pltpu.VMEM((1, LANES), F32)],
        compiler_params=_cparams("arbitrary"),
        name="moe_route",
    )(logits)


INV_TM = 512


def _invert_kernel(pos_ref, inv_ref):
    i = pl.program_id(0)
    n_clear = MOE_ROWS // INV_TM

    @pl.when(i < n_clear)
    def _():
        def clear(r, carry):
            inv_ref[i * INV_TM + r] = 0
            return carry
        lax.fori_loop(0, INV_TM, clear, 0)

    @pl.when(i >= n_clear)
    def _():
        base = (i - n_clear) * INV_TM

        def scatter(r, carry):
            inv_ref[pos_ref[base + r]] = (base + r) >> 1
            return carry
        lax.fori_loop(0, INV_TM, scatter, 0)


def _invert_call(pos):
    return pl.pallas_call(
        _invert_kernel,
        grid_spec=pltpu.PrefetchScalarGridSpec(
            num_scalar_prefetch=1,
            grid=((MOE_ROWS + pos.shape[0]) // INV_TM,),
            in_specs=[],
            out_specs=pl.BlockSpec(memory_space=pltpu.SMEM),
        ),
        out_shape=jax.ShapeDtypeStruct((MOE_ROWS,), jnp.int32),
        compiler_params=_cparams("arbitrary"),
        name="moe_invert",
    )(pos)


def _moe_kernel(inv_sc, pw_ref, pe_ref, pd_ref, plo_ref, phi_ref, pfl_ref,
                h_ref, w1_ref, w3_ref, w2_ref, y_ref, xg_sc, xb_sc, acc_sc, sem):
    p = pl.program_id(0)
    f = pl.program_id(1)
    n_f = pl.num_programs(1)
    w = pw_ref[p]
    lo = plo_ref[p]
    hi = phi_ref[p]
    first = (pfl_ref[p] & 1) != 0
    lastp = (pfl_ref[p] & 2) != 0
    real = pe_ref[p] < N_EXPERTS
    half = D_MODEL // 2

    def row_copy(src_row, r):
        return pltpu.make_async_copy(h_ref.at[pl.ds(src_row, 1)], xg_sc.at[pl.ds(r, 1)], sem)

    def gather(win, r0, n):
        def body(r, carry):
            row_copy(inv_sc[win * MOE_WIN + r], r).start()
            return carry
        lax.fori_loop(r0, r0 + n, body, 0)

    @pl.when((p == 0) & (f == 0))
    def _():
        gather(0, 0, MOE_WIN)

    @pl.when(first & (f == 0))
    def _():
        def wait(r, carry):
            row_copy(0, r).wait()
            return carry
        lax.fori_loop(0, MOE_WIN, wait, 0)
        for sub in range(MOE_WIN // MOE_SUB):
            rows = slice(sub * MOE_SUB, (sub + 1) * MOE_SUB)
            x_lo, x_hi = _unpack_bf16_pairs(xg_sc[rows, :])
            xb_sc[rows, :half] = x_lo.astype(BF16)
            xb_sc[rows, half:] = x_hi.astype(BF16)
        y_ref[...] = jnp.zeros(y_ref.shape, U32)

    @pl.when(lastp & (f < MOE_GATHER_STEPS) & (w + 1 < MOE_NWIN))
    def _():
        gather(w + 1, f * MOE_GATHER_ROWS, MOE_GATHER_ROWS)

    subs = [slice(i * MOE_SUB, (i + 1) * MOE_SUB) for i in range(MOE_WIN // MOE_SUB)]
    full = (lo == 0) & (hi == len(subs))

    @pl.when(real & (f == 0))
    def _():
        acc_sc[...] = jnp.zeros(acc_sc.shape, F32)

    def weights():
        return w1_ref[0, 0].astype(BF16), w3_ref[0, 0].astype(BF16), w2_ref[0, 0].astype(BF16)

    @pl.when(real & full)
    def _():
        _swiglu_accumulate(xb_sc, acc_sc, subs, *weights())

    @pl.when(real & jnp.logical_not(full))
    def _():
        w = weights()
        for sub, rows in enumerate(subs):
            @pl.when((sub >= lo) & (sub < hi))
            def _():
                _swiglu_accumulate(xb_sc, acc_sc, [rows], *w)

    @pl.when(real & (f == n_f - 1))
    def _():
        for sub, rows in enumerate(subs):
            @pl.when((sub >= lo) & (sub < hi))
            def _():
                y_ref[rows, :] = _pack_bf16_pairs(acc_sc[rows, :])


def _moe_call(inv, pair_tables, h_packed, w1, w3, w2):
    n_f = D_FF_EXPERT // MOE_TF

    def f_eff(p, f, pe):
        return jnp.where(pe[p] < N_EXPERTS, f, n_f - 1)

    w13 = pl.BlockSpec((1, 1, D_MODEL, MOE_TF),
                       lambda p, f, inv, pw, pe, pd, plo, phi, pfl: (0, pd[p], 0, f_eff(p, f, pe)))
    w2s = pl.BlockSpec((1, 1, MOE_TF, D_MODEL),
                       lambda p, f, inv, pw, pe, pd, plo, phi, pfl: (0, pd[p], f_eff(p, f, pe), 0))
    return pl.pallas_call(
        _moe_kernel,
        grid_spec=pltpu.PrefetchScalarGridSpec(
            num_scalar_prefetch=7,
            grid=(MOE_PAIRS, n_f),
            in_specs=[pl.BlockSpec(memory_space=pl.ANY), w13, w13, w2s],
            out_specs=pl.BlockSpec((MOE_WIN, D_MODEL // 2),
                                   lambda p, f, inv, pw, pe, pd, plo, phi, pfl: (pw[p], 0)),
            scratch_shapes=[pltpu.VMEM((MOE_WIN, D_MODEL // 2), U32),
                            pltpu.VMEM((MOE_WIN, D_MODEL), BF16),
                            pltpu.VMEM((MOE_WIN, D_MODEL), F32),
                            pltpu.SemaphoreType.DMA(())],
        ),
        out_shape=jax.ShapeDtypeStruct((MOE_ROWS, D_MODEL // 2), U32),
        compiler_params=_cparams("arbitrary", "arbitrary"),
        name="moe_experts",
    )(inv, *pair_tables, h_packed, w1, w3, w2)


def _pair_schedule(counts):
    spw = MOE_WIN // MOE_SUB
    nsub = (counts + MOE_SUB - 1) // MOE_SUB
    seg_end = jnp.cumsum(nsub)
    cuts = jnp.sort(jnp.concatenate([jnp.arange(MOE_NWIN + 1, dtype=jnp.int32) * spw, seg_end]))
    s, t = cuts[:-1], cuts[1:]
    big = MOE_NWIN * spw + 1
    order = jnp.argsort(jnp.where(t > s, s, big))
    s, t = s[order], t[order]
    nonempty = t > s
    expert = jnp.sum(seg_end[None, :] <= s[:, None], axis=1).astype(jnp.int32)
    expert = jnp.where(nonempty, expert, N_EXPERTS)
    real = expert < N_EXPERTS
    window = jnp.where(nonempty, s // spw, MOE_NWIN - 1)
    lo = jnp.where(nonempty, s - window * spw, 0)
    hi = jnp.where(nonempty, t - window * spw, 0)
    flags = jnp.where(nonempty, (lo == 0) * 1 + (hi == spw) * 2, 0)
    dma_expert = jnp.where(real, expert, jnp.max(jnp.where(real, expert, 0)))
    i32 = lambda a: a.astype(jnp.int32)
    return (seg_end - nsub) * MOE_SUB, tuple(i32(a) for a in (window, expert, dma_expert, lo, hi, flags))


def _combine_kernel(pos_ref, x_ref, meta_ref, gf_ref, ys_ref, o_ref, buf, sem):
    base = pl.program_id(0) * COMB_TM

    def copy(r, k, src_row):
        return pltpu.make_async_copy(ys_ref.at[pl.ds(src_row, 1)], buf.at[k, pl.ds(r, 1)], sem)

    def start(r, carry):
        for k in range(2):
            copy(r, k, pos_ref[2 * (base + r) + k]).start()
        return carry

    lax.fori_loop(0, COMB_TM, start, 0)

    def wait(r, carry):
        for k in range(2):
            copy(r, k, 0).wait()
        return carry

    lax.fori_loop(0, COMB_TM, wait, 0)

    half = D_MODEL // 2
    lo0, hi0 = _unpack_bf16_pairs(buf[0])
    lo1, hi1 = _unpack_bf16_pairs(buf[1])
    g1 = meta_ref[:, 2:3]
    g2 = meta_ref[:, 3:4]
    gf = gf_ref[0]
    o_ref[:, :half] = x_ref[:, :half] + gf[:, :half] * (g1 * lo0 + g2 * lo1)
    o_ref[:, half:] = x_ref[:, half:] + gf[:, half:] * (g1 * hi0 + g2 * hi1)


def _combine_call(pos, x, meta, gf, ys):
    rows = x.shape[0]
    tiles_per_batch = SEQ // COMB_TM
    return pl.pallas_call(
        _combine_kernel,
        grid_spec=pltpu.PrefetchScalarGridSpec(
            num_scalar_prefetch=1,
            grid=(rows // COMB_TM,),
            in_specs=[
                pl.BlockSpec((COMB_TM, D_MODEL), lambda i, p: (i, 0)),
                pl.BlockSpec((COMB_TM, LANES), lambda i, p: (i, 0)),
                pl.BlockSpec((1, 1, D_MODEL), lambda i, p: (i // tiles_per_batch, 0, 0)),
                pl.BlockSpec(memory_space=pl.ANY),
            ],
            out_specs=pl.BlockSpec((COMB_TM, D_MODEL), lambda i, p: (i, 0)),
            scratch_shapes=[pltpu.VMEM((2, COMB_TM, D_MODEL // 2), U32), pltpu.SemaphoreType.DMA(())],
        ),
        out_shape=jax.ShapeDtypeStruct((rows, D_MODEL), F32),
        compiler_params=_cparams("arbitrary"),
        name="moe_combine",
    )(pos, x, meta, gf, ys)


def _moe_layer(x, h_packed, logits, gf, w1, w3, w2):
    meta, cnt = _route_call(logits)
    expert = meta[:, 0:2].astype(jnp.int32)
    rank = meta[:, 4:6].astype(jnp.int32)
    counts = cnt[0, :N_EXPERTS].astype(jnp.int32)
    seg_start, pair_tables = _pair_schedule(counts)
    ids = jnp.arange(N_EXPERTS, dtype=jnp.int32)
    pos = rank + jnp.sum(jnp.where(expert[..., None] == ids, seg_start, 0), axis=-1)
    pos = pos.reshape(-1).astype(jnp.int32)
    ys = _moe_call(_invert_call(pos), pair_tables, h_packed, w1, w3, w2)
    return _combine_call(pos, x, meta, gf, ys)


def _rope_tables(dh):
    rows = SEQ // GRID_W
    row = jnp.repeat(jnp.arange(rows, dtype=F32), GRID_W)
    col = jnp.tile(jnp.arange(GRID_W, dtype=F32), rows)
    half = dh // 2
    inv = ROPE_THETA ** (-jnp.arange(0, half, 2, dtype=F32) / half)
    ar = row[:, None] * inv[None, :]
    ac = col[:, None] * inv[None, :]
    ang = jnp.concatenate([ar, ar, ac, ac], axis=-1)
    ang = jnp.concatenate([jnp.zeros((CTX_LEN, dh), F32), ang], axis=0)
    ang = jnp.tile(ang, (1, LANES // dh))
    odd = ((jnp.arange(LANES) // (dh // 4)) % 2 == 1)[None, :]
    sin = jnp.sin(ang)
    return jnp.cos(ang), jnp.where(odd, sin, 0.0), jnp.where(odd, 0.0, -sin)


def kernel(x, c, ctx, c_ctx, ada_w, ada_b, norm_mix_g, norm_ffn_g, w_in, w_out, a_q_norm, a_k_norm,
           a_lam_q1, a_lam_k1, a_lam_q2, a_lam_k2, a_subln_g, b_conv_w, b_out_g, c_q_norm, c_k_norm,
           c_sink, c_out_g, ffn_w1, ffn_w3, ffn_w2, router_w, moe_w1, moe_w3, moe_w2):
    assert x.shape == (BATCH, SEQ, D_MODEL) and ctx.shape == (BATCH, CTX_LEN, D_MODEL)
    cond = jnp.concatenate([c, c_ctx[None, :], jnp.zeros((8 - BATCH - 1, D_MODEL), F32)], axis=0)
    mod = _ada_call(cond, ada_w, ada_b)
    tables = _rope_tables(A_DK) + _rope_tables(C_DH)
    xs = jnp.concatenate([ctx, x], axis=1)
    y_prev = gf_prev = None
    out = None
    for l in range(DEPTH):
        last = l == DEPTH - 1
        sh_a, sc_a, g_a, sh_f, sc_f, g_f = [
            mod[l, :BATCH + 1, k * D_MODEL:(k + 1) * D_MODEL].reshape(BATCH + 1, 1, D_MODEL) for k in range(6)]
        gains = (jnp.tile(a_q_norm[l], LANES // A_DK)[None, :], jnp.tile(a_k_norm[l], LANES // A_DK)[None, :],
                 c_q_norm[l][None, :], c_k_norm[l][None, :])
        res = _in_call(xs, y_prev, gf_prev, norm_mix_g[l][None, :], sc_a, sh_a, w_in[l].astype(BF16),
                       gains, tables)
        if y_prev is not None:
            xs, res = res[0], res[1:]
        qa, ka, va, bb, u, qc, kc, vc = res
        lam_init = 0.8 - 0.6 * math.exp(-0.3 * l)
        lam_vecs = [v[l][None, :] for v in (a_lam_q1, a_lam_k1, a_lam_q2, a_lam_k2)]
        oa = _attn_a_call(qa, ka, va, lam_vecs, a_subln_g[l][None, :], lam_init, with_ctx=not last)
        oc = _attn_c_call(c_sink[l], qc, kc, vc, with_ctx=not last)
        wr = None
        if last:
            wr = jnp.pad(router_w[l // 2], ((0, 0), (0, LANES - N_EXPERTS)))
        res = _out_call(xs, oa, bb, u, oc, w_out[l].astype(BF16), g_a, b_out_g[l][None, :],
                        c_out_g[l][None, :], b_conv_w[l], norm_ffn_g[l][None, :], sc_f, sh_f, wr)
        if not last:
            xs, h2 = res
            y = _ffn_call(h2.reshape(BATCH * N_TOK, D_MODEL), ffn_w1[l // 2:l // 2 + 1],
                          ffn_w3[l // 2:l // 2 + 1], ffn_w2[l // 2:l // 2 + 1])
            y_prev, gf_prev = y.reshape(BATCH, N_TOK, D_MODEL), g_f
        else:
            x_lat, h_packed, logits = res
            i = l // 2
            out = _moe_layer(x_lat.reshape(BATCH * SEQ, D_MODEL), h_packed.reshape(BATCH * SEQ, D_MODEL // 2),
                             logits.reshape(BATCH * SEQ, LANES), g_f[:BATCH],
                             moe_w1[i:i + 1], moe_w3[i:i + 1], moe_w2[i:i + 1])
    return out.reshape(BATCH, SEQ, D_MODEL)
```

```python
import functools
import math

import jax
import jax.numpy as jnp
from jax import lax
from jax.experimental import pallas as pl
from jax.experimental.pallas import tpu as pltpu

F32 = jnp.float32
BF16 = jnp.bfloat16
U32 = jnp.uint32

D_MODEL = 2048
BATCH = 2
SEQ = 4096
DEPTH = 2
GRID_W = 64
CTX_LEN = 256
EPS = 1e-6
ROPE_THETA = 10000.0

A_HEADS = 4
A_DK = 64
A_DV = 128
A_WIDTH = A_HEADS * A_DV
B_WIDTH = 512
C_HEADS = 8
C_KV_HEADS = 2
C_GROUP = C_HEADS // C_KV_HEADS
C_DH = 128
C_WIDTH = C_HEADS * C_DH
C_KV_COLS = C_KV_HEADS * C_DH
WINDOW = 128
MIX_WIDTH = A_WIDTH + B_WIDTH + C_WIDTH
IN_COLS = 3 * A_WIDTH + 3 * B_WIDTH + C_WIDTH + 2 * C_KV_COLS
D_FF = 5632
N_EXPERTS = 8
D_FF_EXPERT = 7168

COL_AQ, COL_AK, COL_AV = 0, 512, 1024
COL_BB, COL_BC, COL_BX = 1536, 2048, 2560
COL_CQ, COL_CK, COL_CV = 3072, 4096, 4352

LANES = 128
N_TOK = CTX_LEN + SEQ
TM = 256
N_TILES = N_TOK // TM
IN_TN = 512
CBLK = 128
N_CBLK = N_TOK // CBLK
A_CHUNK = 1024
A_SPLIT = 4
LOG2E = 1.4426950408889634
FFN_TM = N_TOK // 4
FFN_TF = 256
FFN_SPLIT = 4
MOE_SUB = 256
MOE_NSUB = 8
MOE_TF = 256
MOE_ROWS = 2 * BATCH * SEQ + N_EXPERTS * MOE_SUB
MOE_PAIRS = (MOE_ROWS // MOE_SUB + N_EXPERTS * (MOE_NSUB - 1)) // MOE_NSUB
ROUTE_TM = 512
COMB_TM = 256
VMEM_LIMIT = 56 * 1024 * 1024
NEG = -1e30

NT_DIMS = (((1,), (1,)), ((), ()))


def _cparams(*sem):
    return pltpu.CompilerParams(dimension_semantics=sem, vmem_limit_bytes=VMEM_LIMIT)


def _dot(a, b):
    return jnp.dot(a, b, preferred_element_type=F32)


def _rms(x, width):
    return x * lax.rsqrt(jnp.sum(x * x, axis=-1, keepdims=True) * (1.0 / width) + EPS)


ADA_TN = 1024


def _ada_kernel(c_ref, w_ref, b_ref, o_ref):
    a = c_ref[...]
    a = a / (1.0 + jnp.exp(-a))
    o_ref[0] = _dot(a.astype(BF16), w_ref[0].astype(BF16)) + b_ref[0]


def _ada_call(cond, ada_w, ada_b):
    n = 6 * D_MODEL
    return pl.pallas_call(
        _ada_kernel,
        grid=(DEPTH, n // ADA_TN),
        in_specs=[
            pl.BlockSpec((8, D_MODEL), lambda l, j: (0, 0)),
            pl.BlockSpec((1, D_MODEL, ADA_TN), lambda l, j: (l, 0, j)),
            pl.BlockSpec((1, 1, ADA_TN), lambda l, j: (l, 0, j)),
        ],
        out_specs=pl.BlockSpec((1, 8, ADA_TN), lambda l, j: (l, 0, j)),
        out_shape=jax.ShapeDtypeStruct((DEPTH, 8, n), F32),
        compiler_params=_cparams("arbitrary", "arbitrary"),
        name="ada_mod",
    )(cond, ada_w, ada_b.reshape(DEPTH, 1, n))


def _rope128(n, cos, s_odd, s_even, quarter):
    return n * cos + pltpu.roll(n, quarter, 1) * s_odd + pltpu.roll(n, LANES - quarter, 1) * s_even


def _in_kernel(*refs, with_y):
    if with_y:
        x_ref, y_ref, gf_ref = refs[:3]
        refs = refs[3:]
    else:
        x_ref = refs[0]
        refs = refs[1:]
    (g_ref, sc_ref, sh_ref, w_ref, aqn_ref, akn_ref, cqn_ref, ckn_ref,
     ca_ref, sa1_ref, sa2_ref, cc_ref, sc1_ref, sc2_ref) = refs[:14]
    outs = refs[14:]
    if with_y:
        xo_ref, outs = outs[0], outs[1:]
    qa_ref, ka_ref, va_ref, bb_ref, u_ref, qc_ref, kc_ref, vc_ref = outs

    x = x_ref[0]
    if with_y:
        x = x + gf_ref[0] * y_ref[0].astype(F32)
        xo_ref[0] = x
    h = _rms(x, D_MODEL) * g_ref[...]
    h = h * (1.0 + sc_ref[0]) + sh_ref[0]
    hb = h.astype(BF16)

    lane = lax.broadcasted_iota(jnp.int32, (TM, LANES), 1)
    lo_half = lane < A_DK

    def proj(col0, width):
        return _dot(hb, w_ref[:, col0:col0 + width])

    def head_a(p, gain_ref, scale):
        p2 = p * p
        s_all = jnp.sum(p2, axis=-1, keepdims=True)
        s_lo = jnp.sum(jnp.where(lo_half, p2, 0.0), axis=-1, keepdims=True)
        ms = jnp.where(lo_half, s_lo, s_all - s_lo) * (1.0 / A_DK)
        n = p * lax.rsqrt(ms + EPS) * gain_ref[...]
        r = _rope128(n, ca_ref[...], sa1_ref[...], sa2_ref[...], A_DK // 4)
        return (r * scale).astype(BF16)

    def head_c(p, gain_ref, scale):
        n = _rms(p, C_DH) * gain_ref[...]
        r = _rope128(n, cc_ref[...], sc1_ref[...], sc2_ref[...], C_DH // 4)
        return (r * scale).astype(BF16)

    def heads(o_ref, col0, width, fn, gain_ref, scale):
        for c in range(0, width, IN_TN):
            wd = min(IN_TN, width - c)
            p = proj(col0 + c, wd)
            for j in range(wd // LANES):
                o_ref[0, :, c + j * LANES:c + (j + 1) * LANES] = fn(
                    p[:, j * LANES:(j + 1) * LANES], gain_ref, scale)

    heads(qa_ref, COL_AQ, A_WIDTH, head_a, aqn_ref, A_DK ** -0.5 * LOG2E)
    heads(ka_ref, COL_AK, A_WIDTH, head_a, akn_ref, 1.0)
    va_ref[0] = proj(COL_AV, A_WIDTH).astype(BF16)
    bb_ref[0] = proj(COL_BB, B_WIDTH).astype(BF16)
    u_ref[0] = (proj(COL_BC, B_WIDTH) * proj(COL_BX, B_WIDTH)).astype(BF16)
    heads(qc_ref, COL_CQ, C_WIDTH, head_c, cqn_ref, C_DH ** -0.5)
    heads(kc_ref, COL_CK, C_KV_COLS, head_c, ckn_ref, 1.0)
    vc_ref[0] = proj(COL_CV, C_KV_COLS).astype(BF16)


def _mod_row(b, t):
    return jnp.where(t == 0, BATCH, b)


def _in_call(x, y, gf, g, sc, sh, w, gains, tables):
    with_y = y is not None
    tile = lambda width: pl.BlockSpec((1, TM, width), lambda b, t: (b, t, 0))
    modv = pl.BlockSpec((1, 1, D_MODEL), lambda b, t: (_mod_row(b, t), 0, 0))
    full2 = lambda a: pl.BlockSpec(a.shape, lambda b, t: (0, 0))
    rows = pl.BlockSpec((TM, LANES), lambda b, t: (t, 0))
    in_specs = [tile(D_MODEL)]
    args = [x]
    if with_y:
        in_specs += [tile(D_MODEL), modv]
        args += [y, gf]
    in_specs += [full2(g), modv, modv,
                 pl.BlockSpec(w.shape, lambda b, t: (0, 0), pipeline_mode=pl.Buffered(1))]
    args += [g, sc, sh, w]
    in_specs += [full2(a) for a in gains]
    args += list(gains)
    in_specs += [rows] * 6
    args += list(tables)
    widths = (A_WIDTH, A_WIDTH, A_WIDTH, B_WIDTH, B_WIDTH, C_WIDTH, C_KV_COLS, C_KV_COLS)
    out_specs = [tile(wd) for wd in widths]
    out_shape = [jax.ShapeDtypeStruct((BATCH, N_TOK, wd), BF16) for wd in widths]
    if with_y:
        out_specs = [tile(D_MODEL)] + out_specs
        out_shape = [jax.ShapeDtypeStruct((BATCH, N_TOK, D_MODEL), F32)] + out_shape
    return pl.pallas_call(
        functools.partial(_in_kernel, with_y=with_y),
        grid=(BATCH, N_TILES),
        in_specs=in_specs,
        out_specs=out_specs,
        out_shape=out_shape,
        compiler_params=_cparams("arbitrary", "arbitrary"),
        name="in_proj",
    )(*args)


def _attn_a_kernel(q_ref, k_ref, v_ref, lq1_ref, lk1_ref, lq2_ref, lk2_ref, g_ref, o_ref,
                   m_sc, acc_sc, *, lam_init, t_off):
    t = pl.program_id(2) + t_off
    q = q_ref[0]
    lane = lax.broadcasted_iota(jnp.int32, (TM, LANES), 1)
    zero = jnp.zeros_like(q)
    qs = jnp.concatenate([jnp.where(lane < A_DK, q, zero), jnp.where(lane >= A_DK, q, zero)], axis=0)

    m_sc[...] = jnp.full(m_sc.shape, NEG, F32)
    acc_sc[...] = jnp.zeros(acc_sc.shape, F32)

    def step(k, v):
        v1 = jnp.concatenate([v, jnp.ones_like(v)], axis=1)
        grp = 2 * TM // A_SPLIT
        for g in range(A_SPLIT):
            rows = slice(g * grp, (g + 1) * grp)
            s = lax.dot_general(qs[rows], k, NT_DIMS, preferred_element_type=F32)
            m_prev = m_sc[rows, :]
            m_new = jnp.maximum(m_prev, jnp.max(s, axis=-1, keepdims=True))
            alpha = jnp.exp2(m_prev - m_new)
            p = jnp.exp2(s - jnp.tile(m_new, (1, s.shape[1] // LANES)))
            acc_sc[rows, :] = jnp.tile(alpha, (1, 2)) * acc_sc[rows, :] + _dot(p.astype(BF16), v1)
            m_sc[rows, :] = m_new

    step(k_ref[0, 0:CTX_LEN, :], v_ref[0, 0:CTX_LEN, :])

    def body(i, carry):
        off = pl.multiple_of(CTX_LEN + i * A_CHUNK, 256)
        step(k_ref[0, pl.ds(off, A_CHUNK), :], v_ref[0, pl.ds(off, A_CHUNK), :])
        return carry

    lax.fori_loop(0, jnp.where(t == 0, 0, SEQ // A_CHUNK), body, 0)

    lam = (jnp.exp(jnp.sum(lq1_ref[...] * lk1_ref[...], keepdims=True))
           - jnp.exp(jnp.sum(lq2_ref[...] * lk2_ref[...], keepdims=True)) + lam_init)
    acc = acc_sc[...]
    o = acc[:, :A_DV] / acc[:, A_DV:]
    o = o[:TM] - lam * o[TM:]
    o_ref[0] = (_rms(o, A_DV) * g_ref[...] * (1.0 - lam_init)).astype(BF16)


def _attn_a_call(qa, ka, va, lam_vecs, subln_g, lam_init, with_ctx):
    t_off = 0 if with_ctx else 1
    nt = N_TILES - t_off
    vec = pl.BlockSpec((1, A_DK), lambda b, h, t: (0, 0))
    return pl.pallas_call(
        functools.partial(_attn_a_kernel, lam_init=lam_init, t_off=t_off),
        grid=(BATCH, A_HEADS, nt),
        in_specs=[
            pl.BlockSpec((1, TM, LANES), lambda b, h, t: (b, t + t_off, h)),
            pl.BlockSpec((1, N_TOK, LANES), lambda b, h, t: (b, 0, h)),
            pl.BlockSpec((1, N_TOK, LANES), lambda b, h, t: (b, 0, h)),
            vec, vec, vec, vec,
            pl.BlockSpec((1, A_DV), lambda b, h, t: (0, 0)),
        ],
        out_specs=pl.BlockSpec((1, TM, LANES), lambda b, h, t: (b, t, h)),
        out_shape=jax.ShapeDtypeStruct((BATCH, nt * TM, A_WIDTH), BF16),
        scratch_shapes=[pltpu.VMEM((2 * TM, LANES), F32), pltpu.VMEM((2 * TM, 2 * A_DV), F32)],
        compiler_params=_cparams("arbitrary", "arbitrary", "arbitrary"),
        name="attn_diff",
    )(qa, ka, va, *lam_vecs, subln_g)


def _attn_c_kernel(sink_ref, q_ref, kp_ref, k0_ref, kn_ref, kx_ref, vp_ref, v0_ref, vn_ref, vx_ref,
                   o_ref, *, t_off):
    kvh = pl.program_id(1)
    t = pl.program_id(2) + t_off
    rows = C_GROUP * CBLK
    q4 = jnp.concatenate([q_ref[0, :, g * LANES:(g + 1) * LANES] for g in range(C_GROUP)], axis=0)

    def scores(k_ref):
        return lax.dot_general(q4, k_ref[0], NT_DIMS, preferred_element_type=F32)

    i = lax.broadcasted_iota(jnp.int32, (rows, CBLK), 0) & (CBLK - 1)
    j = lax.broadcasted_iota(jnp.int32, (rows, CBLK), 1)
    n = t - CTX_LEN // CBLK
    n_last = SEQ // CBLK - 1
    far = 4 * CBLK
    off_p = jnp.where(n >= 1, 0, far)
    off_0 = jnp.where(n >= 0, 0, far)
    off_n = jnp.where((n >= 0) & (n < n_last), 0, far)
    s_p = jnp.where(j >= i + off_p, scores(kp_ref), NEG)
    s_0 = jnp.where(j >= off_0, scores(k0_ref), NEG)
    s_n = jnp.where(j + off_n <= i, scores(kn_ref), NEG)
    s_x = scores(kx_ref)

    g_row = lax.broadcasted_iota(jnp.int32, (rows, 1), 0) // CBLK
    sink = jnp.zeros((rows, 1), F32)
    for g in range(C_GROUP):
        sink = jnp.where(g_row == g, sink_ref[kvh * C_GROUP + g], sink)

    rmax = lambda s: jnp.max(s, axis=-1, keepdims=True)
    m = jnp.maximum(jnp.maximum(jnp.maximum(rmax(s_p), rmax(s_0)), jnp.maximum(rmax(s_n), rmax(s_x))), sink)
    l = jnp.exp(sink - m)
    o = jnp.zeros((rows, C_DH), F32)
    for s, v_ref in ((s_p, vp_ref), (s_0, v0_ref), (s_n, vn_ref), (s_x, vx_ref)):
        p = jnp.exp(s - m)
        l = l + jnp.sum(p, axis=-1, keepdims=True)
        o = o + _dot(p.astype(BF16), v_ref[0])
    o = o / l
    for g in range(C_GROUP):
        o_ref[0, :, g * LANES:(g + 1) * LANES] = o[g * CBLK:(g + 1) * CBLK].astype(BF16)


def _attn_c_call(sink, qc, kc, vc, with_ctx):
    t_off = 0 if with_ctx else CTX_LEN // CBLK
    nb = N_CBLK - t_off
    prev = pl.BlockSpec((1, CBLK, LANES), lambda b, k, t, s: (b, jnp.maximum(t + t_off - 1, 0), k))
    own = pl.BlockSpec((1, CBLK, LANES), lambda b, k, t, s: (b, t + t_off, k))
    nxt = pl.BlockSpec((1, CBLK, LANES), lambda b, k, t, s: (b, jnp.minimum(t + t_off + 1, N_CBLK - 1), k))
    ctx = pl.BlockSpec((1, CTX_LEN, LANES), lambda b, k, t, s: (b, 0, k))
    qi = pl.BlockSpec((1, CBLK, C_GROUP * C_DH), lambda b, k, t, s: (b, t + t_off, k))
    qo = pl.BlockSpec((1, CBLK, C_GROUP * C_DH), lambda b, k, t, s: (b, t, k))
    return pl.pallas_call(
        functools.partial(_attn_c_kernel, t_off=t_off),
        grid_spec=pltpu.PrefetchScalarGridSpec(
            num_scalar_prefetch=1,
            grid=(BATCH, C_KV_HEADS, nb),
            in_specs=[qi, prev, own, nxt, ctx, prev, own, nxt, ctx],
            out_specs=qo,
        ),
        out_shape=jax.ShapeDtypeStruct((BATCH, nb * CBLK, C_WIDTH), BF16),
        compiler_params=_cparams("arbitrary", "arbitrary", "arbitrary"),
        name="attn_window",
    )(sink, qc, kc, kc, kc, kc, vc, vc, vc, vc)


HALO = 16


def _pack_bf16_pairs(h):
    hb = h.astype(BF16).astype(F32)
    half = h.shape[-1] // 2
    lo = lax.bitcast_convert_type(hb[:, :half], U32)
    hi = lax.bitcast_convert_type(hb[:, half:], U32)
    return (hi & jnp.uint32(0xFFFF0000)) | (lo >> 16)


def _unpack_bf16_pairs(p):
    lo = lax.bitcast_convert_type(p << 16, F32)
    hi = lax.bitcast_convert_type(p & jnp.uint32(0xFFFF0000), F32)
    return lo, hi


def _out_kernel(*refs, t_off, route):
    (x_ref, oa_ref, bb_ref, u_ref, up_ref, un_ref, oc_ref, w_ref, ga_ref, bg_ref, cg_ref, cw_ref,
     gf_ref, scf_ref, shf_ref) = refs[:15]
    refs = refs[15:]
    if route:
        wrh_ref, wrl_ref, xo_ref, h2_ref, lg_ref = refs
    else:
        xo_ref, h2_ref = refs
    t = pl.program_id(1) + t_off

    u = u_ref[0].astype(F32)
    row = lax.broadcasted_iota(jnp.int32, (TM, 1), 0)
    prev_ok = jnp.where(t >= 2, 1.0, 0.0)
    next_ok = jnp.where((t >= 1) & (t < N_TILES - 1), 1.0, 0.0)
    prev_row = up_ref[0, HALO - 1:HALO, :].astype(F32) * prev_ok
    next_row = un_ref[0, 0:1, :].astype(F32) * next_ok
    u_prev = jnp.where(row == 0, prev_row, pltpu.roll(u, 1, 0))
    u_next = jnp.where(row == TM - 1, next_row, pltpu.roll(u, TM - 1, 0))
    conv = u_prev * cw_ref[0:1, :] + u * cw_ref[1:2, :] + u_next * cw_ref[2:3, :]
    yb = _rms(bb_ref[0].astype(F32) * conv, B_WIDTH) * bg_ref[...]
    yc = _rms(oc_ref[0].astype(F32), C_WIDTH) * cg_ref[...]
    y = (_dot(oa_ref[0], w_ref[0:A_WIDTH, :])
         + _dot(yb.astype(BF16), w_ref[A_WIDTH:A_WIDTH + B_WIDTH, :])
         + _dot(yc.astype(BF16), w_ref[A_WIDTH + B_WIDTH:, :]))
    x = x_ref[0] + ga_ref[0] * y
    xo_ref[0] = x
    h2 = _rms(x, D_MODEL) * gf_ref[...]
    h2 = h2 * (1.0 + scf_ref[0]) + shf_ref[0]
    if route:
        h2_ref[0] = _pack_bf16_pairs(h2)
        h_hi = h2.astype(BF16)
        h_lo = (h2 - h_hi.astype(F32)).astype(BF16)
        lg_ref[0] = _dot(h_hi, wrh_ref[...]) + (_dot(h_lo, wrh_ref[...]) + _dot(h_hi, wrl_ref[...]))
    else:
        h2_ref[0] = h2.astype(BF16)


def _out_call(x, oa, bb, u, oc, w, ga, bg, cg, cw, gfn, scf, shf, wr):
    route = wr is not None
    t_off = 1 if route else 0
    nt = N_TILES - t_off
    hpt = TM // HALO
    tile = lambda width: pl.BlockSpec((1, TM, width), lambda b, t: (b, t + t_off, 0))
    otile = lambda width: pl.BlockSpec((1, TM, width), lambda b, t: (b, t, 0))
    modv = pl.BlockSpec((1, 1, D_MODEL), lambda b, t: (_mod_row(b, t + t_off), 0, 0))
    full2 = lambda a: pl.BlockSpec(a.shape, lambda b, t: (0, 0))
    halo_p = pl.BlockSpec((1, HALO, B_WIDTH), lambda b, t: (b, jnp.maximum((t + t_off) * hpt - 1, 0), 0))
    halo_n = pl.BlockSpec((1, HALO, B_WIDTH),
                          lambda b, t: (b, jnp.minimum((t + t_off + 1) * hpt, N_TOK // HALO - 1), 0))
    in_specs = [tile(D_MODEL), otile(A_WIDTH), tile(B_WIDTH), tile(B_WIDTH), halo_p, halo_n, otile(C_WIDTH),
                pl.BlockSpec(w.shape, lambda b, t: (0, 0), pipeline_mode=pl.Buffered(1)),
                modv, full2(bg), full2(cg), full2(cw), full2(gfn), modv, modv]
    args = [x, oa, bb, u, u, u, oc, w, ga, bg, cg, cw, gfn, scf, shf]
    rows_out = nt * TM
    if route:
        wr_hi = wr.astype(BF16)
        wr_lo = (wr - wr_hi.astype(F32)).astype(BF16)
        in_specs += [full2(wr_hi), full2(wr_lo)]
        args += [wr_hi, wr_lo]
        out_specs = [otile(D_MODEL), otile(D_MODEL // 2), otile(LANES)]
        out_shape = [jax.ShapeDtypeStruct((BATCH, rows_out, D_MODEL), F32),
                     jax.ShapeDtypeStruct((BATCH, rows_out, D_MODEL // 2), U32),
                     jax.ShapeDtypeStruct((BATCH, rows_out, LANES), F32)]
    else:
        out_specs = [otile(D_MODEL), otile(D_MODEL)]
        out_shape = [jax.ShapeDtypeStruct((BATCH, rows_out, D_MODEL), F32),
                     jax.ShapeDtypeStruct((BATCH, rows_out, D_MODEL), BF16)]
    return pl.pallas_call(
        functools.partial(_out_kernel, t_off=t_off, route=route),
        grid=(BATCH, nt),
        in_specs=in_specs,
        out_specs=out_specs,
        out_shape=out_shape,
        compiler_params=_cparams("arbitrary", "arbitrary"),
        name="mix_out",
    )(*args)


def _swiglu_accumulate(h_ref, acc_ref, row_blocks, w1, w3, w2):
    for rows in row_blocks:
        h = h_ref[rows, :]
        a = _dot(h, w1)
        b = _dot(h, w3)
        z = a / (1.0 + jnp.exp(-a)) * b
        acc_ref[rows, :] += _dot(z.astype(BF16), w2)


def _ffn_kernel(h_ref, w1_ref, w3_ref, w2_ref, y_ref, acc_ref):
    f = pl.program_id(1)

    @pl.when(f == 0)
    def _():
        acc_ref[...] = jnp.zeros(acc_ref.shape, F32)

    sub = FFN_TM // FFN_SPLIT
    _swiglu_accumulate(h_ref, acc_ref, [slice(i * sub, (i + 1) * sub) for i in range(FFN_SPLIT)],
                       w1_ref[0].astype(BF16), w3_ref[0].astype(BF16), w2_ref[0].astype(BF16))

    @pl.when(f == pl.num_programs(1) - 1)
    def _():
        y_ref[...] = acc_ref[...].astype(BF16)


def _ffn_call(h, w1, w3, w2):
    rows = h.shape[0]
    return pl.pallas_call(
        _ffn_kernel,
        grid=(rows // FFN_TM, D_FF // FFN_TF),
        in_specs=[
            pl.BlockSpec((FFN_TM, D_MODEL), lambda i, f: (i, 0)),
            pl.BlockSpec((1, D_MODEL, FFN_TF), lambda i, f: (0, 0, f)),
            pl.BlockSpec((1, D_MODEL, FFN_TF), lambda i, f: (0, 0, f)),
            pl.BlockSpec((1, FFN_TF, D_MODEL), lambda i, f: (0, f, 0)),
        ],
        out_specs=pl.BlockSpec((FFN_TM, D_MODEL), lambda i, f: (i, 0)),
        out_shape=jax.ShapeDtypeStruct((rows, D_MODEL), BF16),
        scratch_shapes=[pltpu.VMEM((FFN_TM, D_MODEL), F32)],
        compiler_params=_cparams("arbitrary", "arbitrary"),
        name="ffn_dense",
    )(h, w1, w3, w2)


def _route_kernel(lg_ref, meta_ref, cnt_ref, carry_sc):
    @pl.when(pl.program_id(0) == 0)
    def _():
        carry_sc[...] = jnp.zeros(carry_sc.shape, F32)

    lane = lax.broadcasted_iota(jnp.int32, (ROUTE_TM, LANES), 1)
    lane_f = lane.astype(F32)
    lg = jnp.where(lane < N_EXPERTS, lg_ref[...], -jnp.inf)
    m1 = jnp.max(lg, axis=-1, keepdims=True)
    i1 = jnp.min(jnp.where(lg == m1, lane_f, float(LANES)), axis=-1, keepdims=True)
    oh1 = lane_f == i1
    lg2 = jnp.where(oh1, -jnp.inf, lg)
    m2 = jnp.max(lg2, axis=-1, keepdims=True)
    i2 = jnp.min(jnp.where(lg2 == m2, lane_f, float(LANES)), axis=-1, keepdims=True)
    oh2 = lane_f == i2
    e = jnp.exp(m2 - m1)
    g1 = 1.0 / (1.0 + e)
    g2 = e / (1.0 + e)
    oh = jnp.where(oh1, 1.0, 0.0) + jnp.where(oh2, 1.0, 0.0)
    r = lax.broadcasted_iota(jnp.int32, (ROUTE_TM, ROUTE_TM), 0)
    c = lax.broadcasted_iota(jnp.int32, (ROUTE_TM, ROUTE_TM), 1)
    tri = jnp.where(c < r, 1.0, 0.0).astype(BF16)
    before = _dot(tri, oh.astype(BF16)) + carry_sc[...]
    rank1 = jnp.sum(jnp.where(oh1, before, 0.0), axis=-1, keepdims=True)
    rank2 = jnp.sum(jnp.where(oh2, before, 0.0), axis=-1, keepdims=True)
    meta = jnp.zeros((ROUTE_TM, LANES), F32)
    for k, val in enumerate((i1, i2, g1, g2, rank1, rank2)):
        meta = jnp.where(lane == k, val, meta)
    meta_ref[...] = meta
    carry_sc[...] += jnp.sum(oh, axis=0, keepdims=True)
    cnt_ref[...] = carry_sc[...]


def _route_call(logits):
    rows = logits.shape[0]
    return pl.pallas_call(
        _route_kernel,
        grid=(rows // ROUTE_TM,),
        in_specs=[pl.BlockSpec((ROUTE_TM, LANES), lambda i: (i, 0))],
        out_specs=[pl.BlockSpec((ROUTE_TM, LANES), lambda i: (i, 0)),
                   pl.BlockSpec((1, LANES), lambda i: (0, 0))],
        out_shape=[jax.ShapeDtypeStruct((rows, LANES), F32), jax.ShapeDtypeStruct((1, LANES), F32)],
        scratch_shapes=[pltpu.VMEM((1, LANES), F32)],
        compiler_params=_cparams("arbitrary"),
        name="moe_route",
    )(logits)


INV_TM = 512


def _invert_kernel(pos_ref, zeros_ref, inv_ref, sem):
    @pl.when(pl.program_id(0) == 0)
    def _():
        fill = pltpu.make_async_copy(zeros_ref, inv_ref, sem)
        fill.start()
        fill.wait()

    base = pl.program_id(0) * INV_TM

    def scatter(r, carry):
        inv_ref[pos_ref[base + r]] = (base + r) >> 1
        return carry
    lax.fori_loop(0, INV_TM, scatter, 0, unroll=8)


def _invert_call(pos):
    return pl.pallas_call(
        _invert_kernel,
        grid_spec=pltpu.PrefetchScalarGridSpec(
            num_scalar_prefetch=1,
            grid=(pos.shape[0] // INV_TM,),
            in_specs=[pl.BlockSpec(memory_space=pl.ANY)],
            out_specs=pl.BlockSpec(memory_space=pltpu.SMEM),
            scratch_shapes=[pltpu.SemaphoreType.DMA(())],
        ),
        out_shape=jax.ShapeDtypeStruct((MOE_ROWS,), jnp.int32),
        compiler_params=_cparams("arbitrary"),
        name="moe_invert",
    )(pos, jnp.zeros((MOE_ROWS,), jnp.int32))


def _moe_kernel(inv_ref, ps_ref, pn_ref, pd_ref, h_ref, w1_ref, w3_ref, w2_ref, yz_ref, y_ref,
                xg_sc, xb_sc, acc_sc, st_sc, gsem, osem):
    del yz_ref
    p = pl.program_id(0)
    f = pl.program_id(1)
    n_f = pl.num_programs(1)
    start = ps_ref[p]
    n = pn_ref[p]
    half = D_MODEL // 2
    subs = [slice(i * MOE_SUB, (i + 1) * MOE_SUB) for i in range(MOE_NSUB)]

    def row_copy(src_row, r):
        return pltpu.make_async_copy(h_ref.at[pl.ds(src_row, 1)], xg_sc.at[pl.ds(r, 1)], gsem)

    def gather_sub(chunk_start, sub):
        base = (chunk_start + sub) * MOE_SUB

        def body(r, carry):
            row_copy(inv_ref[base + r], sub * MOE_SUB + r).start()
            return carry
        lax.fori_loop(0, MOE_SUB, body, 0, unroll=8)

    @pl.when((p == 0) & (f == 0))
    def _():
        def body(sub, carry):
            gather_sub(start, sub)
            return carry
        lax.fori_loop(0, n, body, 0)

    @pl.when(f == 0)
    def _():
        def wait(r, carry):
            row_copy(0, r).wait()
            return carry
        lax.fori_loop(0, n * MOE_SUB, wait, 0)
        for sub, rows in enumerate(subs):
            @pl.when(sub < n)
            def _():
                x_lo, x_hi = _unpack_bf16_pairs(xg_sc[rows, :])
                xb_sc[rows, :half] = x_lo.astype(BF16)
                xb_sc[rows, half:] = x_hi.astype(BF16)
                acc_sc[rows, :] = jnp.zeros((MOE_SUB, D_MODEL), F32)

    nxt = jnp.minimum(p + 1, MOE_PAIRS - 1)

    @pl.when((p + 1 < MOE_PAIRS) & (f < pn_ref[nxt]))
    def _():
        gather_sub(ps_ref[nxt], f)

    def weights():
        return w1_ref[0, 0].astype(BF16), w3_ref[0, 0].astype(BF16), w2_ref[0, 0].astype(BF16)

    @pl.when(n == MOE_NSUB)
    def _():
        _swiglu_accumulate(xb_sc, acc_sc, subs, *weights())

    @pl.when((n > 0) & (n < MOE_NSUB))
    def _():
        w = weights()
        for sub, rows in enumerate(subs):
            @pl.when(sub < n)
            def _():
                _swiglu_accumulate(xb_sc, acc_sc, [rows], *w)

    @pl.when(f == n_f - 1)
    def _():
        def out_copy(sub):
            row0 = pl.multiple_of((start + sub) * MOE_SUB, MOE_SUB)
            return pltpu.make_async_copy(st_sc.at[sub % 2], y_ref.at[pl.ds(row0, MOE_SUB)], osem.at[sub % 2])

        for sub, rows in enumerate(subs):
            @pl.when(sub < n)
            def _():
                if sub >= 2:
                    out_copy(sub - 2).wait()
                st_sc[sub % 2] = _pack_bf16_pairs(acc_sc[rows, :])
                out_copy(sub).start()

        for sub in range(MOE_NSUB):
            @pl.when((sub < n) & (sub + 2 >= n))
            def _():
                out_copy(sub).wait()


def _moe_call(inv, pair_tables, h_packed, w1, w3, w2):
    n_f = D_FF_EXPERT // MOE_TF

    assert n_f >= MOE_NSUB

    def f_eff(p, f, pn):
        return jnp.where(pn[p] > 0, f, n_f - 1)

    w13 = pl.BlockSpec((1, 1, D_MODEL, MOE_TF), lambda p, f, inv, ps, pn, pd: (0, pd[p], 0, f_eff(p, f, pn)))
    w2s = pl.BlockSpec((1, 1, MOE_TF, D_MODEL), lambda p, f, inv, ps, pn, pd: (0, pd[p], f_eff(p, f, pn), 0))
    hbm = pl.BlockSpec(memory_space=pl.ANY)
    win = MOE_NSUB * MOE_SUB
    n_prefetch = 4
    return pl.pallas_call(
        _moe_kernel,
        grid_spec=pltpu.PrefetchScalarGridSpec(
            num_scalar_prefetch=n_prefetch,
            grid=(MOE_PAIRS, n_f),
            in_specs=[hbm, w13, w13, w2s, hbm],
            out_specs=hbm,
            scratch_shapes=[pltpu.VMEM((win, D_MODEL // 2), U32),
                            pltpu.VMEM((win, D_MODEL), BF16),
                            pltpu.VMEM((win, D_MODEL), F32),
                            pltpu.VMEM((2, MOE_SUB, D_MODEL // 2), U32),
                            pltpu.SemaphoreType.DMA(()),
                            pltpu.SemaphoreType.DMA((2,))],
        ),
        out_shape=jax.ShapeDtypeStruct((MOE_ROWS, D_MODEL // 2), U32),
        input_output_aliases={n_prefetch + 4: 0},
        compiler_params=_cparams("arbitrary", "arbitrary"),
        name="moe_experts",
    )(inv, *pair_tables, h_packed, w1, w3, w2, jnp.zeros((MOE_ROWS, D_MODEL // 2), U32))


def _pair_schedule(counts):
    nsub = (counts + MOE_SUB - 1) // MOE_SUB
    seg_start = jnp.cumsum(nsub) - nsub
    nchunk = (nsub + MOE_NSUB - 1) // MOE_NSUB
    chunk_end = jnp.cumsum(nchunk)
    idx = jnp.arange(MOE_PAIRS, dtype=jnp.int32)
    expert = jnp.sum(chunk_end[None, :] <= idx[:, None], axis=1).astype(jnp.int32)
    active = expert < N_EXPERTS
    onehot = expert[:, None] == jnp.arange(N_EXPERTS, dtype=jnp.int32)[None, :]
    pick = lambda a: jnp.sum(jnp.where(onehot, a[None, :], 0), axis=1)
    c = idx - (pick(chunk_end) - pick(nchunk))
    first = jnp.where(active, pick(seg_start) + MOE_NSUB * c, 0)
    count = jnp.where(active, jnp.minimum(MOE_NSUB, pick(nsub) - MOE_NSUB * c), 0)
    dma_expert = jnp.where(active, expert, jnp.max(jnp.where(active, expert, 0)))
    i32 = lambda a: a.astype(jnp.int32)
    return seg_start * MOE_SUB, tuple(i32(a) for a in (first, count, dma_expert))


def _combine_kernel(pos_ref, x_ref, meta_ref, gf_ref, ys_ref, o_ref, buf, sem):
    base = pl.program_id(0) * COMB_TM

    def copy(r, k, src_row):
        return pltpu.make_async_copy(ys_ref.at[pl.ds(src_row, 1)], buf.at[k, pl.ds(r, 1)], sem)

    def start(r, carry):
        for k in range(2):
            copy(r, k, pos_ref[2 * (base + r) + k]).start()
        return carry

    lax.fori_loop(0, COMB_TM, start, 0, unroll=8)

    def wait(r, carry):
        for k in range(2):
            copy(r, k, 0).wait()
        return carry

    lax.fori_loop(0, COMB_TM, wait, 0, unroll=8)

    half = D_MODEL // 2
    lo0, hi0 = _unpack_bf16_pairs(buf[0])
    lo1, hi1 = _unpack_bf16_pairs(buf[1])
    g1 = meta_ref[:, 2:3]
    g2 = meta_ref[:, 3:4]
    gf = gf_ref[0]
    o_ref[:, :half] = x_ref[:, :half] + gf[:, :half] * (g1 * lo0 + g2 * lo1)
    o_ref[:, half:] = x_ref[:, half:] + gf[:, half:] * (g1 * hi0 + g2 * hi1)


def _combine_call(pos, x, meta, gf, ys):
    rows = x.shape[0]
    tiles_per_batch = SEQ // COMB_TM
    return pl.pallas_call(
        _combine_kernel,
        grid_spec=pltpu.PrefetchScalarGridSpec(
            num_scalar_prefetch=1,
            grid=(rows // COMB_TM,),
            in_specs=[
                pl.BlockSpec((COMB_TM, D_MODEL), lambda i, p: (i, 0)),
                pl.BlockSpec((COMB_TM, LANES), lambda i, p: (i, 0)),
                pl.BlockSpec((1, 1, D_MODEL), lambda i, p: (i // tiles_per_batch, 0, 0)),
                pl.BlockSpec(memory_space=pl.ANY),
            ],
            out_specs=pl.BlockSpec((COMB_TM, D_MODEL), lambda i, p: (i, 0)),
            scratch_shapes=[pltpu.VMEM((2, COMB_TM, D_MODEL // 2), U32), pltpu.SemaphoreType.DMA(())],
        ),
        out_shape=jax.ShapeDtypeStruct((rows, D_MODEL), F32),
        compiler_params=_cparams("arbitrary"),
        name="moe_combine",
    )(pos, x, meta, gf, ys)


def _moe_layer(x, h_packed, logits, gf, w1, w3, w2):
    meta, cnt = _route_call(logits)
    expert = meta[:, 0:2].astype(jnp.int32)
    rank = meta[:, 4:6].astype(jnp.int32)
    counts = cnt[0, :N_EXPERTS].astype(jnp.int32)
    seg_start, pair_tables = _pair_schedule(counts)
    ids = jnp.arange(N_EXPERTS, dtype=jnp.int32)
    pos = rank + jnp.sum(jnp.where(expert[..., None] == ids, seg_start, 0), axis=-1)
    pos = pos.reshape(-1).astype(jnp.int32)
    ys = _moe_call(_invert_call(pos), pair_tables, h_packed, w1, w3, w2)
    return _combine_call(pos, x, meta, gf, ys)


def _rope_tables(dh):
    rows = SEQ // GRID_W
    row = jnp.repeat(jnp.arange(rows, dtype=F32), GRID_W)
    col = jnp.tile(jnp.arange(GRID_W, dtype=F32), rows)
    half = dh // 2
    inv = ROPE_THETA ** (-jnp.arange(0, half, 2, dtype=F32) / half)
    ar = row[:, None] * inv[None, :]
    ac = col[:, None] * inv[None, :]
    ang = jnp.concatenate([ar, ar, ac, ac], axis=-1)
    ang = jnp.concatenate([jnp.zeros((CTX_LEN, dh), F32), ang], axis=0)
    ang = jnp.tile(ang, (1, LANES // dh))
    odd = ((jnp.arange(LANES) // (dh // 4)) % 2 == 1)[None, :]
    sin = jnp.sin(ang)
    return jnp.cos(ang), jnp.where(odd, sin, 0.0), jnp.where(odd, 0.0, -sin)


def kernel(x, c, ctx, c_ctx, ada_w, ada_b, norm_mix_g, norm_ffn_g, w_in, w_out, a_q_norm, a_k_norm,
           a_lam_q1, a_lam_k1, a_lam_q2, a_lam_k2, a_subln_g, b_conv_w, b_out_g, c_q_norm, c_k_norm,
           c_sink, c_out_g, ffn_w1, ffn_w3, ffn_w2, router_w, moe_w1, moe_w3, moe_w2):
    assert x.shape == (BATCH, SEQ, D_MODEL) and ctx.shape == (BATCH, CTX_LEN, D_MODEL)
    cond = jnp.concatenate([c, c_ctx[None, :], jnp.zeros((8 - BATCH - 1, D_MODEL), F32)], axis=0)
    mod = _ada_call(cond, ada_w, ada_b)
    tables = _rope_tables(A_DK) + _rope_tables(C_DH)
    xs = jnp.concatenate([ctx, x], axis=1)
    y_prev = gf_prev = None
    out = None
    for l in range(DEPTH):
        last = l == DEPTH - 1
        sh_a, sc_a, g_a, sh_f, sc_f, g_f = [
            mod[l, :BATCH + 1, k * D_MODEL:(k + 1) * D_MODEL].reshape(BATCH + 1, 1, D_MODEL) for k in range(6)]
        gains = (jnp.tile(a_q_norm[l], LANES // A_DK)[None, :], jnp.tile(a_k_norm[l], LANES // A_DK)[None, :],
                 c_q_norm[l][None, :], c_k_norm[l][None, :])
        res = _in_call(xs, y_prev, gf_prev, norm_mix_g[l][None, :], sc_a, sh_a, w_in[l].astype(BF16),
                       gains, tables)
        if y_prev is not None:
            xs, res = res[0], res[1:]
        qa, ka, va, bb, u, qc, kc, vc = res
        lam_init = 0.8 - 0.6 * math.exp(-0.3 * l)
        lam_vecs = [v[l][None, :] for v in (a_lam_q1, a_lam_k1, a_lam_q2, a_lam_k2)]
        oa = _attn_a_call(qa, ka, va, lam_vecs, a_subln_g[l][None, :], lam_init, with_ctx=not last)
        oc = _attn_c_call(c_sink[l], qc, kc, vc, with_ctx=not last)
        wr = None
        if last:
            wr = jnp.pad(router_w[l // 2], ((0, 0), (0, LANES - N_EXPERTS)))
        res = _out_call(xs, oa, bb, u, oc, w_out[l].astype(BF16), g_a, b_out_g[l][None, :],
                        c_out_g[l][None, :], b_conv_w[l], norm_ffn_g[l][None, :], sc_f, sh_f, wr)
        if not last:
            xs, h2 = res
            y = _ffn_call(h2.reshape(BATCH * N_TOK, D_MODEL), ffn_w1[l // 2:l // 2 + 1],
                          ffn_w3[l // 2:l // 2 + 1], ffn_w2[l // 2:l // 2 + 1])
            y_prev, gf_prev = y.reshape(BATCH, N_TOK, D_MODEL), g_f
        else:
            x_lat, h_packed, logits = res
            i = l // 2
            out = _moe_layer(x_lat.reshape(BATCH * SEQ, D_MODEL), h_packed.reshape(BATCH * SEQ, D_MODEL // 2),
                             logits.reshape(BATCH * SEQ, LANES), g_f[:BATCH],
                             moe_w1[i:i + 1], moe_w3[i:i + 1], moe_w2[i:i + 1])
    return out.reshape(BATCH, SEQ, D_MODEL)
```

```python
import functools
import math

import jax
import jax.numpy as jnp
from jax import lax
from jax.experimental import pallas as pl
from jax.experimental.pallas import tpu as pltpu

F32 = jnp.float32
BF16 = jnp.bfloat16
U32 = jnp.uint32

D_MODEL = 2048
BATCH = 2
SEQ = 4096
DEPTH = 2
GRID_W = 64
CTX_LEN = 256
EPS = 1e-6
ROPE_THETA = 10000.0

A_HEADS = 4
A_DK = 64
A_DV = 128
A_WIDTH = A_HEADS * A_DV
B_WIDTH = 512
C_HEADS = 8
C_KV_HEADS = 2
C_GROUP = C_HEADS // C_KV_HEADS
C_DH = 128
C_WIDTH = C_HEADS * C_DH
C_KV_COLS = C_KV_HEADS * C_DH
WINDOW = 128
MIX_WIDTH = A_WIDTH + B_WIDTH + C_WIDTH
IN_COLS = 3 * A_WIDTH + 3 * B_WIDTH + C_WIDTH + 2 * C_KV_COLS
D_FF = 5632
N_EXPERTS = 8
D_FF_EXPERT = 7168

COL_AQ, COL_AK, COL_AV = 0, 512, 1024
COL_BB, COL_BC, COL_BX = 1536, 2048, 2560
COL_CQ, COL_CK, COL_CV = 3072, 4096, 4352

LANES = 128
N_TOK = CTX_LEN + SEQ
TM = 256
N_TILES = N_TOK // TM
IN_TN = 512
CBLK = 128
N_CBLK = N_TOK // CBLK
A_CHUNK = 1024
A_SPLIT = 4
LOG2E = 1.4426950408889634
FFN_TM = N_TOK // 4
FFN_TF = 256
FFN_SPLIT = 4
MOE_SUB = 256
MOE_NSUB = 10
MOE_FAST = 8
MOE_TF = 256
MOE_ROWS = 2 * BATCH * SEQ + N_EXPERTS * MOE_SUB
MOE_PAIRS = (MOE_ROWS // MOE_SUB + N_EXPERTS * (MOE_NSUB - 1)) // MOE_NSUB
ROUTE_TM = 512
COMB_TM = 256
VMEM_LIMIT = 56 * 1024 * 1024
NEG = -1e30

NT_DIMS = (((1,), (1,)), ((), ()))


def _cparams(*sem):
    return pltpu.CompilerParams(dimension_semantics=sem, vmem_limit_bytes=VMEM_LIMIT)


def _dot(a, b):
    return jnp.dot(a, b, preferred_element_type=F32)


def _rms(x, width):
    return x * lax.rsqrt(jnp.sum(x * x, axis=-1, keepdims=True) * (1.0 / width) + EPS)


ADA_TN = 1024


def _ada_kernel(c_ref, w_ref, b_ref, o_ref):
    a = c_ref[...]
    a = a / (1.0 + jnp.exp(-a))
    o_ref[0] = _dot(a.astype(BF16), w_ref[0].astype(BF16)) + b_ref[0]


def _ada_call(cond, ada_w, ada_b):
    n = 6 * D_MODEL
    return pl.pallas_call(
        _ada_kernel,
        grid=(DEPTH, n // ADA_TN),
        in_specs=[
            pl.BlockSpec((8, D_MODEL), lambda l, j: (0, 0)),
            pl.BlockSpec((1, D_MODEL, ADA_TN), lambda l, j: (l, 0, j)),
            pl.BlockSpec((1, 1, ADA_TN), lambda l, j: (l, 0, j)),
        ],
        out_specs=pl.BlockSpec((1, 8, ADA_TN), lambda l, j: (l, 0, j)),
        out_shape=jax.ShapeDtypeStruct((DEPTH, 8, n), F32),
        compiler_params=_cparams("arbitrary", "arbitrary"),
        name="ada_mod",
    )(cond, ada_w, ada_b.reshape(DEPTH, 1, n))


def _rope128(n, cos, s_odd, s_even, quarter):
    return n * cos + pltpu.roll(n, quarter, 1) * s_odd + pltpu.roll(n, LANES - quarter, 1) * s_even


def _in_kernel(*refs, with_y):
    if with_y:
        x_ref, y_ref, gf_ref = refs[:3]
        refs = refs[3:]
    else:
        x_ref = refs[0]
        refs = refs[1:]
    (g_ref, sc_ref, sh_ref, w_ref, aqn_ref, akn_ref, cqn_ref, ckn_ref,
     ca_ref, sa1_ref, sa2_ref, cc_ref, sc1_ref, sc2_ref) = refs[:14]
    outs = refs[14:]
    if with_y:
        xo_ref, outs = outs[0], outs[1:]
    qa_ref, ka_ref, va_ref, bb_ref, u_ref, qc_ref, kc_ref, vc_ref = outs

    x = x_ref[0]
    if with_y:
        x = x + gf_ref[0] * y_ref[0].astype(F32)
        xo_ref[0] = x
    h = _rms(x, D_MODEL) * g_ref[...]
    h = h * (1.0 + sc_ref[0]) + sh_ref[0]
    hb = h.astype(BF16)

    lane = lax.broadcasted_iota(jnp.int32, (TM, LANES), 1)
    lo_half = lane < A_DK

    def proj(col0, width):
        return _dot(hb, w_ref[:, col0:col0 + width])

    def head_a(p, gain_ref, scale):
        p2 = p * p
        s_all = jnp.sum(p2, axis=-1, keepdims=True)
        s_lo = jnp.sum(jnp.where(lo_half, p2, 0.0), axis=-1, keepdims=True)
        ms = jnp.where(lo_half, s_lo, s_all - s_lo) * (1.0 / A_DK)
        n = p * lax.rsqrt(ms + EPS) * gain_ref[...]
        r = _rope128(n, ca_ref[...], sa1_ref[...], sa2_ref[...], A_DK // 4)
        return (r * scale).astype(BF16)

    def head_c(p, gain_ref, scale):
        n = _rms(p, C_DH) * gain_ref[...]
        r = _rope128(n, cc_ref[...], sc1_ref[...], sc2_ref[...], C_DH // 4)
        return (r * scale).astype(BF16)

    def heads(o_ref, col0, width, fn, gain_ref, scale):
        for c in range(0, width, IN_TN):
            wd = min(IN_TN, width - c)
            p = proj(col0 + c, wd)
            for j in range(wd // LANES):
                o_ref[0, :, c + j * LANES:c + (j + 1) * LANES] = fn(
                    p[:, j * LANES:(j + 1) * LANES], gain_ref, scale)

    heads(qa_ref, COL_AQ, A_WIDTH, head_a, aqn_ref, A_DK ** -0.5 * LOG2E)
    heads(ka_ref, COL_AK, A_WIDTH, head_a, akn_ref, 1.0)
    va_ref[0] = proj(COL_AV, A_WIDTH).astype(BF16)
    bb_ref[0] = proj(COL_BB, B_WIDTH).astype(BF16)
    u_ref[0] = (proj(COL_BC, B_WIDTH) * proj(COL_BX, B_WIDTH)).astype(BF16)
    heads(qc_ref, COL_CQ, C_WIDTH, head_c, cqn_ref, C_DH ** -0.5 * LOG2E)
    heads(kc_ref, COL_CK, C_KV_COLS, head_c, ckn_ref, 1.0)
    vc_ref[0] = proj(COL_CV, C_KV_COLS).astype(BF16)


def _mod_row(b, t):
    return jnp.where(t == 0, BATCH, b)


def _in_call(x, y, gf, g, sc, sh, w, gains, tables):
    with_y = y is not None
    tile = lambda width: pl.BlockSpec((1, TM, width), lambda b, t: (b, t, 0))
    modv = pl.BlockSpec((1, 1, D_MODEL), lambda b, t: (_mod_row(b, t), 0, 0))
    full2 = lambda a: pl.BlockSpec(a.shape, lambda b, t: (0, 0))
    rows = pl.BlockSpec((TM, LANES), lambda b, t: (t, 0))
    in_specs = [tile(D_MODEL)]
    args = [x]
    if with_y:
        in_specs += [tile(D_MODEL), modv]
        args += [y, gf]
    in_specs += [full2(g), modv, modv,
                 pl.BlockSpec(w.shape, lambda b, t: (0, 0), pipeline_mode=pl.Buffered(1))]
    args += [g, sc, sh, w]
    in_specs += [full2(a) for a in gains]
    args += list(gains)
    in_specs += [rows] * 6
    args += list(tables)
    widths = (A_WIDTH, A_WIDTH, A_WIDTH, B_WIDTH, B_WIDTH, C_WIDTH, C_KV_COLS, C_KV_COLS)
    out_specs = [tile(wd) for wd in widths]
    out_shape = [jax.ShapeDtypeStruct((BATCH, N_TOK, wd), BF16) for wd in widths]
    if with_y:
        out_specs = [tile(D_MODEL)] + out_specs
        out_shape = [jax.ShapeDtypeStruct((BATCH, N_TOK, D_MODEL), F32)] + out_shape
    return pl.pallas_call(
        functools.partial(_in_kernel, with_y=with_y),
        grid=(BATCH, N_TILES),
        in_specs=in_specs,
        out_specs=out_specs,
        out_shape=out_shape,
        compiler_params=_cparams("arbitrary", "arbitrary"),
        name="in_proj",
    )(*args)


def _attn_a_kernel(q_ref, k_ref, v_ref, lq1_ref, lk1_ref, lq2_ref, lk2_ref, g_ref, o_ref,
                   m_sc, acc_sc, *, lam_init, t_off):
    t = pl.program_id(2) + t_off
    q = q_ref[0]
    lane = lax.broadcasted_iota(jnp.int32, (TM, LANES), 1)
    zero = jnp.zeros_like(q)
    qs = jnp.concatenate([jnp.where(lane < A_DK, q, zero), jnp.where(lane >= A_DK, q, zero)], axis=0)

    m_sc[...] = jnp.full(m_sc.shape, NEG, F32)
    acc_sc[...] = jnp.zeros(acc_sc.shape, F32)

    def step(k, v):
        v1 = jnp.concatenate([v, jnp.ones_like(v)], axis=1)
        grp = 2 * TM // A_SPLIT
        for g in range(A_SPLIT):
            rows = slice(g * grp, (g + 1) * grp)
            s = lax.dot_general(qs[rows], k, NT_DIMS, preferred_element_type=F32)
            m_prev = m_sc[rows, :]
            m_new = jnp.maximum(m_prev, jnp.max(s, axis=-1, keepdims=True))
            alpha = jnp.exp2(m_prev - m_new)
            p = jnp.exp2(s - jnp.tile(m_new, (1, s.shape[1] // LANES)))
            acc_sc[rows, :] = jnp.tile(alpha, (1, 2)) * acc_sc[rows, :] + _dot(p.astype(BF16), v1)
            m_sc[rows, :] = m_new

    step(k_ref[0, 0:CTX_LEN, :], v_ref[0, 0:CTX_LEN, :])

    def latent_keys():
        for off in range(CTX_LEN, N_TOK, A_CHUNK):
            step(k_ref[0, off:off + A_CHUNK, :], v_ref[0, off:off + A_CHUNK, :])

    if t_off == 0:
        pl.when(t > 0)(latent_keys)
    else:
        latent_keys()

    lam = (jnp.exp(jnp.sum(lq1_ref[...] * lk1_ref[...], keepdims=True))
           - jnp.exp(jnp.sum(lq2_ref[...] * lk2_ref[...], keepdims=True)) + lam_init)
    acc = acc_sc[...]
    o = acc[:, :A_DV] / acc[:, A_DV:]
    o = o[:TM] - lam * o[TM:]
    o_ref[0] = (_rms(o, A_DV) * g_ref[...] * (1.0 - lam_init)).astype(BF16)


def _attn_a_call(qa, ka, va, lam_vecs, subln_g, lam_init, with_ctx):
    t_off = 0 if with_ctx else 1
    nt = N_TILES - t_off
    vec = pl.BlockSpec((1, A_DK), lambda b, h, t: (0, 0))
    return pl.pallas_call(
        functools.partial(_attn_a_kernel, lam_init=lam_init, t_off=t_off),
        grid=(BATCH, A_HEADS, nt),
        in_specs=[
            pl.BlockSpec((1, TM, LANES), lambda b, h, t: (b, t + t_off, h)),
            pl.BlockSpec((1, N_TOK, LANES), lambda b, h, t: (b, 0, h)),
            pl.BlockSpec((1, N_TOK, LANES), lambda b, h, t: (b, 0, h)),
            vec, vec, vec, vec,
            pl.BlockSpec((1, A_DV), lambda b, h, t: (0, 0)),
        ],
        out_specs=pl.BlockSpec((1, TM, LANES), lambda b, h, t: (b, t, h)),
        out_shape=jax.ShapeDtypeStruct((BATCH, nt * TM, A_WIDTH), BF16),
        scratch_shapes=[pltpu.VMEM((2 * TM, LANES), F32), pltpu.VMEM((2 * TM, 2 * A_DV), F32)],
        compiler_params=_cparams("arbitrary", "arbitrary", "arbitrary"),
        name="attn_diff",
    )(qa, ka, va, *lam_vecs, subln_g)


def _attn_c_kernel(sink_ref, q_ref, kp_ref, k0_ref, kn_ref, kx_ref, vp_ref, v0_ref, vn_ref, vx_ref,
                   o_ref, *, t_off):
    kvh = pl.program_id(1)
    t = pl.program_id(2) + t_off
    rows = C_GROUP * CBLK
    q4 = jnp.concatenate([q_ref[0, :, g * LANES:(g + 1) * LANES] for g in range(C_GROUP)], axis=0)

    def scores(k_ref):
        return lax.dot_general(q4, k_ref[0], NT_DIMS, preferred_element_type=F32)

    i = lax.broadcasted_iota(jnp.int32, (rows, CBLK), 0) & (CBLK - 1)
    j = lax.broadcasted_iota(jnp.int32, (rows, CBLK), 1)
    n = t - CTX_LEN // CBLK
    n_last = SEQ // CBLK - 1
    far = 4 * CBLK
    off_p = jnp.where(n >= 1, 0, far)
    off_0 = jnp.where(n >= 0, 0, far)
    off_n = jnp.where((n >= 0) & (n < n_last), 0, far)
    s_p = jnp.where(j >= i + off_p, scores(kp_ref), NEG)
    s_0 = jnp.where(j >= off_0, scores(k0_ref), NEG)
    s_n = jnp.where(j + off_n <= i, scores(kn_ref), NEG)
    s_x = scores(kx_ref)

    g_row = lax.broadcasted_iota(jnp.int32, (rows, 1), 0) // CBLK
    sink = jnp.zeros((rows, 1), F32)
    for g in range(C_GROUP):
        sink = jnp.where(g_row == g, sink_ref[kvh * C_GROUP + g] * LOG2E, sink)

    rmax = lambda s: jnp.max(s, axis=-1, keepdims=True)
    m = jnp.maximum(jnp.maximum(jnp.maximum(rmax(s_p), rmax(s_0)), jnp.maximum(rmax(s_n), rmax(s_x))), sink)
    p = jnp.concatenate([jnp.exp2(s - m) for s in (s_p, s_0, s_n, s_x)], axis=1).astype(BF16)
    v = jnp.concatenate([vp_ref[0], v0_ref[0], vn_ref[0], vx_ref[0]], axis=0)
    ol = _dot(p, jnp.concatenate([v, jnp.ones_like(v)], axis=1))
    o = ol[:, :C_DH] / (ol[:, C_DH:] + jnp.exp2(sink - m))
    for g in range(C_GROUP):
        o_ref[0, :, g * LANES:(g + 1) * LANES] = o[g * CBLK:(g + 1) * CBLK].astype(BF16)


def _attn_c_call(sink, qc, kc, vc, with_ctx):
    t_off = 0 if with_ctx else CTX_LEN // CBLK
    nb = N_CBLK - t_off
    prev = pl.BlockSpec((1, CBLK, LANES), lambda b, k, t, s: (b, jnp.maximum(t + t_off - 1, 0), k))
    own = pl.BlockSpec((1, CBLK, LANES), lambda b, k, t, s: (b, t + t_off, k))
    nxt = pl.BlockSpec((1, CBLK, LANES), lambda b, k, t, s: (b, jnp.minimum(t + t_off + 1, N_CBLK - 1), k))
    ctx = pl.BlockSpec((1, CTX_LEN, LANES), lambda b, k, t, s: (b, 0, k))
    qi = pl.BlockSpec((1, CBLK, C_GROUP * C_DH), lambda b, k, t, s: (b, t + t_off, k))
    qo = pl.BlockSpec((1, CBLK, C_GROUP * C_DH), lambda b, k, t, s: (b, t, k))
    return pl.pallas_call(
        functools.partial(_attn_c_kernel, t_off=t_off),
        grid_spec=pltpu.PrefetchScalarGridSpec(
            num_scalar_prefetch=1,
            grid=(BATCH, C_KV_HEADS, nb),
            in_specs=[qi, prev, own, nxt, ctx, prev, own, nxt, ctx],
            out_specs=qo,
        ),
        out_shape=jax.ShapeDtypeStruct((BATCH, nb * CBLK, C_WIDTH), BF16),
        compiler_params=_cparams("arbitrary", "arbitrary", "arbitrary"),
        name="attn_window",
    )(sink, qc, kc, kc, kc, kc, vc, vc, vc, vc)


HALO = 16


def _pack_bf16_pairs(h):
    hb = h.astype(BF16).astype(F32)
    half = h.shape[-1] // 2
    lo = lax.bitcast_convert_type(hb[:, :half], U32)
    hi = lax.bitcast_convert_type(hb[:, half:], U32)
    return (hi & jnp.uint32(0xFFFF0000)) | (lo >> 16)


def _unpack_bf16_pairs(p):
    lo = lax.bitcast_convert_type(p << 16, F32)
    hi = lax.bitcast_convert_type(p & jnp.uint32(0xFFFF0000), F32)
    return lo, hi


def _out_kernel(*refs, t_off, route):
    (x_ref, oa_ref, bb_ref, u_ref, up_ref, un_ref, oc_ref, w_ref, ga_ref, bg_ref, cg_ref, cw_ref,
     gf_ref, scf_ref, shf_ref) = refs[:15]
    refs = refs[15:]
    if route:
        wrh_ref, wrl_ref, xo_ref, h2_ref, lg_ref = refs
    else:
        xo_ref, h2_ref = refs
    t = pl.program_id(1) + t_off

    u = u_ref[0].astype(F32)
    row = lax.broadcasted_iota(jnp.int32, (TM, 1), 0)
    prev_ok = jnp.where(t >= 2, 1.0, 0.0)
    next_ok = jnp.where((t >= 1) & (t < N_TILES - 1), 1.0, 0.0)
    prev_row = up_ref[0, HALO - 1:HALO, :].astype(F32) * prev_ok
    next_row = un_ref[0, 0:1, :].astype(F32) * next_ok
    u_prev = jnp.where(row == 0, prev_row, pltpu.roll(u, 1, 0))
    u_next = jnp.where(row == TM - 1, next_row, pltpu.roll(u, TM - 1, 0))
    conv = u_prev * cw_ref[0:1, :] + u * cw_ref[1:2, :] + u_next * cw_ref[2:3, :]
    yb = _rms(bb_ref[0].astype(F32) * conv, B_WIDTH) * bg_ref[...]
    yc = _rms(oc_ref[0].astype(F32), C_WIDTH) * cg_ref[...]
    y = (_dot(oa_ref[0], w_ref[0:A_WIDTH, :])
         + _dot(yb.astype(BF16), w_ref[A_WIDTH:A_WIDTH + B_WIDTH, :])
         + _dot(yc.astype(BF16), w_ref[A_WIDTH + B_WIDTH:, :]))
    x = x_ref[0] + ga_ref[0] * y
    xo_ref[0] = x
    h2 = _rms(x, D_MODEL) * gf_ref[...]
    h2 = h2 * (1.0 + scf_ref[0]) + shf_ref[0]
    if route:
        h2_ref[0] = _pack_bf16_pairs(h2)
        h_hi = h2.astype(BF16)
        h_lo = (h2 - h_hi.astype(F32)).astype(BF16)
        lg_ref[0] = _dot(h_hi, wrh_ref[...]) + (_dot(h_lo, wrh_ref[...]) + _dot(h_hi, wrl_ref[...]))
    else:
        h2_ref[0] = h2.astype(BF16)


def _out_call(x, oa, bb, u, oc, w, ga, bg, cg, cw, gfn, scf, shf, wr):
    route = wr is not None
    t_off = 1 if route else 0
    nt = N_TILES - t_off
    hpt = TM // HALO
    tile = lambda width: pl.BlockSpec((1, TM, width), lambda b, t: (b, t + t_off, 0))
    otile = lambda width: pl.BlockSpec((1, TM, width), lambda b, t: (b, t, 0))
    modv = pl.BlockSpec((1, 1, D_MODEL), lambda b, t: (_mod_row(b, t + t_off), 0, 0))
    full2 = lambda a: pl.BlockSpec(a.shape, lambda b, t: (0, 0))
    halo_p = pl.BlockSpec((1, HALO, B_WIDTH), lambda b, t: (b, jnp.maximum((t + t_off) * hpt - 1, 0), 0))
    halo_n = pl.BlockSpec((1, HALO, B_WIDTH),
                          lambda b, t: (b, jnp.minimum((t + t_off + 1) * hpt, N_TOK // HALO - 1), 0))
    in_specs = [tile(D_MODEL), otile(A_WIDTH), tile(B_WIDTH), tile(B_WIDTH), halo_p, halo_n, otile(C_WIDTH),
                pl.BlockSpec(w.shape, lambda b, t: (0, 0), pipeline_mode=pl.Buffered(1)),
                modv, full2(bg), full2(cg), full2(cw), full2(gfn), modv, modv]
    args = [x, oa, bb, u, u, u, oc, w, ga, bg, cg, cw, gfn, scf, shf]
    rows_out = nt * TM
    if route:
        wr_hi = wr.astype(BF16)
        wr_lo = (wr - wr_hi.astype(F32)).astype(BF16)
        in_specs += [full2(wr_hi), full2(wr_lo)]
        args += [wr_hi, wr_lo]
        out_specs = [otile(D_MODEL), otile(D_MODEL // 2), otile(LANES)]
        out_shape = [jax.ShapeDtypeStruct((BATCH, rows_out, D_MODEL), F32),
                     jax.ShapeDtypeStruct((BATCH, rows_out, D_MODEL // 2), U32),
                     jax.ShapeDtypeStruct((BATCH, rows_out, LANES), F32)]
    else:
        out_specs = [otile(D_MODEL), otile(D_MODEL)]
        out_shape = [jax.ShapeDtypeStruct((BATCH, rows_out, D_MODEL), F32),
                     jax.ShapeDtypeStruct((BATCH, rows_out, D_MODEL), BF16)]
    return pl.pallas_call(
        functools.partial(_out_kernel, t_off=t_off, route=route),
        grid=(BATCH, nt),
        in_specs=in_specs,
        out_specs=out_specs,
        out_shape=out_shape,
        compiler_params=_cparams("arbitrary", "arbitrary"),
        name="mix_out",
    )(*args)


def _swiglu_accumulate(h_ref, acc_ref, row_blocks, w1, w3, w2, packed=False):
    for rows in row_blocks:
        if packed:
            x_lo, x_hi = _unpack_bf16_pairs(h_ref[rows, :])
            h = jnp.concatenate([x_lo.astype(BF16), x_hi.astype(BF16)], axis=1)
        else:
            h = h_ref[rows, :]
        a = _dot(h, w1)
        b = _dot(h, w3)
        z = a / (1.0 + jnp.exp(-a)) * b
        acc_ref[rows, :] += _dot(z.astype(BF16), w2)


def _ffn_kernel(h_ref, w1_ref, w3_ref, w2_ref, y_ref, acc_ref):
    f = pl.program_id(1)

    @pl.when(f == 0)
    def _():
        acc_ref[...] = jnp.zeros(acc_ref.shape, F32)

    sub = FFN_TM // FFN_SPLIT
    _swiglu_accumulate(h_ref, acc_ref, [slice(i * sub, (i + 1) * sub) for i in range(FFN_SPLIT)],
                       w1_ref[0].astype(BF16), w3_ref[0].astype(BF16), w2_ref[0].astype(BF16))

    @pl.when(f == pl.num_programs(1) - 1)
    def _():
        y_ref[...] = acc_ref[...].astype(BF16)


def _ffn_call(h, w1, w3, w2):
    rows = h.shape[0]
    return pl.pallas_call(
        _ffn_kernel,
        grid=(rows // FFN_TM, D_FF // FFN_TF),
        in_specs=[
            pl.BlockSpec((FFN_TM, D_MODEL), lambda i, f: (i, 0)),
            pl.BlockSpec((1, D_MODEL, FFN_TF), lambda i, f: (0, 0, f)),
            pl.BlockSpec((1, D_MODEL, FFN_TF), lambda i, f: (0, 0, f)),
            pl.BlockSpec((1, FFN_TF, D_MODEL), lambda i, f: (0, f, 0)),
        ],
        out_specs=pl.BlockSpec((FFN_TM, D_MODEL), lambda i, f: (i, 0)),
        out_shape=jax.ShapeDtypeStruct((rows, D_MODEL), BF16),
        scratch_shapes=[pltpu.VMEM((FFN_TM, D_MODEL), F32)],
        compiler_params=_cparams("arbitrary", "arbitrary"),
        name="ffn_dense",
    )(h, w1, w3, w2)


def _route_kernel(lg_ref, meta_ref, cnt_ref, carry_sc):
    @pl.when(pl.program_id(0) == 0)
    def _():
        carry_sc[...] = jnp.zeros(carry_sc.shape, F32)

    lane = lax.broadcasted_iota(jnp.int32, (ROUTE_TM, LANES), 1)
    lane_f = lane.astype(F32)
    lg = jnp.where(lane < N_EXPERTS, lg_ref[...], -jnp.inf)
    m1 = jnp.max(lg, axis=-1, keepdims=True)
    i1 = jnp.min(jnp.where(lg == m1, lane_f, float(LANES)), axis=-1, keepdims=True)
    oh1 = lane_f == i1
    lg2 = jnp.where(oh1, -jnp.inf, lg)
    m2 = jnp.max(lg2, axis=-1, keepdims=True)
    i2 = jnp.min(jnp.where(lg2 == m2, lane_f, float(LANES)), axis=-1, keepdims=True)
    oh2 = lane_f == i2
    e = jnp.exp(m2 - m1)
    g1 = 1.0 / (1.0 + e)
    g2 = e / (1.0 + e)
    oh = jnp.where(oh1, 1.0, 0.0) + jnp.where(oh2, 1.0, 0.0)
    r = lax.broadcasted_iota(jnp.int32, (ROUTE_TM, ROUTE_TM), 0)
    c = lax.broadcasted_iota(jnp.int32, (ROUTE_TM, ROUTE_TM), 1)
    tri = jnp.where(c < r, 1.0, 0.0).astype(BF16)
    before = _dot(tri, oh.astype(BF16)) + carry_sc[...]
    rank1 = jnp.sum(jnp.where(oh1, before, 0.0), axis=-1, keepdims=True)
    rank2 = jnp.sum(jnp.where(oh2, before, 0.0), axis=-1, keepdims=True)
    meta = jnp.zeros((ROUTE_TM, LANES), F32)
    for k, val in enumerate((i1, i2, g1, g2, rank1, rank2)):
        meta = jnp.where(lane == k, val, meta)
    meta_ref[...] = meta
    carry_sc[...] += jnp.sum(oh, axis=0, keepdims=True)
    cnt_ref[...] = carry_sc[...]


def _route_call(logits):
    rows = logits.shape[0]
    return pl.pallas_call(
        _route_kernel,
        grid=(rows // ROUTE_TM,),
        in_specs=[pl.BlockSpec((ROUTE_TM, LANES), lambda i: (i, 0))],
        out_specs=[pl.BlockSpec((ROUTE_TM, LANES), lambda i: (i, 0)),
                   pl.BlockSpec((1, LANES), lambda i: (0, 0))],
        out_shape=[jax.ShapeDtypeStruct((rows, LANES), F32), jax.ShapeDtypeStruct((1, LANES), F32)],
        scratch_shapes=[pltpu.VMEM((1, LANES), F32)],
        compiler_params=_cparams("arbitrary"),
        name="moe_route",
    )(logits)


INV_TM = 512


def _invert_kernel(pos_ref, zeros_ref, inv_ref, sem):
    @pl.when(pl.program_id(0) == 0)
    def _():
        fill = pltpu.make_async_copy(zeros_ref, inv_ref, sem)
        fill.start()
        fill.wait()

    base = pl.program_id(0) * INV_TM

    def scatter(r, carry):
        inv_ref[pos_ref[base + r]] = (base + r) >> 1
        return carry
    lax.fori_loop(0, INV_TM, scatter, 0, unroll=8)


def _invert_call(pos):
    return pl.pallas_call(
        _invert_kernel,
        grid_spec=pltpu.PrefetchScalarGridSpec(
            num_scalar_prefetch=1,
            grid=(pos.shape[0] // INV_TM,),
            in_specs=[pl.BlockSpec(memory_space=pl.ANY)],
            out_specs=pl.BlockSpec(memory_space=pltpu.SMEM),
            scratch_shapes=[pltpu.SemaphoreType.DMA(())],
        ),
        out_shape=jax.ShapeDtypeStruct((MOE_ROWS,), jnp.int32),
        compiler_params=_cparams("arbitrary"),
        name="moe_invert",
    )(pos, jnp.zeros((MOE_ROWS,), jnp.int32))


def _moe_kernel(inv_ref, ps_ref, pn_ref, pd_ref, h_ref, w1_ref, w3_ref, w2_ref, yz_ref, y_ref,
                xg_sc, acc_sc, st_sc, gsem, osem):
    del yz_ref
    p = pl.program_id(0)
    f = pl.program_id(1)
    n_f = pl.num_programs(1)
    start = ps_ref[p]
    n = pn_ref[p]
    subs = [slice(i * MOE_SUB, (i + 1) * MOE_SUB) for i in range(MOE_NSUB)]

    def row_copy(src_row, r):
        return pltpu.make_async_copy(h_ref.at[pl.ds(src_row, 1)], xg_sc.at[pl.ds(r, 1)], gsem)

    @pl.when(f == 0)
    def _():
        def per_sub(fn):
            def body(sub, carry):
                lax.fori_loop(0, MOE_SUB, functools.partial(fn, sub * MOE_SUB), 0, unroll=8)
                return carry
            lax.fori_loop(0, n, body, 0)

        def issue(r0, r, carry):
            row_copy(inv_ref[start * MOE_SUB + r0 + r], r0 + r).start()
            return carry

        def wait(r0, r, carry):
            row_copy(0, r0 + r).wait()
            return carry

        per_sub(issue)
        for sub, rows in enumerate(subs):
            @pl.when(sub < n)
            def _():
                acc_sc[rows, :] = jnp.zeros((MOE_SUB, D_MODEL), F32)
        per_sub(wait)

    def weights():
        return w1_ref[0, 0].astype(BF16), w3_ref[0, 0].astype(BF16), w2_ref[0, 0].astype(BF16)

    def accumulate(row_blocks, w):
        _swiglu_accumulate(xg_sc, acc_sc, row_blocks, *w, packed=True)

    @pl.when(n >= MOE_FAST)
    def _():
        w = weights()
        accumulate(subs[:MOE_FAST], w)
        for sub in range(MOE_FAST, MOE_NSUB):
            @pl.when(sub < n)
            def _():
                accumulate([subs[sub]], w)

    @pl.when((n > 0) & (n < MOE_FAST))
    def _():
        w = weights()
        for sub in range(MOE_FAST):
            @pl.when(sub < n)
            def _():
                accumulate([subs[sub]], w)

    @pl.when(f == n_f - 1)
    def _():
        def out_copy(sub):
            row0 = pl.multiple_of((start + sub) * MOE_SUB, MOE_SUB)
            return pltpu.make_async_copy(st_sc.at[sub % 2], y_ref.at[pl.ds(row0, MOE_SUB)], osem.at[sub % 2])

        for sub, rows in enumerate(subs):
            @pl.when(sub < n)
            def _():
                if sub >= 2:
                    out_copy(sub - 2).wait()
                st_sc[sub % 2] = _pack_bf16_pairs(acc_sc[rows, :])
                out_copy(sub).start()

        for sub in range(MOE_NSUB):
            @pl.when((sub < n) & (sub + 2 >= n))
            def _():
                out_copy(sub).wait()


def _moe_call(inv, pair_tables, h_packed, w1, w3, w2):
    n_f = D_FF_EXPERT // MOE_TF

    def f_eff(p, f, pn):
        return jnp.where(pn[p] > 0, f, n_f - 1)

    w13 = pl.BlockSpec((1, 1, D_MODEL, MOE_TF), lambda p, f, inv, ps, pn, pd: (0, pd[p], 0, f_eff(p, f, pn)))
    w2s = pl.BlockSpec((1, 1, MOE_TF, D_MODEL), lambda p, f, inv, ps, pn, pd: (0, pd[p], f_eff(p, f, pn), 0))
    hbm = pl.BlockSpec(memory_space=pl.ANY)
    win = MOE_NSUB * MOE_SUB
    n_prefetch = 4
    return pl.pallas_call(
        _moe_kernel,
        grid_spec=pltpu.PrefetchScalarGridSpec(
            num_scalar_prefetch=n_prefetch,
            grid=(MOE_PAIRS, n_f),
            in_specs=[hbm, w13, w13, w2s, hbm],
            out_specs=hbm,
            scratch_shapes=[pltpu.VMEM((win, D_MODEL // 2), U32),
                            pltpu.VMEM((win, D_MODEL), F32),
                            pltpu.VMEM((2, MOE_SUB, D_MODEL // 2), U32),
                            pltpu.SemaphoreType.DMA(()),
                            pltpu.SemaphoreType.DMA((2,))],
        ),
        out_shape=jax.ShapeDtypeStruct((MOE_ROWS, D_MODEL // 2), U32),
        input_output_aliases={n_prefetch + 4: 0},
        compiler_params=_cparams("arbitrary", "arbitrary"),
        name="moe_experts",
    )(inv, *pair_tables, h_packed, w1, w3, w2, jnp.zeros((MOE_ROWS, D_MODEL // 2), U32))


def _pair_schedule(counts):
    nsub = (counts + MOE_SUB - 1) // MOE_SUB
    seg_start = jnp.cumsum(nsub) - nsub
    nchunk = (nsub + MOE_NSUB - 1) // MOE_NSUB
    chunk_end = jnp.cumsum(nchunk)
    idx = jnp.arange(MOE_PAIRS, dtype=jnp.int32)
    expert = jnp.sum(chunk_end[None, :] <= idx[:, None], axis=1).astype(jnp.int32)
    active = expert < N_EXPERTS
    onehot = expert[:, None] == jnp.arange(N_EXPERTS, dtype=jnp.int32)[None, :]
    pick = lambda a: jnp.sum(jnp.where(onehot, a[None, :], 0), axis=1)
    c = idx - (pick(chunk_end) - pick(nchunk))
    first = jnp.where(active, pick(seg_start) + MOE_NSUB * c, 0)
    count = jnp.where(active, jnp.minimum(MOE_NSUB, pick(nsub) - MOE_NSUB * c), 0)
    dma_expert = jnp.where(active, expert, jnp.max(jnp.where(active, expert, 0)))
    i32 = lambda a: a.astype(jnp.int32)
    return seg_start * MOE_SUB, tuple(i32(a) for a in (first, count, dma_expert))


def _combine_kernel(pos_ref, x_ref, meta_ref, gf_ref, ys_ref, o_ref, buf, sem):
    base = pl.program_id(0) * COMB_TM

    def copy(r, k, src_row):
        return pltpu.make_async_copy(ys_ref.at[pl.ds(src_row, 1)], buf.at[k, pl.ds(r, 1)], sem)

    def start(r, carry):
        for k in range(2):
            copy(r, k, pos_ref[2 * (base + r) + k]).start()
        return carry

    lax.fori_loop(0, COMB_TM, start, 0, unroll=8)

    def wait(r, carry):
        for k in range(2):
            copy(r, k, 0).wait()
        return carry

    lax.fori_loop(0, COMB_TM, wait, 0, unroll=8)

    half = D_MODEL // 2
    lo0, hi0 = _unpack_bf16_pairs(buf[0])
    lo1, hi1 = _unpack_bf16_pairs(buf[1])
    g1 = meta_ref[:, 2:3]
    g2 = meta_ref[:, 3:4]
    gf = gf_ref[0]
    o_ref[:, :half] = x_ref[:, :half] + gf[:, :half] * (g1 * lo0 + g2 * lo1)
    o_ref[:, half:] = x_ref[:, half:] + gf[:, half:] * (g1 * hi0 + g2 * hi1)


def _combine_call(pos, x, meta, gf, ys):
    rows = x.shape[0]
    tiles_per_batch = SEQ // COMB_TM
    return pl.pallas_call(
        _combine_kernel,
        grid_spec=pltpu.PrefetchScalarGridSpec(
            num_scalar_prefetch=1,
            grid=(rows // COMB_TM,),
            in_specs=[
                pl.BlockSpec((COMB_TM, D_MODEL), lambda i, p: (i, 0)),
                pl.BlockSpec((COMB_TM, LANES), lambda i, p: (i, 0)),
                pl.BlockSpec((1, 1, D_MODEL), lambda i, p: (i // tiles_per_batch, 0, 0)),
                pl.BlockSpec(memory_space=pl.ANY),
            ],
            out_specs=pl.BlockSpec((COMB_TM, D_MODEL), lambda i, p: (i, 0)),
            scratch_shapes=[pltpu.VMEM((2, COMB_TM, D_MODEL // 2), U32), pltpu.SemaphoreType.DMA(())],
        ),
        out_shape=jax.ShapeDtypeStruct((rows, D_MODEL), F32),
        compiler_params=_cparams("arbitrary"),
        name="moe_combine",
    )(pos, x, meta, gf, ys)


def _moe_layer(x, h_packed, logits, gf, w1, w3, w2):
    meta, cnt = _route_call(logits)
    expert = meta[:, 0:2].astype(jnp.int32)
    rank = meta[:, 4:6].astype(jnp.int32)
    counts = cnt[0, :N_EXPERTS].astype(jnp.int32)
    seg_start, pair_tables = _pair_schedule(counts)
    ids = jnp.arange(N_EXPERTS, dtype=jnp.int32)
    pos = rank + jnp.sum(jnp.where(expert[..., None] == ids, seg_start, 0), axis=-1)
    pos = pos.reshape(-1).astype(jnp.int32)
    ys = _moe_call(_invert_call(pos), pair_tables, h_packed, w1, w3, w2)
    return _combine_call(pos, x, meta, gf, ys)


def _rope_tables(dh):
    rows = SEQ // GRID_W
    row = jnp.repeat(jnp.arange(rows, dtype=F32), GRID_W)
    col = jnp.tile(jnp.arange(GRID_W, dtype=F32), rows)
    half = dh // 2
    inv = ROPE_THETA ** (-jnp.arange(0, half, 2, dtype=F32) / half)
    ar = row[:, None] * inv[None, :]
    ac = col[:, None] * inv[None, :]
    ang = jnp.concatenate([ar, ar, ac, ac], axis=-1)
    ang = jnp.concatenate([jnp.zeros((CTX_LEN, dh), F32), ang], axis=0)
    ang = jnp.tile(ang, (1, LANES // dh))
    odd = ((jnp.arange(LANES) // (dh // 4)) % 2 == 1)[None, :]
    sin = jnp.sin(ang)
    return jnp.cos(ang), jnp.where(odd, sin, 0.0), jnp.where(odd, 0.0, -sin)


def kernel(x, c, ctx, c_ctx, ada_w, ada_b, norm_mix_g, norm_ffn_g, w_in, w_out, a_q_norm, a_k_norm,
           a_lam_q1, a_lam_k1, a_lam_q2, a_lam_k2, a_subln_g, b_conv_w, b_out_g, c_q_norm, c_k_norm,
           c_sink, c_out_g, ffn_w1, ffn_w3, ffn_w2, router_w, moe_w1, moe_w3, moe_w2):
    assert x.shape == (BATCH, SEQ, D_MODEL) and ctx.shape == (BATCH, CTX_LEN, D_MODEL)
    cond = jnp.concatenate([c, c_ctx[None, :], jnp.zeros((8 - BATCH - 1, D_MODEL), F32)], axis=0)
    mod = _ada_call(cond, ada_w, ada_b)
    tables = _rope_tables(A_DK) + _rope_tables(C_DH)
    xs = jnp.concatenate([ctx, x], axis=1)
    y_prev = gf_prev = None
    out = None
    for l in range(DEPTH):
        last = l == DEPTH - 1
        sh_a, sc_a, g_a, sh_f, sc_f, g_f = [
            mod[l, :BATCH + 1, k * D_MODEL:(k + 1) * D_MODEL].reshape(BATCH + 1, 1, D_MODEL) for k in range(6)]
        gains = (jnp.tile(a_q_norm[l], LANES // A_DK)[None, :], jnp.tile(a_k_norm[l], LANES // A_DK)[None, :],
                 c_q_norm[l][None, :], c_k_norm[l][None, :])
        res = _in_call(xs, y_prev, gf_prev, norm_mix_g[l][None, :], sc_a, sh_a, w_in[l].astype(BF16),
                       gains, tables)
        if y_prev is not None:
            xs, res = res[0], res[1:]
        qa, ka, va, bb, u, qc, kc, vc = res
        lam_init = 0.8 - 0.6 * math.exp(-0.3 * l)
        lam_vecs = [v[l][None, :] for v in (a_lam_q1, a_lam_k1, a_lam_q2, a_lam_k2)]
        oa = _attn_a_call(qa, ka, va, lam_vecs, a_subln_g[l][None, :], lam_init, with_ctx=not last)
        oc = _attn_c_call(c_sink[l], qc, kc, vc, with_ctx=not last)
        wr = None
        if last:
            wr = jnp.pad(router_w[l // 2], ((0, 0), (0, LANES - N_EXPERTS)))
        res = _out_call(xs, oa, bb, u, oc, w_out[l].astype(BF16), g_a, b_out_g[l][None, :],
                        c_out_g[l][None, :], b_conv_w[l], norm_ffn_g[l][None, :], sc_f, sh_f, wr)
        if not last:
            xs, h2 = res
            y = _ffn_call(h2.reshape(BATCH * N_TOK, D_MODEL), ffn_w1[l // 2:l // 2 + 1],
                          ffn_w3[l // 2:l // 2 + 1], ffn_w2[l // 2:l // 2 + 1])
            y_prev, gf_prev = y.reshape(BATCH, N_TOK, D_MODEL), g_f
        else:
            x_lat, h_packed, logits = res
            i = l // 2
            out = _moe_layer(x_lat.reshape(BATCH * SEQ, D_MODEL), h_packed.reshape(BATCH * SEQ, D_MODEL // 2),
                             logits.reshape(BATCH * SEQ, LANES), g_f[:BATCH],
                             moe_w1[i:i + 1], moe_w3[i:i + 1], moe_w2[i:i + 1])
    return out.reshape(BATCH, SEQ, D_MODEL)
```

```python
import functools
import math

import jax
import jax.numpy as jnp
from jax import lax
from jax.experimental import pallas as pl
from jax.experimental.pallas import tpu as pltpu

F32 = jnp.float32
BF16 = jnp.bfloat16
U32 = jnp.uint32

D_MODEL = 2048
BATCH = 2
SEQ = 4096
DEPTH = 2
GRID_W = 64
CTX_LEN = 256
EPS = 1e-6
ROPE_THETA = 10000.0

A_HEADS = 4
A_DK = 64
A_DV = 128
A_WIDTH = A_HEADS * A_DV
B_WIDTH = 512
C_HEADS = 8
C_KV_HEADS = 2
C_GROUP = C_HEADS // C_KV_HEADS
C_DH = 128
C_WIDTH = C_HEADS * C_DH
C_KV_COLS = C_KV_HEADS * C_DH
WINDOW = 128
MIX_WIDTH = A_WIDTH + B_WIDTH + C_WIDTH
IN_COLS = 3 * A_WIDTH + 3 * B_WIDTH + C_WIDTH + 2 * C_KV_COLS
D_FF = 5632
N_EXPERTS = 8
D_FF_EXPERT = 7168

COL_AQ, COL_AK, COL_AV = 0, 512, 1024
COL_BB, COL_BC, COL_BX = 1536, 2048, 2560
COL_CQ, COL_CK, COL_CV = 3072, 4096, 4352

LANES = 128
N_TOK = CTX_LEN + SEQ
TM = 256
N_TILES = N_TOK // TM
IN_TN = 512
CBLK = 128
N_CBLK = N_TOK // CBLK
A_CHUNK = 1024
A_SPLIT = 4
CAST_STEPS = 128
LOG2E = 1.4426950408889634
FFN_TM = N_TOK // 4
FFN_TF = 256
FFN_SPLIT = 4
MOE_SUB = 256
MOE_NSUB = 10
MOE_FAST = 8
MOE_TF = 512
MOE_ROWS = 2 * BATCH * SEQ + N_EXPERTS * MOE_SUB
MOE_PAIRS = (MOE_ROWS // MOE_SUB + N_EXPERTS * (MOE_NSUB - 1)) // MOE_NSUB
ROUTE_TM = 512
COMB_TM = 256
VMEM_LIMIT = 56 * 1024 * 1024
NEG = -1e30

NT_DIMS = (((1,), (1,)), ((), ()))


def _cparams(*sem):
    return pltpu.CompilerParams(dimension_semantics=sem, vmem_limit_bytes=VMEM_LIMIT)


def _dot(a, b):
    return jnp.dot(a, b, preferred_element_type=F32)


def _rms(x, width):
    return x * lax.rsqrt(jnp.sum(x * x, axis=-1, keepdims=True) * (1.0 / width) + EPS)


ADA_TN = 1024


def _ada_kernel(c_ref, w_ref, b_ref, o_ref):
    a = c_ref[...]
    a = a / (1.0 + jnp.exp(-a))
    o_ref[0] = _dot(a.astype(BF16), w_ref[0].astype(BF16)) + b_ref[0]


def _ada_call(cond, ada_w, ada_b):
    n = 6 * D_MODEL
    return pl.pallas_call(
        _ada_kernel,
        grid=(DEPTH, n // ADA_TN),
        in_specs=[
            pl.BlockSpec((8, D_MODEL), lambda l, j: (0, 0)),
            pl.BlockSpec((1, D_MODEL, ADA_TN), lambda l, j: (l, 0, j)),
            pl.BlockSpec((1, 1, ADA_TN), lambda l, j: (l, 0, j)),
        ],
        out_specs=pl.BlockSpec((1, 8, ADA_TN), lambda l, j: (l, 0, j)),
        out_shape=jax.ShapeDtypeStruct((DEPTH, 8, n), F32),
        compiler_params=_cparams("arbitrary", "arbitrary"),
        name="ada_mod",
    )(cond, ada_w, ada_b.reshape(DEPTH, 1, n))


def _rope128(n, cos, s_odd, s_even, quarter):
    return n * cos + pltpu.roll(n, quarter, 1) * s_odd + pltpu.roll(n, LANES - quarter, 1) * s_even


def _in_kernel(*refs, with_y):
    if with_y:
        x_ref, y_ref, gf_ref = refs[:3]
        refs = refs[3:]
    else:
        x_ref = refs[0]
        refs = refs[1:]
    (g_ref, sc_ref, sh_ref, w_ref, aqn_ref, akn_ref, cqn_ref, ckn_ref,
     ca_ref, sa1_ref, sa2_ref, cc_ref, sc1_ref, sc2_ref) = refs[:14]
    outs = refs[14:]
    if with_y:
        xo_ref, outs = outs[0], outs[1:]
    qa_ref, ka_ref, va_ref, bb_ref, u_ref, qc_ref, kc_ref, vc_ref = outs

    x = x_ref[0]
    if with_y:
        x = x + gf_ref[0] * y_ref[0].astype(F32)
        xo_ref[0] = x
    h = _rms(x, D_MODEL) * g_ref[...]
    h = h * (1.0 + sc_ref[0]) + sh_ref[0]
    hb = h.astype(BF16)

    lane = lax.broadcasted_iota(jnp.int32, (TM, LANES), 1)
    lo_half = lane < A_DK

    def proj(col0, width):
        return _dot(hb, w_ref[:, col0:col0 + width])

    def head_a(p, gain_ref, scale):
        p2 = p * p
        s_all = jnp.sum(p2, axis=-1, keepdims=True)
        s_lo = jnp.sum(jnp.where(lo_half, p2, 0.0), axis=-1, keepdims=True)
        ms = jnp.where(lo_half, s_lo, s_all - s_lo) * (1.0 / A_DK)
        n = p * lax.rsqrt(ms + EPS) * gain_ref[...]
        r = _rope128(n, ca_ref[...], sa1_ref[...], sa2_ref[...], A_DK // 4)
        return (r * scale).astype(BF16)

    def head_c(p, gain_ref, scale):
        n = _rms(p, C_DH) * gain_ref[...]
        r = _rope128(n, cc_ref[...], sc1_ref[...], sc2_ref[...], C_DH // 4)
        return (r * scale).astype(BF16)

    def heads(o_ref, col0, width, fn, gain_ref, scale):
        for c in range(0, width, IN_TN):
            wd = min(IN_TN, width - c)
            p = proj(col0 + c, wd)
            for j in range(wd // LANES):
                o_ref[0, :, c + j * LANES:c + (j + 1) * LANES] = fn(
                    p[:, j * LANES:(j + 1) * LANES], gain_ref, scale)

    heads(qa_ref, COL_AQ, A_WIDTH, head_a, aqn_ref, A_DK ** -0.5 * LOG2E)
    heads(ka_ref, COL_AK, A_WIDTH, head_a, akn_ref, 1.0)
    va_ref[0] = proj(COL_AV, A_WIDTH).astype(BF16)
    bb_ref[0] = proj(COL_BB, B_WIDTH).astype(BF16)
    u_ref[0] = (proj(COL_BC, B_WIDTH) * proj(COL_BX, B_WIDTH)).astype(BF16)
    heads(qc_ref, COL_CQ, C_WIDTH, head_c, cqn_ref, C_DH ** -0.5 * LOG2E)
    heads(kc_ref, COL_CK, C_KV_COLS, head_c, ckn_ref, 1.0)
    vc_ref[0] = proj(COL_CV, C_KV_COLS).astype(BF16)


def _mod_row(b, t):
    return jnp.where(t == 0, BATCH, b)


def _in_call(x, y, gf, g, sc, sh, w, gains, tables):
    with_y = y is not None
    tile = lambda width: pl.BlockSpec((1, TM, width), lambda b, t: (b, t, 0))
    modv = pl.BlockSpec((1, 1, D_MODEL), lambda b, t: (_mod_row(b, t), 0, 0))
    full2 = lambda a: pl.BlockSpec(a.shape, lambda b, t: (0, 0))
    rows = pl.BlockSpec((TM, LANES), lambda b, t: (t, 0))
    in_specs = [tile(D_MODEL)]
    args = [x]
    if with_y:
        in_specs += [tile(D_MODEL), modv]
        args += [y, gf]
    in_specs += [full2(g), modv, modv,
                 pl.BlockSpec(w.shape, lambda b, t: (0, 0), pipeline_mode=pl.Buffered(1))]
    args += [g, sc, sh, w]
    in_specs += [full2(a) for a in gains]
    args += list(gains)
    in_specs += [rows] * 6
    args += list(tables)
    widths = (A_WIDTH, A_WIDTH, A_WIDTH, B_WIDTH, B_WIDTH, C_WIDTH, C_KV_COLS, C_KV_COLS)
    out_specs = [tile(wd) for wd in widths]
    out_shape = [jax.ShapeDtypeStruct((BATCH, N_TOK, wd), BF16) for wd in widths]
    if with_y:
        out_specs = [tile(D_MODEL)] + out_specs
        out_shape = [jax.ShapeDtypeStruct((BATCH, N_TOK, D_MODEL), F32)] + out_shape
    return pl.pallas_call(
        functools.partial(_in_kernel, with_y=with_y),
        grid=(BATCH, N_TILES),
        in_specs=in_specs,
        out_specs=out_specs,
        out_shape=out_shape,
        compiler_params=_cparams("arbitrary", "arbitrary"),
        name="in_proj",
    )(*args)


def _attn_a_kernel(*refs, lam_init, t_off, n_cast):
    q_ref, k_ref, v_ref, lq1_ref, lk1_ref, lq2_ref, lk2_ref, g_ref = refs[:8]
    cast_src = refs[8:8 + n_cast]
    o_ref = refs[8 + n_cast]
    cast_dst = refs[9 + n_cast:9 + 2 * n_cast]
    m_sc, acc_sc = refs[9 + 2 * n_cast:]
    t = pl.program_id(2) + t_off

    step_idx = (pl.program_id(0) * pl.num_programs(1) + pl.program_id(1)) * pl.num_programs(2) + pl.program_id(2)

    @pl.when(step_idx < CAST_STEPS)
    def _():
        for src, dst in zip(cast_src, cast_dst):
            dst[...] = src[...].astype(BF16)

    q = q_ref[0]
    lane = lax.broadcasted_iota(jnp.int32, (TM, LANES), 1)
    zero = jnp.zeros_like(q)
    qs = jnp.concatenate([jnp.where(lane < A_DK, q, zero), jnp.where(lane >= A_DK, q, zero)], axis=0)

    m_sc[...] = jnp.full(m_sc.shape, NEG, F32)
    acc_sc[...] = jnp.zeros(acc_sc.shape, F32)

    def step(k, v):
        v1 = jnp.concatenate([v, jnp.ones_like(v)], axis=1)
        grp = 2 * TM // A_SPLIT
        for g in range(A_SPLIT):
            rows = slice(g * grp, (g + 1) * grp)
            s = lax.dot_general(qs[rows], k, NT_DIMS, preferred_element_type=F32)
            m_prev = m_sc[rows, :]
            m_new = jnp.maximum(m_prev, jnp.max(s, axis=-1, keepdims=True))
            alpha = jnp.exp2(m_prev - m_new)
            p = jnp.exp2(s - jnp.tile(m_new, (1, s.shape[1] // LANES)))
            acc_sc[rows, :] = jnp.tile(alpha, (1, 2)) * acc_sc[rows, :] + _dot(p.astype(BF16), v1)
            m_sc[rows, :] = m_new

    step(k_ref[0, 0:CTX_LEN, :], v_ref[0, 0:CTX_LEN, :])

    def latent_keys():
        for off in range(CTX_LEN, N_TOK, A_CHUNK):
            step(k_ref[0, off:off + A_CHUNK, :], v_ref[0, off:off + A_CHUNK, :])

    if t_off == 0:
        pl.when(t > 0)(latent_keys)
    else:
        latent_keys()

    lam = (jnp.exp(jnp.sum(lq1_ref[...] * lk1_ref[...], keepdims=True))
           - jnp.exp(jnp.sum(lq2_ref[...] * lk2_ref[...], keepdims=True)) + lam_init)
    acc = acc_sc[...]
    o = acc[:, :A_DV] / acc[:, A_DV:]
    o = o[:TM] - lam * o[TM:]
    o_ref[0] = (_rms(o, A_DV) * g_ref[...] * (1.0 - lam_init)).astype(BF16)


def _attn_a_call(qa, ka, va, lam_vecs, subln_g, lam_init, with_ctx, cast):
    t_off = 0 if with_ctx else 1
    nt = N_TILES - t_off
    assert BATCH * A_HEADS * nt >= CAST_STEPS
    vec = pl.BlockSpec((1, A_DK), lambda b, h, t: (0, 0))

    def slab(a):
        rows = a.shape[0] // CAST_STEPS
        assert rows * CAST_STEPS == a.shape[0] and rows % 16 == 0
        return pl.BlockSpec(
            (rows, a.shape[1]),
            lambda b, h, t: (jnp.minimum((b * A_HEADS + h) * nt + t, CAST_STEPS - 1), 0))

    res = pl.pallas_call(
        functools.partial(_attn_a_kernel, lam_init=lam_init, t_off=t_off, n_cast=len(cast)),
        grid=(BATCH, A_HEADS, nt),
        in_specs=[
            pl.BlockSpec((1, TM, LANES), lambda b, h, t: (b, t + t_off, h)),
            pl.BlockSpec((1, N_TOK, LANES), lambda b, h, t: (b, 0, h)),
            pl.BlockSpec((1, N_TOK, LANES), lambda b, h, t: (b, 0, h)),
            vec, vec, vec, vec,
            pl.BlockSpec((1, A_DV), lambda b, h, t: (0, 0)),
        ] + [slab(a) for a in cast],
        out_specs=[pl.BlockSpec((1, TM, LANES), lambda b, h, t: (b, t, h))] + [slab(a) for a in cast],
        out_shape=[jax.ShapeDtypeStruct((BATCH, nt * TM, A_WIDTH), BF16)]
        + [jax.ShapeDtypeStruct(a.shape, BF16) for a in cast],
        scratch_shapes=[pltpu.VMEM((2 * TM, LANES), F32), pltpu.VMEM((2 * TM, 2 * A_DV), F32)],
        compiler_params=_cparams("arbitrary", "arbitrary", "arbitrary"),
        name="attn_diff",
    )(qa, ka, va, *lam_vecs, subln_g, *cast)
    return res[0], res[1:]


def _attn_c_kernel(sink_ref, q_ref, kp_ref, k0_ref, kn_ref, kx_ref, vp_ref, v0_ref, vn_ref, vx_ref,
                   o_ref, *, t_off):
    kvh = pl.program_id(1)
    t = pl.program_id(2) + t_off
    rows = C_GROUP * CBLK
    q4 = jnp.concatenate([q_ref[0, :, g * LANES:(g + 1) * LANES] for g in range(C_GROUP)], axis=0)

    def scores(k_ref):
        return lax.dot_general(q4, k_ref[0], NT_DIMS, preferred_element_type=F32)

    i = lax.broadcasted_iota(jnp.int32, (rows, CBLK), 0) & (CBLK - 1)
    j = lax.broadcasted_iota(jnp.int32, (rows, CBLK), 1)
    n = t - CTX_LEN // CBLK
    n_last = SEQ // CBLK - 1
    far = 4 * CBLK
    off_p = jnp.where(n >= 1, 0, far)
    off_0 = jnp.where(n >= 0, 0, far)
    off_n = jnp.where((n >= 0) & (n < n_last), 0, far)
    s_p = jnp.where(j >= i + off_p, scores(kp_ref), NEG)
    s_0 = jnp.where(j >= off_0, scores(k0_ref), NEG)
    s_n = jnp.where(j + off_n <= i, scores(kn_ref), NEG)
    s_x = scores(kx_ref)

    g_row = lax.broadcasted_iota(jnp.int32, (rows, 1), 0) // CBLK
    sink = jnp.zeros((rows, 1), F32)
    for g in range(C_GROUP):
        sink = jnp.where(g_row == g, sink_ref[kvh * C_GROUP + g] * LOG2E, sink)

    rmax = lambda s: jnp.max(s, axis=-1, keepdims=True)
    m = jnp.maximum(jnp.maximum(jnp.maximum(rmax(s_p), rmax(s_0)), jnp.maximum(rmax(s_n), rmax(s_x))), sink)
    p = jnp.concatenate([jnp.exp2(s - m) for s in (s_p, s_0, s_n, s_x)], axis=1).astype(BF16)
    v = jnp.concatenate([vp_ref[0], v0_ref[0], vn_ref[0], vx_ref[0]], axis=0)
    ol = _dot(p, jnp.concatenate([v, jnp.ones_like(v)], axis=1))
    o = ol[:, :C_DH] / (ol[:, C_DH:] + jnp.exp2(sink - m))
    for g in range(C_GROUP):
        o_ref[0, :, g * LANES:(g + 1) * LANES] = o[g * CBLK:(g + 1) * CBLK].astype(BF16)


def _attn_c_call(sink, qc, kc, vc, with_ctx):
    t_off = 0 if with_ctx else CTX_LEN // CBLK
    nb = N_CBLK - t_off
    prev = pl.BlockSpec((1, CBLK, LANES), lambda b, k, t, s: (b, jnp.maximum(t + t_off - 1, 0), k))
    own = pl.BlockSpec((1, CBLK, LANES), lambda b, k, t, s: (b, t + t_off, k))
    nxt = pl.BlockSpec((1, CBLK, LANES), lambda b, k, t, s: (b, jnp.minimum(t + t_off + 1, N_CBLK - 1), k))
    ctx = pl.BlockSpec((1, CTX_LEN, LANES), lambda b, k, t, s: (b, 0, k))
    qi = pl.BlockSpec((1, CBLK, C_GROUP * C_DH), lambda b, k, t, s: (b, t + t_off, k))
    qo = pl.BlockSpec((1, CBLK, C_GROUP * C_DH), lambda b, k, t, s: (b, t, k))
    return pl.pallas_call(
        functools.partial(_attn_c_kernel, t_off=t_off),
        grid_spec=pltpu.PrefetchScalarGridSpec(
            num_scalar_prefetch=1,
            grid=(BATCH, C_KV_HEADS, nb),
            in_specs=[qi, prev, own, nxt, ctx, prev, own, nxt, ctx],
            out_specs=qo,
        ),
        out_shape=jax.ShapeDtypeStruct((BATCH, nb * CBLK, C_WIDTH), BF16),
        compiler_params=_cparams("arbitrary", "arbitrary", "arbitrary"),
        name="attn_window",
    )(sink, qc, kc, kc, kc, kc, vc, vc, vc, vc)


HALO = 16


def _pack_bf16_pairs(h):
    hb = h.astype(BF16).astype(F32)
    half = h.shape[-1] // 2
    lo = lax.bitcast_convert_type(hb[:, :half], U32)
    hi = lax.bitcast_convert_type(hb[:, half:], U32)
    return (hi & jnp.uint32(0xFFFF0000)) | (lo >> 16)


def _unpack_bf16_pairs(p):
    lo = lax.bitcast_convert_type(p << 16, F32)
    hi = lax.bitcast_convert_type(p & jnp.uint32(0xFFFF0000), F32)
    return lo, hi


def _out_kernel(*refs, t_off, route):
    (x_ref, oa_ref, bb_ref, u_ref, up_ref, un_ref, oc_ref, w_ref, ga_ref, bg_ref, cg_ref, cw_ref,
     gf_ref, scf_ref, shf_ref) = refs[:15]
    refs = refs[15:]
    if route:
        wrh_ref, wrl_ref, xo_ref, h2_ref, lg_ref = refs
    else:
        xo_ref, h2_ref = refs
    t = pl.program_id(1) + t_off

    u = u_ref[0].astype(F32)
    row = lax.broadcasted_iota(jnp.int32, (TM, 1), 0)
    prev_ok = jnp.where(t >= 2, 1.0, 0.0)
    next_ok = jnp.where((t >= 1) & (t < N_TILES - 1), 1.0, 0.0)
    prev_row = up_ref[0, HALO - 1:HALO, :].astype(F32) * prev_ok
    next_row = un_ref[0, 0:1, :].astype(F32) * next_ok
    u_prev = jnp.where(row == 0, prev_row, pltpu.roll(u, 1, 0))
    u_next = jnp.where(row == TM - 1, next_row, pltpu.roll(u, TM - 1, 0))
    conv = u_prev * cw_ref[0:1, :] + u * cw_ref[1:2, :] + u_next * cw_ref[2:3, :]
    yb = _rms(bb_ref[0].astype(F32) * conv, B_WIDTH) * bg_ref[...]
    yc = _rms(oc_ref[0].astype(F32), C_WIDTH) * cg_ref[...]
    y = (_dot(oa_ref[0], w_ref[0:A_WIDTH, :])
         + _dot(yb.astype(BF16), w_ref[A_WIDTH:A_WIDTH + B_WIDTH, :])
         + _dot(yc.astype(BF16), w_ref[A_WIDTH + B_WIDTH:, :]))
    x = x_ref[0] + ga_ref[0] * y
    xo_ref[0] = x
    h2 = _rms(x, D_MODEL) * gf_ref[...]
    h2 = h2 * (1.0 + scf_ref[0]) + shf_ref[0]
    if route:
        h2_ref[0] = _pack_bf16_pairs(h2)
        h_hi = h2.astype(BF16)
        h_lo = (h2 - h_hi.astype(F32)).astype(BF16)
        lg_ref[0] = _dot(h_hi, wrh_ref[...]) + (_dot(h_lo, wrh_ref[...]) + _dot(h_hi, wrl_ref[...]))
    else:
        h2_ref[0] = h2.astype(BF16)


def _out_call(x, oa, bb, u, oc, w, ga, bg, cg, cw, gfn, scf, shf, wr):
    route = wr is not None
    t_off = 1 if route else 0
    nt = N_TILES - t_off
    hpt = TM // HALO
    tile = lambda width: pl.BlockSpec((1, TM, width), lambda b, t: (b, t + t_off, 0))
    otile = lambda width: pl.BlockSpec((1, TM, width), lambda b, t: (b, t, 0))
    modv = pl.BlockSpec((1, 1, D_MODEL), lambda b, t: (_mod_row(b, t + t_off), 0, 0))
    full2 = lambda a: pl.BlockSpec(a.shape, lambda b, t: (0, 0))
    halo_p = pl.BlockSpec((1, HALO, B_WIDTH), lambda b, t: (b, jnp.maximum((t + t_off) * hpt - 1, 0), 0))
    halo_n = pl.BlockSpec((1, HALO, B_WIDTH),
                          lambda b, t: (b, jnp.minimum((t + t_off + 1) * hpt, N_TOK // HALO - 1), 0))
    in_specs = [tile(D_MODEL), otile(A_WIDTH), tile(B_WIDTH), tile(B_WIDTH), halo_p, halo_n, otile(C_WIDTH),
                pl.BlockSpec(w.shape, lambda b, t: (0, 0), pipeline_mode=pl.Buffered(1)),
                modv, full2(bg), full2(cg), full2(cw), full2(gfn), modv, modv]
    args = [x, oa, bb, u, u, u, oc, w, ga, bg, cg, cw, gfn, scf, shf]
    rows_out = nt * TM
    if route:
        wr_hi = wr.astype(BF16)
        wr_lo = (wr - wr_hi.astype(F32)).astype(BF16)
        in_specs += [full2(wr_hi), full2(wr_lo)]
        args += [wr_hi, wr_lo]
        out_specs = [otile(D_MODEL), otile(D_MODEL // 2), otile(LANES)]
        out_shape = [jax.ShapeDtypeStruct((BATCH, rows_out, D_MODEL), F32),
                     jax.ShapeDtypeStruct((BATCH, rows_out, D_MODEL // 2), U32),
                     jax.ShapeDtypeStruct((BATCH, rows_out, LANES), F32)]
    else:
        out_specs = [otile(D_MODEL), otile(D_MODEL)]
        out_shape = [jax.ShapeDtypeStruct((BATCH, rows_out, D_MODEL), F32),
                     jax.ShapeDtypeStruct((BATCH, rows_out, D_MODEL), BF16)]
    return pl.pallas_call(
        functools.partial(_out_kernel, t_off=t_off, route=route),
        grid=(BATCH, nt),
        in_specs=in_specs,
        out_specs=out_specs,
        out_shape=out_shape,
        compiler_params=_cparams("arbitrary", "arbitrary"),
        name="mix_out",
    )(*args)


def _swiglu_accumulate(h_ref, acc_ref, row_blocks, w1, w3, w2, packed=False):
    for rows in row_blocks:
        if packed:
            x_lo, x_hi = _unpack_bf16_pairs(h_ref[rows, :])
            h = jnp.concatenate([x_lo.astype(BF16), x_hi.astype(BF16)], axis=1)
        else:
            h = h_ref[rows, :]
        a = _dot(h, w1)
        b = _dot(h, w3)
        z = a / (1.0 + jnp.exp(-a)) * b
        acc_ref[rows, :] += _dot(z.astype(BF16), w2)


def _ffn_kernel(h_ref, w1_ref, w3_ref, w2_ref, y_ref, acc_ref):
    f = pl.program_id(1)

    @pl.when(f == 0)
    def _():
        acc_ref[...] = jnp.zeros(acc_ref.shape, F32)

    sub = FFN_TM // FFN_SPLIT
    _swiglu_accumulate(h_ref, acc_ref, [slice(i * sub, (i + 1) * sub) for i in range(FFN_SPLIT)],
                       w1_ref[0].astype(BF16), w3_ref[0].astype(BF16), w2_ref[0].astype(BF16))

    @pl.when(f == pl.num_programs(1) - 1)
    def _():
        y_ref[...] = acc_ref[...].astype(BF16)


def _ffn_call(h, w1, w3, w2):
    rows = h.shape[0]
    return pl.pallas_call(
        _ffn_kernel,
        grid=(rows // FFN_TM, D_FF // FFN_TF),
        in_specs=[
            pl.BlockSpec((FFN_TM, D_MODEL), lambda i, f: (i, 0)),
            pl.BlockSpec((1, D_MODEL, FFN_TF), lambda i, f: (0, 0, f)),
            pl.BlockSpec((1, D_MODEL, FFN_TF), lambda i, f: (0, 0, f)),
            pl.BlockSpec((1, FFN_TF, D_MODEL), lambda i, f: (0, f, 0)),
        ],
        out_specs=pl.BlockSpec((FFN_TM, D_MODEL), lambda i, f: (i, 0)),
        out_shape=jax.ShapeDtypeStruct((rows, D_MODEL), BF16),
        scratch_shapes=[pltpu.VMEM((FFN_TM, D_MODEL), F32)],
        compiler_params=_cparams("arbitrary", "arbitrary"),
        name="ffn_dense",
    )(h, w1, w3, w2)


def _route_kernel(lg_ref, meta_ref, cnt_ref, carry_sc):
    @pl.when(pl.program_id(0) == 0)
    def _():
        carry_sc[...] = jnp.zeros(carry_sc.shape, F32)

    lane = lax.broadcasted_iota(jnp.int32, (ROUTE_TM, LANES), 1)
    lane_f = lane.astype(F32)
    lg = jnp.where(lane < N_EXPERTS, lg_ref[...], -jnp.inf)
    m1 = jnp.max(lg, axis=-1, keepdims=True)
    i1 = jnp.min(jnp.where(lg == m1, lane_f, float(LANES)), axis=-1, keepdims=True)
    oh1 = lane_f == i1
    lg2 = jnp.where(oh1, -jnp.inf, lg)
    m2 = jnp.max(lg2, axis=-1, keepdims=True)
    i2 = jnp.min(jnp.where(lg2 == m2, lane_f, float(LANES)), axis=-1, keepdims=True)
    oh2 = lane_f == i2
    e = jnp.exp(m2 - m1)
    g1 = 1.0 / (1.0 + e)
    g2 = e / (1.0 + e)
    oh = jnp.where(oh1, 1.0, 0.0) + jnp.where(oh2, 1.0, 0.0)
    r = lax.broadcasted_iota(jnp.int32, (ROUTE_TM, ROUTE_TM), 0)
    c = lax.broadcasted_iota(jnp.int32, (ROUTE_TM, ROUTE_TM), 1)
    tri = jnp.where(c < r, 1.0, 0.0).astype(BF16)
    before = _dot(tri, oh.astype(BF16)) + carry_sc[...]
    rank1 = jnp.sum(jnp.where(oh1, before, 0.0), axis=-1, keepdims=True)
    rank2 = jnp.sum(jnp.where(oh2, before, 0.0), axis=-1, keepdims=True)
    meta = jnp.zeros((ROUTE_TM, LANES), F32)
    for k, val in enumerate((i1, i2, g1, g2, rank1, rank2)):
        meta = jnp.where(lane == k, val, meta)
    meta_ref[...] = meta
    carry_sc[...] += jnp.sum(oh, axis=0, keepdims=True)
    cnt_ref[...] = carry_sc[...]


def _route_call(logits):
    rows = logits.shape[0]
    return pl.pallas_call(
        _route_kernel,
        grid=(rows // ROUTE_TM,),
        in_specs=[pl.BlockSpec((ROUTE_TM, LANES), lambda i: (i, 0))],
        out_specs=[pl.BlockSpec((ROUTE_TM, LANES), lambda i: (i, 0)),
                   pl.BlockSpec((1, LANES), lambda i: (0, 0))],
        out_shape=[jax.ShapeDtypeStruct((rows, LANES), F32), jax.ShapeDtypeStruct((1, LANES), F32)],
        scratch_shapes=[pltpu.VMEM((1, LANES), F32)],
        compiler_params=_cparams("arbitrary"),
        name="moe_route",
    )(logits)


INV_TM = 512


def _invert_kernel(pos_ref, zeros_ref, inv_ref, sem):
    @pl.when(pl.program_id(0) == 0)
    def _():
        fill = pltpu.make_async_copy(zeros_ref, inv_ref, sem)
        fill.start()
        fill.wait()

    base = pl.program_id(0) * INV_TM

    def scatter(r, carry):
        inv_ref[pos_ref[base + r]] = (base + r) >> 1
        return carry
    lax.fori_loop(0, INV_TM, scatter, 0, unroll=8)


def _invert_call(pos):
    return pl.pallas_call(
        _invert_kernel,
        grid_spec=pltpu.PrefetchScalarGridSpec(
            num_scalar_prefetch=1,
            grid=(pos.shape[0] // INV_TM,),
            in_specs=[pl.BlockSpec(memory_space=pl.ANY)],
            out_specs=pl.BlockSpec(memory_space=pltpu.SMEM),
            scratch_shapes=[pltpu.SemaphoreType.DMA(())],
        ),
        out_shape=jax.ShapeDtypeStruct((MOE_ROWS,), jnp.int32),
        compiler_params=_cparams("arbitrary"),
        name="moe_invert",
    )(pos, jnp.zeros((MOE_ROWS,), jnp.int32))


def _moe_kernel(inv_ref, ps_ref, pn_ref, pd_ref, h_ref, w1_ref, w3_ref, w2_ref, yz_ref, y_ref,
                xg_sc, acc_sc, st_sc, gsem, osem):
    del yz_ref
    p = pl.program_id(0)
    f = pl.program_id(1)
    n_f = pl.num_programs(1)
    start = ps_ref[p]
    n = pn_ref[p]
    subs = [slice(i * MOE_SUB, (i + 1) * MOE_SUB) for i in range(MOE_NSUB)]

    def row_copy(src_row, r):
        return pltpu.make_async_copy(h_ref.at[pl.ds(src_row, 1)], xg_sc.at[pl.ds(r, 1)], gsem)

    @pl.when(f == 0)
    def _():
        def per_sub(fn):
            def body(sub, carry):
                lax.fori_loop(0, MOE_SUB, functools.partial(fn, sub * MOE_SUB), 0, unroll=8)
                return carry
            lax.fori_loop(0, n, body, 0)

        def issue(r0, r, carry):
            row_copy(inv_ref[start * MOE_SUB + r0 + r], r0 + r).start()
            return carry

        def wait(r0, r, carry):
            row_copy(0, r0 + r).wait()
            return carry

        per_sub(issue)
        for sub, rows in enumerate(subs):
            @pl.when(sub < n)
            def _():
                acc_sc[rows, :] = jnp.zeros((MOE_SUB, D_MODEL), F32)
        per_sub(wait)

    def weights():
        return w1_ref[0, 0], w3_ref[0, 0], w2_ref[0, 0]

    def accumulate(row_blocks, w):
        _swiglu_accumulate(xg_sc, acc_sc, row_blocks, *w, packed=True)

    @pl.when(n >= MOE_FAST)
    def _():
        w = weights()
        accumulate(subs[:MOE_FAST], w)
        for sub in range(MOE_FAST, MOE_NSUB):
            @pl.when(sub < n)
            def _():
                accumulate([subs[sub]], w)

    @pl.when((n > 0) & (n < MOE_FAST))
    def _():
        w = weights()
        for sub in range(MOE_FAST):
            @pl.when(sub < n)
            def _():
                accumulate([subs[sub]], w)

    @pl.when(f == n_f - 1)
    def _():
        def out_copy(sub):
            row0 = pl.multiple_of((start + sub) * MOE_SUB, MOE_SUB)
            return pltpu.make_async_copy(st_sc.at[sub % 2], y_ref.at[pl.ds(row0, MOE_SUB)], osem.at[sub % 2])

        for sub, rows in enumerate(subs):
            @pl.when(sub < n)
            def _():
                if sub >= 2:
                    out_copy(sub - 2).wait()
                st_sc[sub % 2] = _pack_bf16_pairs(acc_sc[rows, :])
                out_copy(sub).start()

        for sub in range(MOE_NSUB):
            @pl.when((sub < n) & (sub + 2 >= n))
            def _():
                out_copy(sub).wait()


def _moe_call(inv, pair_tables, h_packed, w1, w3, w2):
    n_f = D_FF_EXPERT // MOE_TF

    def f_eff(p, f, pn):
        return jnp.where(pn[p] > 0, f, n_f - 1)

    w13 = pl.BlockSpec((1, 1, D_MODEL, MOE_TF), lambda p, f, inv, ps, pn, pd: (0, pd[p], 0, f_eff(p, f, pn)))
    w2s = pl.BlockSpec((1, 1, MOE_TF, D_MODEL), lambda p, f, inv, ps, pn, pd: (0, pd[p], f_eff(p, f, pn), 0))
    hbm = pl.BlockSpec(memory_space=pl.ANY)
    win = MOE_NSUB * MOE_SUB
    n_prefetch = 4
    return pl.pallas_call(
        _moe_kernel,
        grid_spec=pltpu.PrefetchScalarGridSpec(
            num_scalar_prefetch=n_prefetch,
            grid=(MOE_PAIRS, n_f),
            in_specs=[hbm, w13, w13, w2s, hbm],
            out_specs=hbm,
            scratch_shapes=[pltpu.VMEM((win, D_MODEL // 2), U32),
                            pltpu.VMEM((win, D_MODEL), F32),
                            pltpu.VMEM((2, MOE_SUB, D_MODEL // 2), U32),
                            pltpu.SemaphoreType.DMA(()),
                            pltpu.SemaphoreType.DMA((2,))],
        ),
        out_shape=jax.ShapeDtypeStruct((MOE_ROWS, D_MODEL // 2), U32),
        input_output_aliases={n_prefetch + 4: 0},
        compiler_params=_cparams("arbitrary", "arbitrary"),
        name="moe_experts",
    )(inv, *pair_tables, h_packed, w1, w3, w2, jnp.zeros((MOE_ROWS, D_MODEL // 2), U32))


def _pair_schedule(counts):
    nsub = (counts + MOE_SUB - 1) // MOE_SUB
    seg_start = jnp.cumsum(nsub) - nsub
    nchunk = (nsub + MOE_NSUB - 1) // MOE_NSUB
    chunk_end = jnp.cumsum(nchunk)
    idx = jnp.arange(MOE_PAIRS, dtype=jnp.int32)
    expert = jnp.sum(chunk_end[None, :] <= idx[:, None], axis=1).astype(jnp.int32)
    active = expert < N_EXPERTS
    onehot = expert[:, None] == jnp.arange(N_EXPERTS, dtype=jnp.int32)[None, :]
    pick = lambda a: jnp.sum(jnp.where(onehot, a[None, :], 0), axis=1)
    c = idx - (pick(chunk_end) - pick(nchunk))
    first = jnp.where(active, pick(seg_start) + MOE_NSUB * c, 0)
    count = jnp.where(active, jnp.minimum(MOE_NSUB, pick(nsub) - MOE_NSUB * c), 0)
    dma_expert = jnp.where(active, expert, jnp.max(jnp.where(active, expert, 0)))
    i32 = lambda a: a.astype(jnp.int32)
    return seg_start * MOE_SUB, tuple(i32(a) for a in (first, count, dma_expert))


def _combine_kernel(pos_ref, x_ref, meta_ref, gf_ref, ys_ref, o_ref, buf, sem):
    base = pl.program_id(0) * COMB_TM

    def copy(r, k, src_row):
        return pltpu.make_async_copy(ys_ref.at[pl.ds(src_row, 1)], buf.at[k, pl.ds(r, 1)], sem)

    def start(r, carry):
        for k in range(2):
            copy(r, k, pos_ref[2 * (base + r) + k]).start()
        return carry

    lax.fori_loop(0, COMB_TM, start, 0, unroll=8)

    def wait(r, carry):
        for k in range(2):
            copy(r, k, 0).wait()
        return carry

    lax.fori_loop(0, COMB_TM, wait, 0, unroll=8)

    half = D_MODEL // 2
    lo0, hi0 = _unpack_bf16_pairs(buf[0])
    lo1, hi1 = _unpack_bf16_pairs(buf[1])
    g1 = meta_ref[:, 2:3]
    g2 = meta_ref[:, 3:4]
    gf = gf_ref[0]
    o_ref[:, :half] = x_ref[:, :half] + gf[:, :half] * (g1 * lo0 + g2 * lo1)
    o_ref[:, half:] = x_ref[:, half:] + gf[:, half:] * (g1 * hi0 + g2 * hi1)


def _combine_call(pos, x, meta, gf, ys):
    rows = x.shape[0]
    tiles_per_batch = SEQ // COMB_TM
    return pl.pallas_call(
        _combine_kernel,
        grid_spec=pltpu.PrefetchScalarGridSpec(
            num_scalar_prefetch=1,
            grid=(rows // COMB_TM,),
            in_specs=[
                pl.BlockSpec((COMB_TM, D_MODEL), lambda i, p: (i, 0)),
                pl.BlockSpec((COMB_TM, LANES), lambda i, p: (i, 0)),
                pl.BlockSpec((1, 1, D_MODEL), lambda i, p: (i // tiles_per_batch, 0, 0)),
                pl.BlockSpec(memory_space=pl.ANY),
            ],
            out_specs=pl.BlockSpec((COMB_TM, D_MODEL), lambda i, p: (i, 0)),
            scratch_shapes=[pltpu.VMEM((2, COMB_TM, D_MODEL // 2), U32), pltpu.SemaphoreType.DMA(())],
        ),
        out_shape=jax.ShapeDtypeStruct((rows, D_MODEL), F32),
        compiler_params=_cparams("arbitrary"),
        name="moe_combine",
    )(pos, x, meta, gf, ys)


def _moe_layer(x, h_packed, logits, gf, w1, w3, w2):
    meta, cnt = _route_call(logits)
    expert = meta[:, 0:2].astype(jnp.int32)
    rank = meta[:, 4:6].astype(jnp.int32)
    counts = cnt[0, :N_EXPERTS].astype(jnp.int32)
    seg_start, pair_tables = _pair_schedule(counts)
    ids = jnp.arange(N_EXPERTS, dtype=jnp.int32)
    pos = rank + jnp.sum(jnp.where(expert[..., None] == ids, seg_start, 0), axis=-1)
    pos = pos.reshape(-1).astype(jnp.int32)
    ys = _moe_call(_invert_call(pos), pair_tables, h_packed, w1, w3, w2)
    return _combine_call(pos, x, meta, gf, ys)


def _rope_tables(dh):
    rows = SEQ // GRID_W
    row = jnp.repeat(jnp.arange(rows, dtype=F32), GRID_W)
    col = jnp.tile(jnp.arange(GRID_W, dtype=F32), rows)
    half = dh // 2
    inv = ROPE_THETA ** (-jnp.arange(0, half, 2, dtype=F32) / half)
    ar = row[:, None] * inv[None, :]
    ac = col[:, None] * inv[None, :]
    ang = jnp.concatenate([ar, ar, ac, ac], axis=-1)
    ang = jnp.concatenate([jnp.zeros((CTX_LEN, dh), F32), ang], axis=0)
    ang = jnp.tile(ang, (1, LANES // dh))
    odd = ((jnp.arange(LANES) // (dh // 4)) % 2 == 1)[None, :]
    sin = jnp.sin(ang)
    return jnp.cos(ang), jnp.where(odd, sin, 0.0), jnp.where(odd, 0.0, -sin)


def kernel(x, c, ctx, c_ctx, ada_w, ada_b, norm_mix_g, norm_ffn_g, w_in, w_out, a_q_norm, a_k_norm,
           a_lam_q1, a_lam_k1, a_lam_q2, a_lam_k2, a_subln_g, b_conv_w, b_out_g, c_q_norm, c_k_norm,
           c_sink, c_out_g, ffn_w1, ffn_w3, ffn_w2, router_w, moe_w1, moe_w3, moe_w2):
    assert x.shape == (BATCH, SEQ, D_MODEL) and ctx.shape == (BATCH, CTX_LEN, D_MODEL)
    cond = jnp.concatenate([c, c_ctx[None, :], jnp.zeros((8 - BATCH - 1, D_MODEL), F32)], axis=0)
    mod = _ada_call(cond, ada_w, ada_b)
    tables = _rope_tables(A_DK) + _rope_tables(C_DH)
    xs = jnp.concatenate([ctx, x], axis=1)
    y_prev = gf_prev = None
    out = None
    moe_bf16 = []
    assert DEPTH == 2 and moe_w1.shape[0] == 1
    for l in range(DEPTH):
        last = l == DEPTH - 1
        sh_a, sc_a, g_a, sh_f, sc_f, g_f = [
            mod[l, :BATCH + 1, k * D_MODEL:(k + 1) * D_MODEL].reshape(BATCH + 1, 1, D_MODEL) for k in range(6)]
        gains = (jnp.tile(a_q_norm[l], LANES // A_DK)[None, :], jnp.tile(a_k_norm[l], LANES // A_DK)[None, :],
                 c_q_norm[l][None, :], c_k_norm[l][None, :])
        res = _in_call(xs, y_prev, gf_prev, norm_mix_g[l][None, :], sc_a, sh_a, w_in[l].astype(BF16),
                       gains, tables)
        if y_prev is not None:
            xs, res = res[0], res[1:]
        qa, ka, va, bb, u, qc, kc, vc = res
        lam_init = 0.8 - 0.6 * math.exp(-0.3 * l)
        lam_vecs = [v[l][None, :] for v in (a_lam_q1, a_lam_k1, a_lam_q2, a_lam_k2)]
        cast_jobs = ((moe_w1[0], moe_w2[0]), (moe_w3[0],))[l]
        oa, cast_out = _attn_a_call(qa, ka, va, lam_vecs, a_subln_g[l][None, :], lam_init, with_ctx=not last,
                                    cast=[w.reshape(-1, w.shape[-1]) for w in cast_jobs])
        moe_bf16 += [c.reshape((1,) + w.shape) for c, w in zip(cast_out, cast_jobs)]
        oc = _attn_c_call(c_sink[l], qc, kc, vc, with_ctx=not last)
        wr = None
        if last:
            wr = jnp.pad(router_w[l // 2], ((0, 0), (0, LANES - N_EXPERTS)))
        res = _out_call(xs, oa, bb, u, oc, w_out[l].astype(BF16), g_a, b_out_g[l][None, :],
                        c_out_g[l][None, :], b_conv_w[l], norm_ffn_g[l][None, :], sc_f, sh_f, wr)
        if not last:
            xs, h2 = res
            y = _ffn_call(h2.reshape(BATCH * N_TOK, D_MODEL), ffn_w1[l // 2:l // 2 + 1],
                          ffn_w3[l // 2:l // 2 + 1], ffn_w2[l // 2:l // 2 + 1])
            y_prev, gf_prev = y.reshape(BATCH, N_TOK, D_MODEL), g_f
        else:
            x_lat, h_packed, logits = res
            w1b, w2b, w3b = moe_bf16
            out = _moe_layer(x_lat.reshape(BATCH * SEQ, D_MODEL), h_packed.reshape(BATCH * SEQ, D_MODEL // 2),
                             logits.reshape(BATCH * SEQ, LANES), g_f[:BATCH], w1b, w3b, w2b)
    return out.reshape(BATCH, SEQ, D_MODEL)
```

```python
import functools
import math

import jax
import jax.numpy as jnp
from jax import lax
from jax.experimental import pallas as pl
from jax.experimental.pallas import tpu as pltpu

F32 = jnp.float32
BF16 = jnp.bfloat16
U32 = jnp.uint32

D_MODEL = 2048
BATCH = 2
SEQ = 4096
DEPTH = 2
GRID_W = 64
CTX_LEN = 256
EPS = 1e-6
ROPE_THETA = 10000.0

A_HEADS = 4
A_DK = 64
A_DV = 128
A_WIDTH = A_HEADS * A_DV
B_WIDTH = 512
C_HEADS = 8
C_KV_HEADS = 2
C_GROUP = C_HEADS // C_KV_HEADS
C_DH = 128
C_WIDTH = C_HEADS * C_DH
C_KV_COLS = C_KV_HEADS * C_DH
WINDOW = 128
MIX_WIDTH = A_WIDTH + B_WIDTH + C_WIDTH
IN_COLS = 3 * A_WIDTH + 3 * B_WIDTH + C_WIDTH + 2 * C_KV_COLS
D_FF = 5632
N_EXPERTS = 8
D_FF_EXPERT = 7168

COL_AQ, COL_AK, COL_AV = 0, 512, 1024
COL_BB, COL_BC, COL_BX = 1536, 2048, 2560
COL_CQ, COL_CK, COL_CV = 3072, 4096, 4352

LANES = 128
N_TOK = CTX_LEN + SEQ
TM = 256
N_TILES = N_TOK // TM
IN_TN = 512
CBLK = 128
N_CBLK = N_TOK // CBLK
A_CHUNK = 1024
A_SPLIT = 4
CAST_STEPS = 128
LOG2E = 1.4426950408889634
FFN_TM = N_TOK // 4
FFN_TF = 512
IN_CAST_STEPS = 32
FFN_SPLIT = 4
MOE_SUB = 256
MOE_NSUB = 10
MOE_FAST = 8
MOE_TF = 512
MOE_ROWS = 2 * BATCH * SEQ + N_EXPERTS * MOE_SUB
MOE_PAIRS = (MOE_ROWS // MOE_SUB + N_EXPERTS * (MOE_NSUB - 1)) // MOE_NSUB
ROUTE_TM = 512
COMB_TM = 256
VMEM_LIMIT = 56 * 1024 * 1024
NEG = -1e30

NT_DIMS = (((1,), (1,)), ((), ()))


def _cparams(*sem):
    return pltpu.CompilerParams(dimension_semantics=sem, vmem_limit_bytes=VMEM_LIMIT)


def _dot(a, b):
    return jnp.dot(a, b, preferred_element_type=F32)


def _rms(x, width):
    return x * lax.rsqrt(jnp.sum(x * x, axis=-1, keepdims=True) * (1.0 / width) + EPS)


def _slab_spec(a, n_steps, step_of):
    rows = a.shape[0] // n_steps
    assert rows * n_steps == a.shape[0] and rows % 16 == 0
    return pl.BlockSpec((rows, a.shape[1]), lambda *g: (jnp.minimum(step_of(*g), n_steps - 1), 0))


def _cast_slabs(step, n_steps, srcs, dsts):
    @pl.when(step < n_steps)
    def _():
        for src, dst in zip(srcs, dsts):
            dst[...] = src[...].astype(BF16)


ADA_TN = 1024


def _ada_kernel(c_ref, w_ref, b_ref, o_ref):
    a = c_ref[...]
    a = a / (1.0 + jnp.exp(-a))
    o_ref[0] = _dot(a.astype(BF16), w_ref[0].astype(BF16)) + b_ref[0]


def _ada_call(cond, ada_w, ada_b):
    n = 6 * D_MODEL
    return pl.pallas_call(
        _ada_kernel,
        grid=(DEPTH, n // ADA_TN),
        in_specs=[
            pl.BlockSpec((8, D_MODEL), lambda l, j: (0, 0)),
            pl.BlockSpec((1, D_MODEL, ADA_TN), lambda l, j: (l, 0, j)),
            pl.BlockSpec((1, 1, ADA_TN), lambda l, j: (l, 0, j)),
        ],
        out_specs=pl.BlockSpec((1, 8, ADA_TN), lambda l, j: (l, 0, j)),
        out_shape=jax.ShapeDtypeStruct((DEPTH, 8, n), F32),
        compiler_params=_cparams("arbitrary", "arbitrary"),
        name="ada_mod",
    )(cond, ada_w, ada_b.reshape(DEPTH, 1, n))


def _rope128(n, cos, s_odd, s_even, quarter):
    return n * cos + pltpu.roll(n, quarter, 1) * s_odd + pltpu.roll(n, LANES - quarter, 1) * s_even


def _in_kernel(*refs, with_y, n_cast):
    if with_y:
        x_ref, y_ref, gf_ref = refs[:3]
    else:
        ctx_ref, x_ref = refs[:2]
    refs = refs[3 if with_y else 2:]
    (g_ref, sc_ref, sh_ref, w_ref, aqn_ref, akn_ref, cqn_ref, ckn_ref,
     ca_ref, sa1_ref, sa2_ref, cc_ref, sc1_ref, sc2_ref) = refs[:14]
    cast_src = refs[14:14 + n_cast]
    outs = refs[14 + n_cast:]
    if with_y:
        xo_ref, outs = outs[0], outs[1:]
    qa_ref, ka_ref, va_ref, bb_ref, u_ref, qc_ref, kc_ref, vc_ref = outs[:8]
    _cast_slabs(pl.program_id(0) * pl.num_programs(1) + pl.program_id(1), IN_CAST_STEPS, cast_src, outs[8:])

    if with_y:
        x = x_ref[0] + gf_ref[0] * y_ref[0].astype(F32)
        xo_ref[0] = x
    else:
        x = jnp.where(pl.program_id(1) == 0, ctx_ref[0], x_ref[0])
    h = _rms(x, D_MODEL) * g_ref[...]
    h = h * (1.0 + sc_ref[0]) + sh_ref[0]
    hb = h.astype(BF16)

    lane = lax.broadcasted_iota(jnp.int32, (TM, LANES), 1)
    lo_half = lane < A_DK

    def proj(col0, width):
        return _dot(hb, w_ref[:, col0:col0 + width])

    def head_a(p, gain_ref, scale):
        p2 = p * p
        s_all = jnp.sum(p2, axis=-1, keepdims=True)
        s_lo = jnp.sum(jnp.where(lo_half, p2, 0.0), axis=-1, keepdims=True)
        ms = jnp.where(lo_half, s_lo, s_all - s_lo) * (1.0 / A_DK)
        n = p * lax.rsqrt(ms + EPS) * gain_ref[...]
        r = _rope128(n, ca_ref[...], sa1_ref[...], sa2_ref[...], A_DK // 4)
        return (r * scale).astype(BF16)

    def head_c(p, gain_ref, scale):
        n = _rms(p, C_DH) * gain_ref[...]
        r = _rope128(n, cc_ref[...], sc1_ref[...], sc2_ref[...], C_DH // 4)
        return (r * scale).astype(BF16)

    def heads(o_ref, col0, width, fn, gain_ref, scale):
        for c in range(0, width, IN_TN):
            wd = min(IN_TN, width - c)
            p = proj(col0 + c, wd)
            for j in range(wd // LANES):
                o_ref[0, :, c + j * LANES:c + (j + 1) * LANES] = fn(
                    p[:, j * LANES:(j + 1) * LANES], gain_ref, scale)

    heads(qa_ref, COL_AQ, A_WIDTH, head_a, aqn_ref, A_DK ** -0.5 * LOG2E)
    heads(ka_ref, COL_AK, A_WIDTH, head_a, akn_ref, 1.0)
    va_ref[0] = proj(COL_AV, A_WIDTH).astype(BF16)
    bb_ref[0] = proj(COL_BB, B_WIDTH).astype(BF16)
    u_ref[0] = (proj(COL_BC, B_WIDTH) * proj(COL_BX, B_WIDTH)).astype(BF16)
    heads(qc_ref, COL_CQ, C_WIDTH, head_c, cqn_ref, C_DH ** -0.5 * LOG2E)
    heads(kc_ref, COL_CK, C_KV_COLS, head_c, ckn_ref, 1.0)
    vc_ref[0] = proj(COL_CV, C_KV_COLS).astype(BF16)


def _mod_row(b, t):
    return jnp.where(t == 0, BATCH, b)


def _split_specs():
    return [pl.BlockSpec((1, TM, D_MODEL), lambda b, t: (b, 0, 0)),
            pl.BlockSpec((1, TM, D_MODEL), lambda b, t: (b, jnp.maximum(t - 1, 0), 0))]


def _in_call(x, y, gf, g, sc, sh, w, gains, tables, cast):
    with_y = y is not None
    tile = lambda width: pl.BlockSpec((1, TM, width), lambda b, t: (b, t, 0))
    modv = pl.BlockSpec((1, 1, D_MODEL), lambda b, t: (_mod_row(b, t), 0, 0))
    full2 = lambda a: pl.BlockSpec(a.shape, lambda b, t: (0, 0))
    rows = pl.BlockSpec((TM, LANES), lambda b, t: (t, 0))
    slab = lambda a: _slab_spec(a, IN_CAST_STEPS, lambda b, t: b * N_TILES + t)
    if with_y:
        in_specs = [tile(D_MODEL), tile(D_MODEL), modv]
        args = [x, y, gf]
    else:
        in_specs = _split_specs()
        args = list(x)
    in_specs += [full2(g), modv, modv,
                 pl.BlockSpec(w.shape, lambda b, t: (0, 0), pipeline_mode=pl.Buffered(1))]
    args += [g, sc, sh, w]
    in_specs += [full2(a) for a in gains]
    args += list(gains)
    in_specs += [rows] * 6
    args += list(tables)
    in_specs += [slab(a) for a in cast]
    args += list(cast)
    widths = (A_WIDTH, A_WIDTH, A_WIDTH, B_WIDTH, B_WIDTH, C_WIDTH, C_KV_COLS, C_KV_COLS)
    out_specs = [tile(wd) for wd in widths] + [slab(a) for a in cast]
    out_shape = ([jax.ShapeDtypeStruct((BATCH, N_TOK, wd), BF16) for wd in widths]
                 + [jax.ShapeDtypeStruct(a.shape, BF16) for a in cast])
    if with_y:
        out_specs = [tile(D_MODEL)] + out_specs
        out_shape = [jax.ShapeDtypeStruct((BATCH, N_TOK, D_MODEL), F32)] + out_shape
    assert BATCH * N_TILES >= IN_CAST_STEPS
    return pl.pallas_call(
        functools.partial(_in_kernel, with_y=with_y, n_cast=len(cast)),
        grid=(BATCH, N_TILES),
        in_specs=in_specs,
        out_specs=out_specs,
        out_shape=out_shape,
        compiler_params=_cparams("arbitrary", "arbitrary"),
        name="in_proj",
    )(*args)


def _attn_a_kernel(*refs, lam_init, t_off, n_cast):
    q_ref, k_ref, v_ref, lq1_ref, lk1_ref, lq2_ref, lk2_ref, g_ref = refs[:8]
    cast_src = refs[8:8 + n_cast]
    o_ref = refs[8 + n_cast]
    cast_dst = refs[9 + n_cast:9 + 2 * n_cast]
    m_sc, acc_sc = refs[9 + 2 * n_cast:]
    t = pl.program_id(2) + t_off

    step_idx = (pl.program_id(0) * pl.num_programs(1) + pl.program_id(1)) * pl.num_programs(2) + pl.program_id(2)
    _cast_slabs(step_idx, CAST_STEPS, cast_src, cast_dst)

    q = q_ref[0]
    lane = lax.broadcasted_iota(jnp.int32, (TM, LANES), 1)
    zero = jnp.zeros_like(q)
    qs = jnp.concatenate([jnp.where(lane < A_DK, q, zero), jnp.where(lane >= A_DK, q, zero)], axis=0)

    m_sc[...] = jnp.full(m_sc.shape, NEG, F32)
    acc_sc[...] = jnp.zeros(acc_sc.shape, F32)

    def step(k, v):
        v1 = jnp.concatenate([v, jnp.ones_like(v)], axis=1)
        grp = 2 * TM // A_SPLIT
        for g in range(A_SPLIT):
            rows = slice(g * grp, (g + 1) * grp)
            s = lax.dot_general(qs[rows], k, NT_DIMS, preferred_element_type=F32)
            m_prev = m_sc[rows, :]
            m_new = jnp.maximum(m_prev, jnp.max(s, axis=-1, keepdims=True))
            alpha = jnp.exp2(m_prev - m_new)
            p = jnp.exp2(s - jnp.tile(m_new, (1, s.shape[1] // LANES)))
            acc_sc[rows, :] = jnp.tile(alpha, (1, 2)) * acc_sc[rows, :] + _dot(p.astype(BF16), v1)
            m_sc[rows, :] = m_new

    step(k_ref[0, 0:CTX_LEN, :], v_ref[0, 0:CTX_LEN, :])

    def latent_keys():
        for off in range(CTX_LEN, N_TOK, A_CHUNK):
            step(k_ref[0, off:off + A_CHUNK, :], v_ref[0, off:off + A_CHUNK, :])

    if t_off == 0:
        pl.when(t > 0)(latent_keys)
    else:
        latent_keys()

    lam = (jnp.exp(jnp.sum(lq1_ref[...] * lk1_ref[...], keepdims=True))
           - jnp.exp(jnp.sum(lq2_ref[...] * lk2_ref[...], keepdims=True)) + lam_init)
    acc = acc_sc[...]
    o = acc[:, :A_DV] / acc[:, A_DV:]
    o = o[:TM] - lam * o[TM:]
    o_ref[0] = (_rms(o, A_DV) * g_ref[...] * (1.0 - lam_init)).astype(BF16)


def _attn_a_call(qa, ka, va, lam_vecs, subln_g, lam_init, with_ctx, cast):
    t_off = 0 if with_ctx else 1
    nt = N_TILES - t_off
    assert BATCH * A_HEADS * nt >= CAST_STEPS
    vec = pl.BlockSpec((1, A_DK), lambda b, h, t: (0, 0))

    def slab(a):
        return _slab_spec(a, CAST_STEPS, lambda b, h, t: (b * A_HEADS + h) * nt + t)

    res = pl.pallas_call(
        functools.partial(_attn_a_kernel, lam_init=lam_init, t_off=t_off, n_cast=len(cast)),
        grid=(BATCH, A_HEADS, nt),
        in_specs=[
            pl.BlockSpec((1, TM, LANES), lambda b, h, t: (b, t + t_off, h)),
            pl.BlockSpec((1, N_TOK, LANES), lambda b, h, t: (b, 0, h)),
            pl.BlockSpec((1, N_TOK, LANES), lambda b, h, t: (b, 0, h)),
            vec, vec, vec, vec,
            pl.BlockSpec((1, A_DV), lambda b, h, t: (0, 0)),
        ] + [slab(a) for a in cast],
        out_specs=[pl.BlockSpec((1, TM, LANES), lambda b, h, t: (b, t, h))] + [slab(a) for a in cast],
        out_shape=[jax.ShapeDtypeStruct((BATCH, nt * TM, A_WIDTH), BF16)]
        + [jax.ShapeDtypeStruct(a.shape, BF16) for a in cast],
        scratch_shapes=[pltpu.VMEM((2 * TM, LANES), F32), pltpu.VMEM((2 * TM, 2 * A_DV), F32)],
        compiler_params=_cparams("arbitrary", "arbitrary", "arbitrary"),
        name="attn_diff",
    )(qa, ka, va, *lam_vecs, subln_g, *cast)
    return res[0], res[1:]


def _attn_c_kernel(sink_ref, q_ref, kp_ref, k0_ref, kn_ref, kx_ref, vp_ref, v0_ref, vn_ref, vx_ref,
                   o_ref, *, t_off):
    kvh = pl.program_id(1)
    t = pl.program_id(2) + t_off
    rows = C_GROUP * CBLK
    q4 = jnp.concatenate([q_ref[0, :, g * LANES:(g + 1) * LANES] for g in range(C_GROUP)], axis=0)

    def scores(k_ref):
        return lax.dot_general(q4, k_ref[0], NT_DIMS, preferred_element_type=F32)

    i = lax.broadcasted_iota(jnp.int32, (rows, CBLK), 0) & (CBLK - 1)
    j = lax.broadcasted_iota(jnp.int32, (rows, CBLK), 1)
    n = t - CTX_LEN // CBLK
    n_last = SEQ // CBLK - 1
    far = 4 * CBLK
    off_p = jnp.where(n >= 1, 0, far)
    off_0 = jnp.where(n >= 0, 0, far)
    off_n = jnp.where((n >= 0) & (n < n_last), 0, far)
    s_p = jnp.where(j >= i + off_p, scores(kp_ref), NEG)
    s_0 = jnp.where(j >= off_0, scores(k0_ref), NEG)
    s_n = jnp.where(j + off_n <= i, scores(kn_ref), NEG)
    s_x = scores(kx_ref)

    g_row = lax.broadcasted_iota(jnp.int32, (rows, 1), 0) // CBLK
    sink = jnp.zeros((rows, 1), F32)
    for g in range(C_GROUP):
        sink = jnp.where(g_row == g, sink_ref[kvh * C_GROUP + g] * LOG2E, sink)

    rmax = lambda s: jnp.max(s, axis=-1, keepdims=True)
    m = jnp.maximum(jnp.maximum(jnp.maximum(rmax(s_p), rmax(s_0)), jnp.maximum(rmax(s_n), rmax(s_x))), sink)
    p = jnp.concatenate([jnp.exp2(s - m) for s in (s_p, s_0, s_n, s_x)], axis=1).astype(BF16)
    v = jnp.concatenate([vp_ref[0], v0_ref[0], vn_ref[0], vx_ref[0]], axis=0)
    ol = _dot(p, jnp.concatenate([v, jnp.ones_like(v)], axis=1))
    o = ol[:, :C_DH] / (ol[:, C_DH:] + jnp.exp2(sink - m))
    for g in range(C_GROUP):
        o_ref[0, :, g * LANES:(g + 1) * LANES] = o[g * CBLK:(g + 1) * CBLK].astype(BF16)


def _attn_c_call(sink, qc, kc, vc, with_ctx):
    t_off = 0 if with_ctx else CTX_LEN // CBLK
    nb = N_CBLK - t_off
    prev = pl.BlockSpec((1, CBLK, LANES), lambda b, k, t, s: (b, jnp.maximum(t + t_off - 1, 0), k))
    own = pl.BlockSpec((1, CBLK, LANES), lambda b, k, t, s: (b, t + t_off, k))
    nxt = pl.BlockSpec((1, CBLK, LANES), lambda b, k, t, s: (b, jnp.minimum(t + t_off + 1, N_CBLK - 1), k))
    ctx = pl.BlockSpec((1, CTX_LEN, LANES), lambda b, k, t, s: (b, 0, k))
    qi = pl.BlockSpec((1, CBLK, C_GROUP * C_DH), lambda b, k, t, s: (b, t + t_off, k))
    qo = pl.BlockSpec((1, CBLK, C_GROUP * C_DH), lambda b, k, t, s: (b, t, k))
    return pl.pallas_call(
        functools.partial(_attn_c_kernel, t_off=t_off),
        grid_spec=pltpu.PrefetchScalarGridSpec(
            num_scalar_prefetch=1,
            grid=(BATCH, C_KV_HEADS, nb),
            in_specs=[qi, prev, own, nxt, ctx, prev, own, nxt, ctx],
            out_specs=qo,
        ),
        out_shape=jax.ShapeDtypeStruct((BATCH, nb * CBLK, C_WIDTH), BF16),
        compiler_params=_cparams("arbitrary", "arbitrary", "arbitrary"),
        name="attn_window",
    )(sink, qc, kc, kc, kc, kc, vc, vc, vc, vc)


HALO = 16


def _pack_bf16_pairs(h):
    hb = h.astype(BF16).astype(F32)
    half = h.shape[-1] // 2
    lo = lax.bitcast_convert_type(hb[:, :half], U32)
    hi = lax.bitcast_convert_type(hb[:, half:], U32)
    return (hi & jnp.uint32(0xFFFF0000)) | (lo >> 16)


def _unpack_bf16_pairs(p):
    lo = lax.bitcast_convert_type(p << 16, F32)
    hi = lax.bitcast_convert_type(p & jnp.uint32(0xFFFF0000), F32)
    return lo, hi


def _out_kernel(*refs, t_off, route):
    if route:
        x_ref = refs[0]
    else:
        ctx_ref, x_ref = refs[:2]
        refs = refs[1:]
    (oa_ref, bb_ref, u_ref, up_ref, un_ref, oc_ref, w_ref, ga_ref, bg_ref, cg_ref, cw_ref,
     gf_ref, scf_ref, shf_ref) = refs[1:15]
    refs = refs[15:]
    if route:
        wrh_ref, wrl_ref, xo_ref, h2_ref, lg_ref = refs
    else:
        xo_ref, h2_ref = refs
    t = pl.program_id(1) + t_off

    u = u_ref[0].astype(F32)
    row = lax.broadcasted_iota(jnp.int32, (TM, 1), 0)
    prev_ok = jnp.where(t >= 2, 1.0, 0.0)
    next_ok = jnp.where((t >= 1) & (t < N_TILES - 1), 1.0, 0.0)
    prev_row = up_ref[0, HALO - 1:HALO, :].astype(F32) * prev_ok
    next_row = un_ref[0, 0:1, :].astype(F32) * next_ok
    u_prev = jnp.where(row == 0, prev_row, pltpu.roll(u, 1, 0))
    u_next = jnp.where(row == TM - 1, next_row, pltpu.roll(u, TM - 1, 0))
    conv = u_prev * cw_ref[0:1, :] + u * cw_ref[1:2, :] + u_next * cw_ref[2:3, :]
    yb = _rms(bb_ref[0].astype(F32) * conv, B_WIDTH) * bg_ref[...]
    yc = _rms(oc_ref[0].astype(F32), C_WIDTH) * cg_ref[...]
    y = (_dot(oa_ref[0], w_ref[0:A_WIDTH, :])
         + _dot(yb.astype(BF16), w_ref[A_WIDTH:A_WIDTH + B_WIDTH, :])
         + _dot(yc.astype(BF16), w_ref[A_WIDTH + B_WIDTH:, :]))
    x_in = x_ref[0] if route else jnp.where(t == 0, ctx_ref[0], x_ref[0])
    x = x_in + ga_ref[0] * y
    xo_ref[0] = x
    h2 = _rms(x, D_MODEL) * gf_ref[...]
    h2 = h2 * (1.0 + scf_ref[0]) + shf_ref[0]
    if route:
        h2_ref[0] = _pack_bf16_pairs(h2)
        h_hi = h2.astype(BF16)
        h_lo = (h2 - h_hi.astype(F32)).astype(BF16)
        lg_ref[0] = _dot(h_hi, wrh_ref[...]) + (_dot(h_lo, wrh_ref[...]) + _dot(h_hi, wrl_ref[...]))
    else:
        h2_ref[0] = h2.astype(BF16)


def _out_call(x, oa, bb, u, oc, w, ga, bg, cg, cw, gfn, scf, shf, wr):
    route = wr is not None
    t_off = 1 if route else 0
    nt = N_TILES - t_off
    hpt = TM // HALO
    tile = lambda width: pl.BlockSpec((1, TM, width), lambda b, t: (b, t + t_off, 0))
    otile = lambda width: pl.BlockSpec((1, TM, width), lambda b, t: (b, t, 0))
    modv = pl.BlockSpec((1, 1, D_MODEL), lambda b, t: (_mod_row(b, t + t_off), 0, 0))
    full2 = lambda a: pl.BlockSpec(a.shape, lambda b, t: (0, 0))
    halo_p = pl.BlockSpec((1, HALO, B_WIDTH), lambda b, t: (b, jnp.maximum((t + t_off) * hpt - 1, 0), 0))
    halo_n = pl.BlockSpec((1, HALO, B_WIDTH),
                          lambda b, t: (b, jnp.minimum((t + t_off + 1) * hpt, N_TOK // HALO - 1), 0))
    x_specs, x_args = ([tile(D_MODEL)], [x]) if route else (_split_specs(), list(x))
    in_specs = x_specs + [otile(A_WIDTH), tile(B_WIDTH), tile(B_WIDTH), halo_p, halo_n, otile(C_WIDTH),
                          pl.BlockSpec(w.shape, lambda b, t: (0, 0), pipeline_mode=pl.Buffered(1)),
                          modv, full2(bg), full2(cg), full2(cw), full2(gfn), modv, modv]
    args = x_args + [oa, bb, u, u, u, oc, w, ga, bg, cg, cw, gfn, scf, shf]
    rows_out = nt * TM
    if route:
        wr_hi = wr.astype(BF16)
        wr_lo = (wr - wr_hi.astype(F32)).astype(BF16)
        in_specs += [full2(wr_hi), full2(wr_lo)]
        args += [wr_hi, wr_lo]
        out_specs = [otile(D_MODEL), otile(D_MODEL // 2), otile(LANES)]
        out_shape = [jax.ShapeDtypeStruct((BATCH, rows_out, D_MODEL), F32),
                     jax.ShapeDtypeStruct((BATCH, rows_out, D_MODEL // 2), U32),
                     jax.ShapeDtypeStruct((BATCH, rows_out, LANES), F32)]
    else:
        out_specs = [otile(D_MODEL), otile(D_MODEL)]
        out_shape = [jax.ShapeDtypeStruct((BATCH, rows_out, D_MODEL), F32),
                     jax.ShapeDtypeStruct((BATCH, rows_out, D_MODEL), BF16)]
    return pl.pallas_call(
        functools.partial(_out_kernel, t_off=t_off, route=route),
        grid=(BATCH, nt),
        in_specs=in_specs,
        out_specs=out_specs,
        out_shape=out_shape,
        compiler_params=_cparams("arbitrary", "arbitrary"),
        name="mix_out",
    )(*args)


def _swiglu_accumulate(h_ref, acc_ref, row_blocks, w1, w3, w2, packed=False):
    for rows in row_blocks:
        if packed:
            x_lo, x_hi = _unpack_bf16_pairs(h_ref[rows, :])
            h = jnp.concatenate([x_lo.astype(BF16), x_hi.astype(BF16)], axis=1)
        else:
            h = h_ref[rows, :]
        a = _dot(h, w1)
        b = _dot(h, w3)
        z = a / (1.0 + jnp.exp(-a)) * b
        acc_ref[rows, :] += _dot(z.astype(BF16), w2)


def _ffn_kernel(h_ref, w1_ref, w3_ref, w2_ref, y_ref, acc_ref):
    f = pl.program_id(1)

    @pl.when(f == 0)
    def _():
        acc_ref[...] = jnp.zeros(acc_ref.shape, F32)

    sub = FFN_TM // FFN_SPLIT
    _swiglu_accumulate(h_ref, acc_ref, [slice(i * sub, (i + 1) * sub) for i in range(FFN_SPLIT)],
                       w1_ref[...], w3_ref[...], w2_ref[...])

    @pl.when(f == pl.num_programs(1) - 1)
    def _():
        y_ref[...] = acc_ref[...].astype(BF16)


def _ffn_call(h, w1, w3, w2):
    rows = h.shape[0]
    return pl.pallas_call(
        _ffn_kernel,
        grid=(rows // FFN_TM, D_FF // FFN_TF),
        in_specs=[
            pl.BlockSpec((FFN_TM, D_MODEL), lambda i, f: (i, 0)),
            pl.BlockSpec((D_MODEL, FFN_TF), lambda i, f: (0, f)),
            pl.BlockSpec((D_MODEL, FFN_TF), lambda i, f: (0, f)),
            pl.BlockSpec((FFN_TF, D_MODEL), lambda i, f: (f, 0)),
        ],
        out_specs=pl.BlockSpec((FFN_TM, D_MODEL), lambda i, f: (i, 0)),
        out_shape=jax.ShapeDtypeStruct((rows, D_MODEL), BF16),
        scratch_shapes=[pltpu.VMEM((FFN_TM, D_MODEL), F32)],
        compiler_params=_cparams("arbitrary", "arbitrary"),
        name="ffn_dense",
    )(h, w1, w3, w2)


def _route_kernel(lg_ref, meta_ref, cnt_ref, carry_sc):
    @pl.when(pl.program_id(0) == 0)
    def _():
        carry_sc[...] = jnp.zeros(carry_sc.shape, F32)

    lane = lax.broadcasted_iota(jnp.int32, (ROUTE_TM, LANES), 1)
    lane_f = lane.astype(F32)
    lg = jnp.where(lane < N_EXPERTS, lg_ref[...], -jnp.inf)
    m1 = jnp.max(lg, axis=-1, keepdims=True)
    i1 = jnp.min(jnp.where(lg == m1, lane_f, float(LANES)), axis=-1, keepdims=True)
    oh1 = lane_f == i1
    lg2 = jnp.where(oh1, -jnp.inf, lg)
    m2 = jnp.max(lg2, axis=-1, keepdims=True)
    i2 = jnp.min(jnp.where(lg2 == m2, lane_f, float(LANES)), axis=-1, keepdims=True)
    oh2 = lane_f == i2
    e = jnp.exp(m2 - m1)
    g1 = 1.0 / (1.0 + e)
    g2 = e / (1.0 + e)
    oh = jnp.where(oh1, 1.0, 0.0) + jnp.where(oh2, 1.0, 0.0)
    r = lax.broadcasted_iota(jnp.int32, (ROUTE_TM, ROUTE_TM), 0)
    c = lax.broadcasted_iota(jnp.int32, (ROUTE_TM, ROUTE_TM), 1)
    tri = jnp.where(c < r, 1.0, 0.0).astype(BF16)
    before = _dot(tri, oh.astype(BF16)) + carry_sc[...]
    rank1 = jnp.sum(jnp.where(oh1, before, 0.0), axis=-1, keepdims=True)
    rank2 = jnp.sum(jnp.where(oh2, before, 0.0), axis=-1, keepdims=True)
    meta = jnp.zeros((ROUTE_TM, LANES), F32)
    for k, val in enumerate((i1, i2, g1, g2, rank1, rank2)):
        meta = jnp.where(lane == k, val, meta)
    meta_ref[...] = meta
    carry_sc[...] += jnp.sum(oh, axis=0, keepdims=True)
    cnt_ref[...] = carry_sc[...]


def _route_call(logits):
    rows = logits.shape[0]
    return pl.pallas_call(
        _route_kernel,
        grid=(rows // ROUTE_TM,),
        in_specs=[pl.BlockSpec((ROUTE_TM, LANES), lambda i: (i, 0))],
        out_specs=[pl.BlockSpec((ROUTE_TM, LANES), lambda i: (i, 0)),
                   pl.BlockSpec((1, LANES), lambda i: (0, 0))],
        out_shape=[jax.ShapeDtypeStruct((rows, LANES), F32), jax.ShapeDtypeStruct((1, LANES), F32)],
        scratch_shapes=[pltpu.VMEM((1, LANES), F32)],
        compiler_params=_cparams("arbitrary"),
        name="moe_route",
    )(logits)


INV_TM = 512


def _invert_kernel(pos_ref, zeros_ref, inv_ref, sem):
    @pl.when(pl.program_id(0) == 0)
    def _():
        fill = pltpu.make_async_copy(zeros_ref, inv_ref, sem)
        fill.start()
        fill.wait()

    base = pl.program_id(0) * INV_TM

    def scatter(r, carry):
        inv_ref[pos_ref[base + r]] = (base + r) >> 1
        return carry
    lax.fori_loop(0, INV_TM, scatter, 0, unroll=8)


def _invert_call(pos):
    return pl.pallas_call(
        _invert_kernel,
        grid_spec=pltpu.PrefetchScalarGridSpec(
            num_scalar_prefetch=1,
            grid=(pos.shape[0] // INV_TM,),
            in_specs=[pl.BlockSpec(memory_space=pl.ANY)],
            out_specs=pl.BlockSpec(memory_space=pltpu.SMEM),
            scratch_shapes=[pltpu.SemaphoreType.DMA(())],
        ),
        out_shape=jax.ShapeDtypeStruct((MOE_ROWS,), jnp.int32),
        compiler_params=_cparams("arbitrary"),
        name="moe_invert",
    )(pos, jnp.zeros((MOE_ROWS,), jnp.int32))


def _moe_kernel(inv_ref, ps_ref, pn_ref, pd_ref, h_ref, w1_ref, w3_ref, w2_ref, yz_ref, y_ref,
                xg_sc, acc_sc, st_sc, gsem, osem):
    del yz_ref
    p = pl.program_id(0)
    f = pl.program_id(1)
    n_f = pl.num_programs(1)
    start = ps_ref[p]
    n = pn_ref[p]
    subs = [slice(i * MOE_SUB, (i + 1) * MOE_SUB) for i in range(MOE_NSUB)]

    def row_copy(src_row, r):
        return pltpu.make_async_copy(h_ref.at[pl.ds(src_row, 1)], xg_sc.at[pl.ds(r, 1)], gsem)

    @pl.when(f == 0)
    def _():
        def per_sub(fn):
            def body(sub, carry):
                lax.fori_loop(0, MOE_SUB, functools.partial(fn, sub * MOE_SUB), 0, unroll=8)
                return carry
            lax.fori_loop(0, n, body, 0)

        def issue(r0, r, carry):
            row_copy(inv_ref[start * MOE_SUB + r0 + r], r0 + r).start()
            return carry

        def wait(r0, r, carry):
            row_copy(0, r0 + r).wait()
            return carry

        per_sub(issue)
        for sub, rows in enumerate(subs):
            @pl.when(sub < n)
            def _():
                acc_sc[rows, :] = jnp.zeros((MOE_SUB, D_MODEL), F32)
        per_sub(wait)

    def weights():
        return w1_ref[0, 0], w3_ref[0, 0], w2_ref[0, 0]

    def accumulate(row_blocks, w):
        _swiglu_accumulate(xg_sc, acc_sc, row_blocks, *w, packed=True)

    @pl.when(n >= MOE_FAST)
    def _():
        w = weights()
        accumulate(subs[:MOE_FAST], w)
        for sub in range(MOE_FAST, MOE_NSUB):
            @pl.when(sub < n)
            def _():
                accumulate([subs[sub]], w)

    @pl.when((n > 0) & (n < MOE_FAST))
    def _():
        w = weights()
        for sub in range(MOE_FAST):
            @pl.when(sub < n)
            def _():
                accumulate([subs[sub]], w)

    @pl.when(f == n_f - 1)
    def _():
        def out_copy(sub):
            row0 = pl.multiple_of((start + sub) * MOE_SUB, MOE_SUB)
            return pltpu.make_async_copy(st_sc.at[sub % 2], y_ref.at[pl.ds(row0, MOE_SUB)], osem.at[sub % 2])

        for sub, rows in enumerate(subs):
            @pl.when(sub < n)
            def _():
                if sub >= 2:
                    out_copy(sub - 2).wait()
                st_sc[sub % 2] = _pack_bf16_pairs(acc_sc[rows, :])
                out_copy(sub).start()

        for sub in range(MOE_NSUB):
            @pl.when((sub < n) & (sub + 2 >= n))
            def _():
                out_copy(sub).wait()


def _moe_call(inv, pair_tables, h_packed, w1, w3, w2):
    n_f = D_FF_EXPERT // MOE_TF

    def f_eff(p, f, pn):
        return jnp.where(pn[p] > 0, f, n_f - 1)

    w13 = pl.BlockSpec((1, 1, D_MODEL, MOE_TF), lambda p, f, inv, ps, pn, pd: (0, pd[p], 0, f_eff(p, f, pn)))
    w2s = pl.BlockSpec((1, 1, MOE_TF, D_MODEL), lambda p, f, inv, ps, pn, pd: (0, pd[p], f_eff(p, f, pn), 0))
    hbm = pl.BlockSpec(memory_space=pl.ANY)
    win = MOE_NSUB * MOE_SUB
    n_prefetch = 4
    return pl.pallas_call(
        _moe_kernel,
        grid_spec=pltpu.PrefetchScalarGridSpec(
            num_scalar_prefetch=n_prefetch,
            grid=(MOE_PAIRS, n_f),
            in_specs=[hbm, w13, w13, w2s, hbm],
            out_specs=hbm,
            scratch_shapes=[pltpu.VMEM((win, D_MODEL // 2), U32),
                            pltpu.VMEM((win, D_MODEL), F32),
                            pltpu.VMEM((2, MOE_SUB, D_MODEL // 2), U32),
                            pltpu.SemaphoreType.DMA(()),
                            pltpu.SemaphoreType.DMA((2,))],
        ),
        out_shape=jax.ShapeDtypeStruct((MOE_ROWS, D_MODEL // 2), U32),
        input_output_aliases={n_prefetch + 4: 0},
        compiler_params=_cparams("arbitrary", "arbitrary"),
        name="moe_experts",
    )(inv, *pair_tables, h_packed, w1, w3, w2, jnp.zeros((MOE_ROWS, D_MODEL // 2), U32))


def _pair_schedule(counts):
    nsub = (counts + MOE_SUB - 1) // MOE_SUB
    seg_start = jnp.cumsum(nsub) - nsub
    nchunk = (nsub + MOE_NSUB - 1) // MOE_NSUB
    chunk_end = jnp.cumsum(nchunk)
    idx = jnp.arange(MOE_PAIRS, dtype=jnp.int32)
    expert = jnp.sum(chunk_end[None, :] <= idx[:, None], axis=1).astype(jnp.int32)
    active = expert < N_EXPERTS
    onehot = expert[:, None] == jnp.arange(N_EXPERTS, dtype=jnp.int32)[None, :]
    pick = lambda a: jnp.sum(jnp.where(onehot, a[None, :], 0), axis=1)
    c = idx - (pick(chunk_end) - pick(nchunk))
    first = jnp.where(active, pick(seg_start) + MOE_NSUB * c, 0)
    count = jnp.where(active, jnp.minimum(MOE_NSUB, pick(nsub) - MOE_NSUB * c), 0)
    dma_expert = jnp.where(active, expert, jnp.max(jnp.where(active, expert, 0)))
    i32 = lambda a: a.astype(jnp.int32)
    return seg_start * MOE_SUB, tuple(i32(a) for a in (first, count, dma_expert))


def _combine_kernel(pos_ref, x_ref, meta_ref, gf_ref, ys_ref, o_ref, buf, sem):
    base = pl.program_id(0) * COMB_TM

    def copy(r, k, src_row):
        return pltpu.make_async_copy(ys_ref.at[pl.ds(src_row, 1)], buf.at[k, pl.ds(r, 1)], sem)

    def start(r, carry):
        for k in range(2):
            copy(r, k, pos_ref[2 * (base + r) + k]).start()
        return carry

    lax.fori_loop(0, COMB_TM, start, 0, unroll=8)

    def wait(r, carry):
        for k in range(2):
            copy(r, k, 0).wait()
        return carry

    lax.fori_loop(0, COMB_TM, wait, 0, unroll=8)

    half = D_MODEL // 2
    lo0, hi0 = _unpack_bf16_pairs(buf[0])
    lo1, hi1 = _unpack_bf16_pairs(buf[1])
    g1 = meta_ref[:, 2:3]
    g2 = meta_ref[:, 3:4]
    gf = gf_ref[0]
    o_ref[:, :half] = x_ref[:, :half] + gf[:, :half] * (g1 * lo0 + g2 * lo1)
    o_ref[:, half:] = x_ref[:, half:] + gf[:, half:] * (g1 * hi0 + g2 * hi1)


def _combine_call(pos, x, meta, gf, ys):
    rows = x.shape[0]
    tiles_per_batch = SEQ // COMB_TM
    return pl.pallas_call(
        _combine_kernel,
        grid_spec=pltpu.PrefetchScalarGridSpec(
            num_scalar_prefetch=1,
            grid=(rows // COMB_TM,),
            in_specs=[
                pl.BlockSpec((COMB_TM, D_MODEL), lambda i, p: (i, 0)),
                pl.BlockSpec((COMB_TM, LANES), lambda i, p: (i, 0)),
                pl.BlockSpec((1, 1, D_MODEL), lambda i, p: (i // tiles_per_batch, 0, 0)),
                pl.BlockSpec(memory_space=pl.ANY),
            ],
            out_specs=pl.BlockSpec((COMB_TM, D_MODEL), lambda i, p: (i, 0)),
            scratch_shapes=[pltpu.VMEM((2, COMB_TM, D_MODEL // 2), U32), pltpu.SemaphoreType.DMA(())],
        ),
        out_shape=jax.ShapeDtypeStruct((rows, D_MODEL), F32),
        compiler_params=_cparams("arbitrary"),
        name="moe_combine",
    )(pos, x, meta, gf, ys)


def _moe_layer(x, h_packed, logits, gf, w1, w3, w2):
    meta, cnt = _route_call(logits)
    expert = meta[:, 0:2].astype(jnp.int32)
    rank = meta[:, 4:6].astype(jnp.int32)
    counts = cnt[0, :N_EXPERTS].astype(jnp.int32)
    seg_start, pair_tables = _pair_schedule(counts)
    ids = jnp.arange(N_EXPERTS, dtype=jnp.int32)
    pos = rank + jnp.sum(jnp.where(expert[..., None] == ids, seg_start, 0), axis=-1)
    pos = pos.reshape(-1).astype(jnp.int32)
    ys = _moe_call(_invert_call(pos), pair_tables, h_packed, w1, w3, w2)
    return _combine_call(pos, x, meta, gf, ys)


def _rope_tables(dh):
    rows = SEQ // GRID_W
    row = jnp.repeat(jnp.arange(rows, dtype=F32), GRID_W)
    col = jnp.tile(jnp.arange(GRID_W, dtype=F32), rows)
    half = dh // 2
    inv = ROPE_THETA ** (-jnp.arange(0, half, 2, dtype=F32) / half)
    ar = row[:, None] * inv[None, :]
    ac = col[:, None] * inv[None, :]
    ang = jnp.concatenate([ar, ar, ac, ac], axis=-1)
    ang = jnp.concatenate([jnp.zeros((CTX_LEN, dh), F32), ang], axis=0)
    ang = jnp.tile(ang, (1, LANES // dh))
    odd = ((jnp.arange(LANES) // (dh // 4)) % 2 == 1)[None, :]
    sin = jnp.sin(ang)
    return jnp.cos(ang), jnp.where(odd, sin, 0.0), jnp.where(odd, 0.0, -sin)


def kernel(x, c, ctx, c_ctx, ada_w, ada_b, norm_mix_g, norm_ffn_g, w_in, w_out, a_q_norm, a_k_norm,
           a_lam_q1, a_lam_k1, a_lam_q2, a_lam_k2, a_subln_g, b_conv_w, b_out_g, c_q_norm, c_k_norm,
           c_sink, c_out_g, ffn_w1, ffn_w3, ffn_w2, router_w, moe_w1, moe_w3, moe_w2):
    assert x.shape == (BATCH, SEQ, D_MODEL) and ctx.shape == (BATCH, CTX_LEN, D_MODEL)
    cond = jnp.concatenate([c, c_ctx[None, :], jnp.zeros((8 - BATCH - 1, D_MODEL), F32)], axis=0)
    mod = _ada_call(cond, ada_w, ada_b)
    tables = _rope_tables(A_DK) + _rope_tables(C_DH)
    xs = (ctx, x)
    y_prev = gf_prev = None
    out = None
    moe_bf16 = []
    assert DEPTH == 2 and moe_w1.shape[0] == 1
    for l in range(DEPTH):
        last = l == DEPTH - 1
        sh_a, sc_a, g_a, sh_f, sc_f, g_f = [
            mod[l, :BATCH + 1, k * D_MODEL:(k + 1) * D_MODEL].reshape(BATCH + 1, 1, D_MODEL) for k in range(6)]
        gains = (jnp.tile(a_q_norm[l], LANES // A_DK)[None, :], jnp.tile(a_k_norm[l], LANES // A_DK)[None, :],
                 c_q_norm[l][None, :], c_k_norm[l][None, :])
        ffn_cast = [] if last else [ffn_w1[l // 2], ffn_w3[l // 2], ffn_w2[l // 2]]
        res = _in_call(xs, y_prev, gf_prev, norm_mix_g[l][None, :], sc_a, sh_a, w_in[l].astype(BF16),
                       gains, tables, ffn_cast)
        if y_prev is not None:
            xs, res = res[0], res[1:]
        qa, ka, va, bb, u, qc, kc, vc = res[:8]
        ffn_bf16 = res[8:]
        lam_init = 0.8 - 0.6 * math.exp(-0.3 * l)
        lam_vecs = [v[l][None, :] for v in (a_lam_q1, a_lam_k1, a_lam_q2, a_lam_k2)]
        cast_jobs = ((moe_w1[0], moe_w2[0]), (moe_w3[0],))[l]
        oa, cast_out = _attn_a_call(qa, ka, va, lam_vecs, a_subln_g[l][None, :], lam_init, with_ctx=not last,
                                    cast=[w.reshape(-1, w.shape[-1]) for w in cast_jobs])
        moe_bf16 += [c.reshape((1,) + w.shape) for c, w in zip(cast_out, cast_jobs)]
        oc = _attn_c_call(c_sink[l], qc, kc, vc, with_ctx=not last)
        wr = None
        if last:
            wr = jnp.pad(router_w[l // 2], ((0, 0), (0, LANES - N_EXPERTS)))
        res = _out_call(xs, oa, bb, u, oc, w_out[l].astype(BF16), g_a, b_out_g[l][None, :],
                        c_out_g[l][None, :], b_conv_w[l], norm_ffn_g[l][None, :], sc_f, sh_f, wr)
        if not last:
            xs, h2 = res
            y = _ffn_call(h2.reshape(BATCH * N_TOK, D_MODEL), *ffn_bf16)
            y_prev, gf_prev = y.reshape(BATCH, N_TOK, D_MODEL), g_f
        else:
            x_lat, h_packed, logits = res
            w1b, w2b, w3b = moe_bf16
            out = _moe_layer(x_lat.reshape(BATCH * SEQ, D_MODEL), h_packed.reshape(BATCH * SEQ, D_MODEL // 2),
                             logits.reshape(BATCH * SEQ, LANES), g_f[:BATCH], w1b, w3b, w2b)
    return out.reshape(BATCH, SEQ, D_MODEL)
```

```python
import functools
import math

import jax
import jax.numpy as jnp
from jax import lax
from jax.experimental import pallas as pl
from jax.experimental.pallas import tpu as pltpu

F32 = jnp.float32
BF16 = jnp.bfloat16
U32 = jnp.uint32

D_MODEL = 2048
BATCH = 2
SEQ = 4096
DEPTH = 2
GRID_W = 64
CTX_LEN = 256
EPS = 1e-6
ROPE_THETA = 10000.0

A_HEADS = 4
A_DK = 64
A_DV = 128
A_WIDTH = A_HEADS * A_DV
B_WIDTH = 512
C_HEADS = 8
C_KV_HEADS = 2
C_GROUP = C_HEADS // C_KV_HEADS
C_DH = 128
C_WIDTH = C_HEADS * C_DH
C_KV_COLS = C_KV_HEADS * C_DH
WINDOW = 128
MIX_WIDTH = A_WIDTH + B_WIDTH + C_WIDTH
IN_COLS = 3 * A_WIDTH + 3 * B_WIDTH + C_WIDTH + 2 * C_KV_COLS
D_FF = 5632
N_EXPERTS = 8
D_FF_EXPERT = 7168

COL_AQ, COL_AK, COL_AV = 0, 512, 1024
COL_BB, COL_BC, COL_BX = 1536, 2048, 2560
COL_CQ, COL_CK, COL_CV = 3072, 4096, 4352

LANES = 128
N_TOK = CTX_LEN + SEQ
TM = 256
N_TILES = N_TOK // TM
IN_TN = 512
CBLK = 128
N_CBLK = N_TOK // CBLK
A_CHUNK = 1024
A_SPLIT = 4
CAST_STEPS = 128
LOG2E = 1.4426950408889634
FFN_TM = N_TOK // 4
FFN_TF = 512
IN_CAST_STEPS = 32
FFN_SPLIT = 4
MOE_SUB = 256
MOE_NSUB = 10
MOE_FAST = 8
MOE_TF = 512
MOE_ROWS = 2 * BATCH * SEQ + N_EXPERTS * MOE_SUB
MOE_PAIRS = (MOE_ROWS // MOE_SUB + N_EXPERTS * (MOE_NSUB - 1)) // MOE_NSUB
ROUTE_TM = 512
COMB_TM = 256
VMEM_LIMIT = 56 * 1024 * 1024
NEG = -1e30

NT_DIMS = (((1,), (1,)), ((), ()))


def _cparams(*sem):
    return pltpu.CompilerParams(dimension_semantics=sem, vmem_limit_bytes=VMEM_LIMIT)


def _dot(a, b):
    return jnp.dot(a, b, preferred_element_type=F32)


def _rms(x, width):
    return x * lax.rsqrt(jnp.sum(x * x, axis=-1, keepdims=True) * (1.0 / width) + EPS)


def _slab_spec(a, n_steps, step_of):
    rows = a.shape[0] // n_steps
    assert rows * n_steps == a.shape[0] and rows % 16 == 0
    return pl.BlockSpec((rows, a.shape[1]), lambda *g: (jnp.minimum(step_of(*g), n_steps - 1), 0))


def _cast_slabs(step, n_steps, srcs, dsts):
    @pl.when(step < n_steps)
    def _():
        for src, dst in zip(srcs, dsts):
            dst[...] = src[...].astype(BF16)


ADA_TN = 1024


def _ada_kernel(c_ref, w_ref, b_ref, o_ref):
    a = c_ref[...]
    a = a / (1.0 + jnp.exp(-a))
    o_ref[0] = _dot(a.astype(BF16), w_ref[0].astype(BF16)) + b_ref[0]


def _ada_call(cond, ada_w, ada_b):
    n = 6 * D_MODEL
    return pl.pallas_call(
        _ada_kernel,
        grid=(DEPTH, n // ADA_TN),
        in_specs=[
            pl.BlockSpec((8, D_MODEL), lambda l, j: (0, 0)),
            pl.BlockSpec((1, D_MODEL, ADA_TN), lambda l, j: (l, 0, j)),
            pl.BlockSpec((1, 1, ADA_TN), lambda l, j: (l, 0, j)),
        ],
        out_specs=pl.BlockSpec((1, 8, ADA_TN), lambda l, j: (l, 0, j)),
        out_shape=jax.ShapeDtypeStruct((DEPTH, 8, n), F32),
        compiler_params=_cparams("arbitrary", "arbitrary"),
        name="ada_mod",
    )(cond, ada_w, ada_b.reshape(DEPTH, 1, n))


def _rope128(n, cos, s_odd, s_even, quarter):
    return n * cos + pltpu.roll(n, quarter, 1) * s_odd + pltpu.roll(n, LANES - quarter, 1) * s_even


def _in_kernel(*refs, with_y, n_cast):
    if with_y:
        x_ref, y_ref, gf_ref = refs[:3]
    else:
        ctx_ref, x_ref = refs[:2]
    refs = refs[3 if with_y else 2:]
    (g_ref, sc_ref, sh_ref, w_ref, aqn_ref, akn_ref, cqn_ref, ckn_ref,
     ca_ref, sa1_ref, sa2_ref, cc_ref, sc1_ref, sc2_ref) = refs[:14]
    cast_src = refs[14:14 + n_cast]
    outs = refs[14 + n_cast:]
    if with_y:
        xo_ref, outs = outs[0], outs[1:]
    qa_ref, ka_ref, va_ref, bb_ref, u_ref, qc_ref, kc_ref, vc_ref = outs[:8]
    _cast_slabs(pl.program_id(0) * pl.num_programs(1) + pl.program_id(1), IN_CAST_STEPS, cast_src, outs[8:])

    if with_y:
        x = x_ref[0] + gf_ref[0] * y_ref[0].astype(F32)
        xo_ref[0] = x
    else:
        x = jnp.where(pl.program_id(1) == 0, ctx_ref[0], x_ref[0])
    h = _rms(x, D_MODEL) * g_ref[...]
    h = h * (1.0 + sc_ref[0]) + sh_ref[0]
    hb = h.astype(BF16)

    lane = lax.broadcasted_iota(jnp.int32, (TM, LANES), 1)
    lo_half = lane < A_DK

    def proj(col0, width):
        return _dot(hb, w_ref[:, col0:col0 + width])

    def head_a(p, gain_ref, scale):
        p2 = p * p
        s_all = jnp.sum(p2, axis=-1, keepdims=True)
        s_lo = jnp.sum(jnp.where(lo_half, p2, 0.0), axis=-1, keepdims=True)
        ms = jnp.where(lo_half, s_lo, s_all - s_lo) * (1.0 / A_DK)
        n = p * lax.rsqrt(ms + EPS) * gain_ref[...]
        r = _rope128(n, ca_ref[...], sa1_ref[...], sa2_ref[...], A_DK // 4)
        return (r * scale).astype(BF16)

    def head_c(p, gain_ref, scale):
        n = _rms(p, C_DH) * gain_ref[...]
        r = _rope128(n, cc_ref[...], sc1_ref[...], sc2_ref[...], C_DH // 4)
        return (r * scale).astype(BF16)

    def heads(o_ref, col0, width, fn, gain_ref, scale):
        for c in range(0, width, IN_TN):
            wd = min(IN_TN, width - c)
            p = proj(col0 + c, wd)
            for j in range(wd // LANES):
                o_ref[0, :, c + j * LANES:c + (j + 1) * LANES] = fn(
                    p[:, j * LANES:(j + 1) * LANES], gain_ref, scale)

    heads(qa_ref, COL_AQ, A_WIDTH, head_a, aqn_ref, A_DK ** -0.5 * LOG2E)
    heads(ka_ref, COL_AK, A_WIDTH, head_a, akn_ref, 1.0)
    va_ref[0] = proj(COL_AV, A_WIDTH).astype(BF16)
    bb_ref[0] = proj(COL_BB, B_WIDTH).astype(BF16)
    u_ref[0] = (proj(COL_BC, B_WIDTH) * proj(COL_BX, B_WIDTH)).astype(BF16)
    heads(qc_ref, COL_CQ, C_WIDTH, head_c, cqn_ref, C_DH ** -0.5 * LOG2E)
    heads(kc_ref, COL_CK, C_KV_COLS, head_c, ckn_ref, 1.0)
    vc_ref[0] = proj(COL_CV, C_KV_COLS).astype(BF16)


def _mod_row(b, t):
    return jnp.where(t == 0, BATCH, b)


def _split_specs():
    return [pl.BlockSpec((1, TM, D_MODEL), lambda b, t: (b, 0, 0)),
            pl.BlockSpec((1, TM, D_MODEL), lambda b, t: (b, jnp.maximum(t - 1, 0), 0))]


def _in_call(x, y, gf, g, sc, sh, w, gains, tables, cast):
    with_y = y is not None
    tile = lambda width: pl.BlockSpec((1, TM, width), lambda b, t: (b, t, 0))
    modv = pl.BlockSpec((1, 1, D_MODEL), lambda b, t: (_mod_row(b, t), 0, 0))
    full2 = lambda a: pl.BlockSpec(a.shape, lambda b, t: (0, 0))
    rows = pl.BlockSpec((TM, LANES), lambda b, t: (t, 0))
    slab = lambda a: _slab_spec(a, IN_CAST_STEPS, lambda b, t: b * N_TILES + t)
    if with_y:
        in_specs = [tile(D_MODEL), tile(D_MODEL), modv]
        args = [x, y, gf]
    else:
        in_specs = _split_specs()
        args = list(x)
    in_specs += [full2(g), modv, modv,
                 pl.BlockSpec(w.shape, lambda b, t: (0, 0), pipeline_mode=pl.Buffered(1))]
    args += [g, sc, sh, w]
    in_specs += [full2(a) for a in gains]
    args += list(gains)
    in_specs += [rows] * 6
    args += list(tables)
    in_specs += [slab(a) for a in cast]
    args += list(cast)
    widths = (A_WIDTH, A_WIDTH, A_WIDTH, B_WIDTH, B_WIDTH, C_WIDTH, C_KV_COLS, C_KV_COLS)
    out_specs = [tile(wd) for wd in widths] + [slab(a) for a in cast]
    out_shape = ([jax.ShapeDtypeStruct((BATCH, N_TOK, wd), BF16) for wd in widths]
                 + [jax.ShapeDtypeStruct(a.shape, BF16) for a in cast])
    if with_y:
        out_specs = [tile(D_MODEL)] + out_specs
        out_shape = [jax.ShapeDtypeStruct((BATCH, N_TOK, D_MODEL), F32)] + out_shape
    assert BATCH * N_TILES >= IN_CAST_STEPS
    return pl.pallas_call(
        functools.partial(_in_kernel, with_y=with_y, n_cast=len(cast)),
        grid=(BATCH, N_TILES),
        in_specs=in_specs,
        out_specs=out_specs,
        out_shape=out_shape,
        compiler_params=_cparams("arbitrary", "arbitrary"),
        name="in_proj",
    )(*args)


def _attn_a_kernel(*refs, lam_init, t_off, n_cast):
    q_ref, k_ref, v_ref, lq1_ref, lk1_ref, lq2_ref, lk2_ref, g_ref = refs[:8]
    cast_src = refs[8:8 + n_cast]
    o_ref = refs[8 + n_cast]
    cast_dst = refs[9 + n_cast:9 + 2 * n_cast]
    m_sc, acc_sc = refs[9 + 2 * n_cast:]
    t = pl.program_id(2) + t_off

    step_idx = (pl.program_id(0) * pl.num_programs(1) + pl.program_id(1)) * pl.num_programs(2) + pl.program_id(2)
    _cast_slabs(step_idx, CAST_STEPS, cast_src, cast_dst)

    q = q_ref[0]
    lane = lax.broadcasted_iota(jnp.int32, (TM, LANES), 1)
    zero = jnp.zeros_like(q)
    qs = jnp.concatenate([jnp.where(lane < A_DK, q, zero), jnp.where(lane >= A_DK, q, zero)], axis=0)

    m_sc[...] = jnp.full(m_sc.shape, NEG, F32)
    acc_sc[...] = jnp.zeros(acc_sc.shape, F32)

    def step(k, v):
        v1 = jnp.concatenate([v, jnp.ones_like(v)], axis=1)
        grp = 2 * TM // A_SPLIT
        for g in range(A_SPLIT):
            rows = slice(g * grp, (g + 1) * grp)
            s = lax.dot_general(qs[rows], k, NT_DIMS, preferred_element_type=F32)
            m_prev = m_sc[rows, :]
            m_new = jnp.maximum(m_prev, jnp.max(s, axis=-1, keepdims=True))
            alpha = jnp.exp2(m_prev - m_new)
            p = jnp.exp2(s - jnp.tile(m_new, (1, s.shape[1] // LANES)))
            acc_sc[rows, :] = jnp.tile(alpha, (1, 2)) * acc_sc[rows, :] + _dot(p.astype(BF16), v1)
            m_sc[rows, :] = m_new

    def keys(bounds):
        for lo, hi in zip(bounds[:-1], bounds[1:]):
            step(k_ref[0, lo:hi, :], v_ref[0, lo:hi, :])

    all_keys = functools.partial(keys, [0] + list(range(CTX_LEN + A_CHUNK, N_TOK + 1, A_CHUNK)))
    if t_off == 0:
        pl.when(t == 0)(functools.partial(keys, [0, CTX_LEN]))
        pl.when(t > 0)(all_keys)
    else:
        all_keys()

    lam = (jnp.exp(jnp.sum(lq1_ref[...] * lk1_ref[...], keepdims=True))
           - jnp.exp(jnp.sum(lq2_ref[...] * lk2_ref[...], keepdims=True)) + lam_init)
    acc = acc_sc[...]
    o = acc[:, :A_DV] / acc[:, A_DV:]
    o = o[:TM] - lam * o[TM:]
    o_ref[0] = (_rms(o, A_DV) * g_ref[...] * (1.0 - lam_init)).astype(BF16)


def _attn_a_call(qa, ka, va, lam_vecs, subln_g, lam_init, with_ctx, cast):
    t_off = 0 if with_ctx else 1
    nt = N_TILES - t_off
    assert BATCH * A_HEADS * nt >= CAST_STEPS
    vec = pl.BlockSpec((1, A_DK), lambda b, h, t: (0, 0))

    def slab(a):
        return _slab_spec(a, CAST_STEPS, lambda b, h, t: (b * A_HEADS + h) * nt + t)

    res = pl.pallas_call(
        functools.partial(_attn_a_kernel, lam_init=lam_init, t_off=t_off, n_cast=len(cast)),
        grid=(BATCH, A_HEADS, nt),
        in_specs=[
            pl.BlockSpec((1, TM, LANES), lambda b, h, t: (b, t + t_off, h)),
            pl.BlockSpec((1, N_TOK, LANES), lambda b, h, t: (b, 0, h)),
            pl.BlockSpec((1, N_TOK, LANES), lambda b, h, t: (b, 0, h)),
            vec, vec, vec, vec,
            pl.BlockSpec((1, A_DV), lambda b, h, t: (0, 0)),
        ] + [slab(a) for a in cast],
        out_specs=[pl.BlockSpec((1, TM, LANES), lambda b, h, t: (b, t, h))] + [slab(a) for a in cast],
        out_shape=[jax.ShapeDtypeStruct((BATCH, nt * TM, A_WIDTH), BF16)]
        + [jax.ShapeDtypeStruct(a.shape, BF16) for a in cast],
        scratch_shapes=[pltpu.VMEM((2 * TM, LANES), F32), pltpu.VMEM((2 * TM, 2 * A_DV), F32)],
        compiler_params=_cparams("arbitrary", "arbitrary", "arbitrary"),
        name="attn_diff",
    )(qa, ka, va, *lam_vecs, subln_g, *cast)
    return res[0], res[1:]


def _attn_c_kernel(sink_ref, q_ref, kp_ref, k0_ref, kn_ref, kx_ref, vp_ref, v0_ref, vn_ref, vx_ref,
                   o_ref, *, t_off):
    t = pl.program_id(1) + t_off
    rows = C_GROUP * CBLK
    i = lax.broadcasted_iota(jnp.int32, (rows, CBLK), 0) & (CBLK - 1)
    j = lax.broadcasted_iota(jnp.int32, (rows, CBLK), 1)
    n = t - CTX_LEN // CBLK
    n_last = SEQ // CBLK - 1
    far = 4 * CBLK
    off_p = jnp.where(n >= 1, 0, far)
    off_0 = jnp.where(n >= 0, 0, far)
    off_n = jnp.where((n >= 0) & (n < n_last), 0, far)
    g_row = lax.broadcasted_iota(jnp.int32, (rows, 1), 0) // CBLK
    rmax = lambda s: jnp.max(s, axis=-1, keepdims=True)

    for kvh in range(C_KV_HEADS):
        col = slice(kvh * C_DH, (kvh + 1) * C_DH)
        q0 = kvh * C_GROUP * C_DH
        q4 = jnp.concatenate([q_ref[0, :, q0 + g * LANES:q0 + (g + 1) * LANES] for g in range(C_GROUP)], axis=0)
        k_loc = jnp.concatenate([kp_ref[0, :, col], k0_ref[0, :, col], kn_ref[0, :, col]], axis=0)
        s_loc = lax.dot_general(q4, k_loc, NT_DIMS, preferred_element_type=F32)
        s_p = jnp.where(j >= i + off_p, s_loc[:, :CBLK], NEG)
        s_0 = jnp.where(j >= off_0, s_loc[:, CBLK:2 * CBLK], NEG)
        s_n = jnp.where(j + off_n <= i, s_loc[:, 2 * CBLK:], NEG)
        s_x = lax.dot_general(q4, kx_ref[0, :, col], NT_DIMS, preferred_element_type=F32)

        sink = jnp.zeros((rows, 1), F32)
        for g in range(C_GROUP):
            sink = jnp.where(g_row == g, sink_ref[kvh * C_GROUP + g] * LOG2E, sink)

        m = jnp.maximum(jnp.maximum(jnp.maximum(rmax(s_p), rmax(s_0)), jnp.maximum(rmax(s_n), rmax(s_x))), sink)
        p = jnp.concatenate([jnp.exp2(s - m) for s in (s_p, s_0, s_n, s_x)], axis=1).astype(BF16)
        v = jnp.concatenate([vp_ref[0, :, col], v0_ref[0, :, col], vn_ref[0, :, col], vx_ref[0, :, col]], axis=0)
        ol = _dot(p, jnp.concatenate([v, jnp.ones_like(v)], axis=1))
        o = ol[:, :C_DH] / (ol[:, C_DH:] + jnp.exp2(sink - m))
        for g in range(C_GROUP):
            o_ref[0, :, q0 + g * LANES:q0 + (g + 1) * LANES] = o[g * CBLK:(g + 1) * CBLK].astype(BF16)


def _attn_c_call(sink, qc, kc, vc, with_ctx):
    t_off = 0 if with_ctx else CTX_LEN // CBLK
    nb = N_CBLK - t_off
    prev = pl.BlockSpec((1, CBLK, C_KV_COLS), lambda b, t, s: (b, jnp.maximum(t + t_off - 1, 0), 0))
    own = pl.BlockSpec((1, CBLK, C_KV_COLS), lambda b, t, s: (b, t + t_off, 0))
    nxt = pl.BlockSpec((1, CBLK, C_KV_COLS), lambda b, t, s: (b, jnp.minimum(t + t_off + 1, N_CBLK - 1), 0))
    ctx = pl.BlockSpec((1, CTX_LEN, C_KV_COLS), lambda b, t, s: (b, 0, 0))
    qi = pl.BlockSpec((1, CBLK, C_WIDTH), lambda b, t, s: (b, t + t_off, 0))
    qo = pl.BlockSpec((1, CBLK, C_WIDTH), lambda b, t, s: (b, t, 0))
    return pl.pallas_call(
        functools.partial(_attn_c_kernel, t_off=t_off),
        grid_spec=pltpu.PrefetchScalarGridSpec(
            num_scalar_prefetch=1,
            grid=(BATCH, nb),
            in_specs=[qi, prev, own, nxt, ctx, prev, own, nxt, ctx],
            out_specs=qo,
        ),
        out_shape=jax.ShapeDtypeStruct((BATCH, nb * CBLK, C_WIDTH), BF16),
        compiler_params=_cparams("arbitrary", "arbitrary"),
        name="attn_window",
    )(sink, qc, kc, kc, kc, kc, vc, vc, vc, vc)


HALO = 16


def _pack_bf16_pairs(h):
    hb = h.astype(BF16).astype(F32)
    half = h.shape[-1] // 2
    lo = lax.bitcast_convert_type(hb[:, :half], U32)
    hi = lax.bitcast_convert_type(hb[:, half:], U32)
    return (hi & jnp.uint32(0xFFFF0000)) | (lo >> 16)


def _unpack_bf16_pairs(p):
    lo = lax.bitcast_convert_type(p << 16, F32)
    hi = lax.bitcast_convert_type(p & jnp.uint32(0xFFFF0000), F32)
    return lo, hi


def _out_kernel(*refs, t_off, route):
    if route:
        x_ref = refs[0]
    else:
        ctx_ref, x_ref = refs[:2]
        refs = refs[1:]
    (oa_ref, bb_ref, u_ref, up_ref, un_ref, oc_ref, w_ref, ga_ref, bg_ref, cg_ref, cw_ref,
     gf_ref, scf_ref, shf_ref) = refs[1:15]
    refs = refs[15:]
    if route:
        wrh_ref, wrl_ref, xo_ref, h2_ref, lg_ref = refs
    else:
        xo_ref, h2_ref = refs
    t = pl.program_id(1) + t_off

    u = u_ref[0].astype(F32)
    row = lax.broadcasted_iota(jnp.int32, (TM, 1), 0)
    prev_ok = jnp.where(t >= 2, 1.0, 0.0)
    next_ok = jnp.where((t >= 1) & (t < N_TILES - 1), 1.0, 0.0)
    prev_row = up_ref[0, HALO - 1:HALO, :].astype(F32) * prev_ok
    next_row = un_ref[0, 0:1, :].astype(F32) * next_ok
    u_prev = jnp.where(row == 0, prev_row, pltpu.roll(u, 1, 0))
    u_next = jnp.where(row == TM - 1, next_row, pltpu.roll(u, TM - 1, 0))
    conv = u_prev * cw_ref[0:1, :] + u * cw_ref[1:2, :] + u_next * cw_ref[2:3, :]
    yb = _rms(bb_ref[0].astype(F32) * conv, B_WIDTH) * bg_ref[...]
    yc = _rms(oc_ref[0].astype(F32), C_WIDTH) * cg_ref[...]
    y = (_dot(oa_ref[0], w_ref[0:A_WIDTH, :])
         + _dot(yb.astype(BF16), w_ref[A_WIDTH:A_WIDTH + B_WIDTH, :])
         + _dot(yc.astype(BF16), w_ref[A_WIDTH + B_WIDTH:, :]))
    x_in = x_ref[0] if route else jnp.where(t == 0, ctx_ref[0], x_ref[0])
    x = x_in + ga_ref[0] * y
    xo_ref[0] = x
    h2 = _rms(x, D_MODEL) * gf_ref[...]
    h2 = h2 * (1.0 + scf_ref[0]) + shf_ref[0]
    if route:
        h2_ref[0] = _pack_bf16_pairs(h2)
        h_hi = h2.astype(BF16)
        h_lo = (h2 - h_hi.astype(F32)).astype(BF16)
        lg_ref[0] = _dot(h_hi, wrh_ref[...]) + (_dot(h_lo, wrh_ref[...]) + _dot(h_hi, wrl_ref[...]))
    else:
        h2_ref[0] = h2.astype(BF16)


def _out_call(x, oa, bb, u, oc, w, ga, bg, cg, cw, gfn, scf, shf, wr):
    route = wr is not None
    t_off = 1 if route else 0
    nt = N_TILES - t_off
    hpt = TM // HALO
    tile = lambda width: pl.BlockSpec((1, TM, width), lambda b, t: (b, t + t_off, 0))
    otile = lambda width: pl.BlockSpec((1, TM, width), lambda b, t: (b, t, 0))
    modv = pl.BlockSpec((1, 1, D_MODEL), lambda b, t: (_mod_row(b, t + t_off), 0, 0))
    full2 = lambda a: pl.BlockSpec(a.shape, lambda b, t: (0, 0))
    halo_p = pl.BlockSpec((1, HALO, B_WIDTH), lambda b, t: (b, jnp.maximum((t + t_off) * hpt - 1, 0), 0))
    halo_n = pl.BlockSpec((1, HALO, B_WIDTH),
                          lambda b, t: (b, jnp.minimum((t + t_off + 1) * hpt, N_TOK // HALO - 1), 0))
    x_specs, x_args = ([tile(D_MODEL)], [x]) if route else (_split_specs(), list(x))
    in_specs = x_specs + [otile(A_WIDTH), tile(B_WIDTH), tile(B_WIDTH), halo_p, halo_n, otile(C_WIDTH),
                          pl.BlockSpec(w.shape, lambda b, t: (0, 0), pipeline_mode=pl.Buffered(1)),
                          modv, full2(bg), full2(cg), full2(cw), full2(gfn), modv, modv]
    args = x_args + [oa, bb, u, u, u, oc, w, ga, bg, cg, cw, gfn, scf, shf]
    rows_out = nt * TM
    if route:
        wr_hi = wr.astype(BF16)
        wr_lo = (wr - wr_hi.astype(F32)).astype(BF16)
        in_specs += [full2(wr_hi), full2(wr_lo)]
        args += [wr_hi, wr_lo]
        out_specs = [otile(D_MODEL), otile(D_MODEL // 2), otile(LANES)]
        out_shape = [jax.ShapeDtypeStruct((BATCH, rows_out, D_MODEL), F32),
                     jax.ShapeDtypeStruct((BATCH, rows_out, D_MODEL // 2), U32),
                     jax.ShapeDtypeStruct((BATCH, rows_out, LANES), F32)]
    else:
        out_specs = [otile(D_MODEL), otile(D_MODEL)]
        out_shape = [jax.ShapeDtypeStruct((BATCH, rows_out, D_MODEL), F32),
                     jax.ShapeDtypeStruct((BATCH, rows_out, D_MODEL), BF16)]
    return pl.pallas_call(
        functools.partial(_out_kernel, t_off=t_off, route=route),
        grid=(BATCH, nt),
        in_specs=in_specs,
        out_specs=out_specs,
        out_shape=out_shape,
        compiler_params=_cparams("arbitrary", "arbitrary"),
        name="mix_out",
    )(*args)


def _swiglu_accumulate(h_ref, acc_ref, row_blocks, w1, w3, w2, packed=False):
    for rows in row_blocks:
        if packed:
            x_lo, x_hi = _unpack_bf16_pairs(h_ref[rows, :])
            h = jnp.concatenate([x_lo.astype(BF16), x_hi.astype(BF16)], axis=1)
        else:
            h = h_ref[rows, :]
        a = _dot(h, w1)
        b = _dot(h, w3)
        z = a / (1.0 + jnp.exp(-a)) * b
        acc_ref[rows, :] += _dot(z.astype(BF16), w2)


def _ffn_kernel(h_ref, w1_ref, w3_ref, w2_ref, y_ref, acc_ref):
    f = pl.program_id(1)

    @pl.when(f == 0)
    def _():
        acc_ref[...] = jnp.zeros(acc_ref.shape, F32)

    sub = FFN_TM // FFN_SPLIT
    _swiglu_accumulate(h_ref, acc_ref, [slice(i * sub, (i + 1) * sub) for i in range(FFN_SPLIT)],
                       w1_ref[...], w3_ref[...], w2_ref[...])

    @pl.when(f == pl.num_programs(1) - 1)
    def _():
        y_ref[...] = acc_ref[...].astype(BF16)


def _ffn_call(h, w1, w3, w2):
    rows = h.shape[0]
    return pl.pallas_call(
        _ffn_kernel,
        grid=(rows // FFN_TM, D_FF // FFN_TF),
        in_specs=[
            pl.BlockSpec((FFN_TM, D_MODEL), lambda i, f: (i, 0)),
            pl.BlockSpec((D_MODEL, FFN_TF), lambda i, f: (0, f)),
            pl.BlockSpec((D_MODEL, FFN_TF), lambda i, f: (0, f)),
            pl.BlockSpec((FFN_TF, D_MODEL), lambda i, f: (f, 0)),
        ],
        out_specs=pl.BlockSpec((FFN_TM, D_MODEL), lambda i, f: (i, 0)),
        out_shape=jax.ShapeDtypeStruct((rows, D_MODEL), BF16),
        scratch_shapes=[pltpu.VMEM((FFN_TM, D_MODEL), F32)],
        compiler_params=_cparams("arbitrary", "arbitrary"),
        name="ffn_dense",
    )(h, w1, w3, w2)


def _route_kernel(lg_ref, meta_ref, cnt_ref, carry_sc):
    @pl.when(pl.program_id(0) == 0)
    def _():
        carry_sc[...] = jnp.zeros(carry_sc.shape, F32)

    lane = lax.broadcasted_iota(jnp.int32, (ROUTE_TM, LANES), 1)
    lane_f = lane.astype(F32)
    lg = jnp.where(lane < N_EXPERTS, lg_ref[...], -jnp.inf)
    m1 = jnp.max(lg, axis=-1, keepdims=True)
    i1 = jnp.min(jnp.where(lg == m1, lane_f, float(LANES)), axis=-1, keepdims=True)
    oh1 = lane_f == i1
    lg2 = jnp.where(oh1, -jnp.inf, lg)
    m2 = jnp.max(lg2, axis=-1, keepdims=True)
    i2 = jnp.min(jnp.where(lg2 == m2, lane_f, float(LANES)), axis=-1, keepdims=True)
    oh2 = lane_f == i2
    e = jnp.exp(m2 - m1)
    g1 = 1.0 / (1.0 + e)
    g2 = e / (1.0 + e)
    oh = jnp.where(oh1, 1.0, 0.0) + jnp.where(oh2, 1.0, 0.0)
    r = lax.broadcasted_iota(jnp.int32, (ROUTE_TM, ROUTE_TM), 0)
    c = lax.broadcasted_iota(jnp.int32, (ROUTE_TM, ROUTE_TM), 1)
    tri = jnp.where(c < r, 1.0, 0.0).astype(BF16)
    before = _dot(tri, oh.astype(BF16)) + carry_sc[...]
    rank1 = jnp.sum(jnp.where(oh1, before, 0.0), axis=-1, keepdims=True)
    rank2 = jnp.sum(jnp.where(oh2, before, 0.0), axis=-1, keepdims=True)
    meta = jnp.zeros((ROUTE_TM, LANES), F32)
    for k, val in enumerate((i1, i2, g1, g2, rank1, rank2)):
        meta = jnp.where(lane == k, val, meta)
    meta_ref[...] = meta
    carry_sc[...] += jnp.sum(oh, axis=0, keepdims=True)
    cnt_ref[...] = carry_sc[...]


def _route_call(logits):
    rows = logits.shape[0]
    return pl.pallas_call(
        _route_kernel,
        grid=(rows // ROUTE_TM,),
        in_specs=[pl.BlockSpec((ROUTE_TM, LANES), lambda i: (i, 0))],
        out_specs=[pl.BlockSpec((ROUTE_TM, LANES), lambda i: (i, 0)),
                   pl.BlockSpec((1, LANES), lambda i: (0, 0))],
        out_shape=[jax.ShapeDtypeStruct((rows, LANES), F32), jax.ShapeDtypeStruct((1, LANES), F32)],
        scratch_shapes=[pltpu.VMEM((1, LANES), F32)],
        compiler_params=_cparams("arbitrary"),
        name="moe_route",
    )(logits)


INV_TM = 512


def _invert_kernel(pos_ref, zeros_ref, inv_ref, sem):
    @pl.when(pl.program_id(0) == 0)
    def _():
        fill = pltpu.make_async_copy(zeros_ref, inv_ref, sem)
        fill.start()
        fill.wait()

    base = pl.program_id(0) * INV_TM

    def scatter(r, carry):
        inv_ref[pos_ref[base + r]] = (base + r) >> 1
        return carry
    lax.fori_loop(0, INV_TM, scatter, 0, unroll=8)


def _invert_call(pos):
    return pl.pallas_call(
        _invert_kernel,
        grid_spec=pltpu.PrefetchScalarGridSpec(
            num_scalar_prefetch=1,
            grid=(pos.shape[0] // INV_TM,),
            in_specs=[pl.BlockSpec(memory_space=pl.ANY)],
            out_specs=pl.BlockSpec(memory_space=pltpu.SMEM),
            scratch_shapes=[pltpu.SemaphoreType.DMA(())],
        ),
        out_shape=jax.ShapeDtypeStruct((MOE_ROWS,), jnp.int32),
        compiler_params=_cparams("arbitrary"),
        name="moe_invert",
    )(pos, jnp.zeros((MOE_ROWS,), jnp.int32))


def _moe_kernel(inv_ref, ps_ref, pn_ref, pd_ref, h_ref, w1_ref, w3_ref, w2_ref, yz_ref, y_ref,
                xg_sc, acc_sc, st_sc, gsem, osem):
    del yz_ref
    p = pl.program_id(0)
    f = pl.program_id(1)
    n_f = pl.num_programs(1)
    start = ps_ref[p]
    n = pn_ref[p]
    subs = [slice(i * MOE_SUB, (i + 1) * MOE_SUB) for i in range(MOE_NSUB)]

    def row_copy(src_row, r):
        return pltpu.make_async_copy(h_ref.at[pl.ds(src_row, 1)], xg_sc.at[pl.ds(r, 1)], gsem)

    @pl.when(f == 0)
    def _():
        def per_sub(fn):
            def body(sub, carry):
                lax.fori_loop(0, MOE_SUB, functools.partial(fn, sub * MOE_SUB), 0, unroll=8)
                return carry
            lax.fori_loop(0, n, body, 0)

        def issue(r0, r, carry):
            row_copy(inv_ref[start * MOE_SUB + r0 + r], r0 + r).start()
            return carry

        def wait(r0, r, carry):
            row_copy(0, r0 + r).wait()
            return carry

        per_sub(issue)
        for sub, rows in enumerate(subs):
            @pl.when(sub < n)
            def _():
                acc_sc[rows, :] = jnp.zeros((MOE_SUB, D_MODEL), F32)
        per_sub(wait)

    def weights():
        return w1_ref[0, 0], w3_ref[0, 0], w2_ref[0, 0]

    def accumulate(row_blocks, w):
        _swiglu_accumulate(xg_sc, acc_sc, row_blocks, *w, packed=True)

    @pl.when(n >= MOE_FAST)
    def _():
        w = weights()
        accumulate(subs[:MOE_FAST], w)
        for sub in range(MOE_FAST, MOE_NSUB):
            @pl.when(sub < n)
            def _():
                accumulate([subs[sub]], w)

    @pl.when((n > 0) & (n < MOE_FAST))
    def _():
        w = weights()
        for sub in range(MOE_FAST):
            @pl.when(sub < n)
            def _():
                accumulate([subs[sub]], w)

    @pl.when(f == n_f - 1)
    def _():
        def out_copy(sub):
            row0 = pl.multiple_of((start + sub) * MOE_SUB, MOE_SUB)
            return pltpu.make_async_copy(st_sc.at[sub % 2], y_ref.at[pl.ds(row0, MOE_SUB)], osem.at[sub % 2])

        for sub, rows in enumerate(subs):
            @pl.when(sub < n)
            def _():
                if sub >= 2:
                    out_copy(sub - 2).wait()
                st_sc[sub % 2] = _pack_bf16_pairs(acc_sc[rows, :])
                out_copy(sub).start()

        for sub in range(MOE_NSUB):
            @pl.when((sub < n) & (sub + 2 >= n))
            def _():
                out_copy(sub).wait()


def _moe_call(inv, pair_tables, h_packed, w1, w3, w2):
    n_f = D_FF_EXPERT // MOE_TF

    def f_eff(p, f, pn):
        return jnp.where(pn[p] > 0, f, n_f - 1)

    w13 = pl.BlockSpec((1, 1, D_MODEL, MOE_TF), lambda p, f, inv, ps, pn, pd: (0, pd[p], 0, f_eff(p, f, pn)))
    w2s = pl.BlockSpec((1, 1, MOE_TF, D_MODEL), lambda p, f, inv, ps, pn, pd: (0, pd[p], f_eff(p, f, pn), 0))
    hbm = pl.BlockSpec(memory_space=pl.ANY)
    win = MOE_NSUB * MOE_SUB
    n_prefetch = 4
    return pl.pallas_call(
        _moe_kernel,
        grid_spec=pltpu.PrefetchScalarGridSpec(
            num_scalar_prefetch=n_prefetch,
            grid=(MOE_PAIRS, n_f),
            in_specs=[hbm, w13, w13, w2s, hbm],
            out_specs=hbm,
            scratch_shapes=[pltpu.VMEM((win, D_MODEL // 2), U32),
                            pltpu.VMEM((win, D_MODEL), F32),
                            pltpu.VMEM((2, MOE_SUB, D_MODEL // 2), U32),
                            pltpu.SemaphoreType.DMA(()),
                            pltpu.SemaphoreType.DMA((2,))],
        ),
        out_shape=jax.ShapeDtypeStruct((MOE_ROWS, D_MODEL // 2), U32),
        input_output_aliases={n_prefetch + 4: 0},
        compiler_params=_cparams("arbitrary", "arbitrary"),
        name="moe_experts",
    )(inv, *pair_tables, h_packed, w1, w3, w2, jnp.zeros((MOE_ROWS, D_MODEL // 2), U32))


def _pair_schedule(counts):
    nsub = (counts + MOE_SUB - 1) // MOE_SUB
    seg_start = jnp.cumsum(nsub) - nsub
    nchunk = (nsub + MOE_NSUB - 1) // MOE_NSUB
    chunk_end = jnp.cumsum(nchunk)
    idx = jnp.arange(MOE_PAIRS, dtype=jnp.int32)
    expert = jnp.sum(chunk_end[None, :] <= idx[:, None], axis=1).astype(jnp.int32)
    active = expert < N_EXPERTS
    onehot = expert[:, None] == jnp.arange(N_EXPERTS, dtype=jnp.int32)[None, :]
    pick = lambda a: jnp.sum(jnp.where(onehot, a[None, :], 0), axis=1)
    c = idx - (pick(chunk_end) - pick(nchunk))
    first = jnp.where(active, pick(seg_start) + MOE_NSUB * c, 0)
    count = jnp.where(active, jnp.minimum(MOE_NSUB, pick(nsub) - MOE_NSUB * c), 0)
    dma_expert = jnp.where(active, expert, jnp.max(jnp.where(active, expert, 0)))
    i32 = lambda a: a.astype(jnp.int32)
    return seg_start * MOE_SUB, tuple(i32(a) for a in (first, count, dma_expert))


def _combine_kernel(pos_ref, x_ref, meta_ref, gf_ref, ys_ref, o_ref, buf, sem):
    base = pl.program_id(0) * COMB_TM

    def copy(r, k, src_row):
        return pltpu.make_async_copy(ys_ref.at[pl.ds(src_row, 1)], buf.at[k, pl.ds(r, 1)], sem)

    def start(r, carry):
        for k in range(2):
            copy(r, k, pos_ref[2 * (base + r) + k]).start()
        return carry

    lax.fori_loop(0, COMB_TM, start, 0, unroll=8)

    def wait(r, carry):
        for k in range(2):
            copy(r, k, 0).wait()
        return carry

    lax.fori_loop(0, COMB_TM, wait, 0, unroll=8)

    half = D_MODEL // 2
    lo0, hi0 = _unpack_bf16_pairs(buf[0])
    lo1, hi1 = _unpack_bf16_pairs(buf[1])
    g1 = meta_ref[:, 2:3]
    g2 = meta_ref[:, 3:4]
    gf = gf_ref[0]
    o_ref[:, :half] = x_ref[:, :half] + gf[:, :half] * (g1 * lo0 + g2 * lo1)
    o_ref[:, half:] = x_ref[:, half:] + gf[:, half:] * (g1 * hi0 + g2 * hi1)


def _combine_call(pos, x, meta, gf, ys):
    rows = x.shape[0]
    tiles_per_batch = SEQ // COMB_TM
    return pl.pallas_call(
        _combine_kernel,
        grid_spec=pltpu.PrefetchScalarGridSpec(
            num_scalar_prefetch=1,
            grid=(rows // COMB_TM,),
            in_specs=[
                pl.BlockSpec((COMB_TM, D_MODEL), lambda i, p: (i, 0)),
                pl.BlockSpec((COMB_TM, LANES), lambda i, p: (i, 0)),
                pl.BlockSpec((1, 1, D_MODEL), lambda i, p: (i // tiles_per_batch, 0, 0)),
                pl.BlockSpec(memory_space=pl.ANY),
            ],
            out_specs=pl.BlockSpec((COMB_TM, D_MODEL), lambda i, p: (i, 0)),
            scratch_shapes=[pltpu.VMEM((2, COMB_TM, D_MODEL // 2), U32), pltpu.SemaphoreType.DMA(())],
        ),
        out_shape=jax.ShapeDtypeStruct((rows, D_MODEL), F32),
        compiler_params=_cparams("arbitrary"),
        name="moe_combine",
    )(pos, x, meta, gf, ys)


def _moe_layer(x, h_packed, logits, gf, w1, w3, w2):
    meta, cnt = _route_call(logits)
    expert = meta[:, 0:2].astype(jnp.int32)
    rank = meta[:, 4:6].astype(jnp.int32)
    counts = cnt[0, :N_EXPERTS].astype(jnp.int32)
    seg_start, pair_tables = _pair_schedule(counts)
    ids = jnp.arange(N_EXPERTS, dtype=jnp.int32)
    pos = rank + jnp.sum(jnp.where(expert[..., None] == ids, seg_start, 0), axis=-1)
    pos = pos.reshape(-1).astype(jnp.int32)
    ys = _moe_call(_invert_call(pos), pair_tables, h_packed, w1, w3, w2)
    return _combine_call(pos, x, meta, gf, ys)


def _rope_tables(dh):
    rows = SEQ // GRID_W
    row = jnp.repeat(jnp.arange(rows, dtype=F32), GRID_W)
    col = jnp.tile(jnp.arange(GRID_W, dtype=F32), rows)
    half = dh // 2
    inv = ROPE_THETA ** (-jnp.arange(0, half, 2, dtype=F32) / half)
    ar = row[:, None] * inv[None, :]
    ac = col[:, None] * inv[None, :]
    ang = jnp.concatenate([ar, ar, ac, ac], axis=-1)
    ang = jnp.concatenate([jnp.zeros((CTX_LEN, dh), F32), ang], axis=0)
    ang = jnp.tile(ang, (1, LANES // dh))
    odd = ((jnp.arange(LANES) // (dh // 4)) % 2 == 1)[None, :]
    sin = jnp.sin(ang)
    return jnp.cos(ang), jnp.where(odd, sin, 0.0), jnp.where(odd, 0.0, -sin)


def kernel(x, c, ctx, c_ctx, ada_w, ada_b, norm_mix_g, norm_ffn_g, w_in, w_out, a_q_norm, a_k_norm,
           a_lam_q1, a_lam_k1, a_lam_q2, a_lam_k2, a_subln_g, b_conv_w, b_out_g, c_q_norm, c_k_norm,
           c_sink, c_out_g, ffn_w1, ffn_w3, ffn_w2, router_w, moe_w1, moe_w3, moe_w2):
    assert x.shape == (BATCH, SEQ, D_MODEL) and ctx.shape == (BATCH, CTX_LEN, D_MODEL)
    cond = jnp.concatenate([c, c_ctx[None, :], jnp.zeros((8 - BATCH - 1, D_MODEL), F32)], axis=0)
    mod = _ada_call(cond, ada_w, ada_b)
    tables = _rope_tables(A_DK) + _rope_tables(C_DH)
    xs = (ctx, x)
    y_prev = gf_prev = None
    out = None
    moe_bf16 = []
    assert DEPTH == 2 and moe_w1.shape[0] == 1
    for l in range(DEPTH):
        last = l == DEPTH - 1
        sh_a, sc_a, g_a, sh_f, sc_f, g_f = [
            mod[l, :BATCH + 1, k * D_MODEL:(k + 1) * D_MODEL].reshape(BATCH + 1, 1, D_MODEL) for k in range(6)]
        gains = (jnp.tile(a_q_norm[l], LANES // A_DK)[None, :], jnp.tile(a_k_norm[l], LANES // A_DK)[None, :],
                 c_q_norm[l][None, :], c_k_norm[l][None, :])
        ffn_cast = [] if last else [ffn_w1[l // 2], ffn_w3[l // 2], ffn_w2[l // 2]]
        res = _in_call(xs, y_prev, gf_prev, norm_mix_g[l][None, :], sc_a, sh_a, w_in[l].astype(BF16),
                       gains, tables, ffn_cast)
        if y_prev is not None:
            xs, res = res[0], res[1:]
        qa, ka, va, bb, u, qc, kc, vc = res[:8]
        ffn_bf16 = res[8:]
        lam_init = 0.8 - 0.6 * math.exp(-0.3 * l)
        lam_vecs = [v[l][None, :] for v in (a_lam_q1, a_lam_k1, a_lam_q2, a_lam_k2)]
        cast_jobs = ((moe_w1[0], moe_w2[0]), (moe_w3[0],))[l]
        oa, cast_out = _attn_a_call(qa, ka, va, lam_vecs, a_subln_g[l][None, :], lam_init, with_ctx=not last,
                                    cast=[w.reshape(-1, w.shape[-1]) for w in cast_jobs])
        moe_bf16 += [c.reshape((1,) + w.shape) for c, w in zip(cast_out, cast_jobs)]
        oc = _attn_c_call(c_sink[l], qc, kc, vc, with_ctx=not last)
        wr = None
        if last:
            wr = jnp.pad(router_w[l // 2], ((0, 0), (0, LANES - N_EXPERTS)))
        res = _out_call(xs, oa, bb, u, oc, w_out[l].astype(BF16), g_a, b_out_g[l][None, :],
                        c_out_g[l][None, :], b_conv_w[l], norm_ffn_g[l][None, :], sc_f, sh_f, wr)
        if not last:
            xs, h2 = res
            y = _ffn_call(h2.reshape(BATCH * N_TOK, D_MODEL), *ffn_bf16)
            y_prev, gf_prev = y.reshape(BATCH, N_TOK, D_MODEL), g_f
        else:
            x_lat, h_packed, logits = res
            w1b, w2b, w3b = moe_bf16
            out = _moe_layer(x_lat.reshape(BATCH * SEQ, D_MODEL), h_packed.reshape(BATCH * SEQ, D_MODEL // 2),
                             logits.reshape(BATCH * SEQ, LANES), g_f[:BATCH], w1b, w3b, w2b)
    return out.reshape(BATCH, SEQ, D_MODEL)
```

```python
import functools
import math

import jax
import jax.numpy as jnp
from jax import lax
from jax.experimental import pallas as pl
from jax.experimental.pallas import tpu as pltpu

F32 = jnp.float32
BF16 = jnp.bfloat16
U32 = jnp.uint32

D_MODEL = 2048
BATCH = 2
SEQ = 4096
DEPTH = 2
GRID_W = 64
CTX_LEN = 256
EPS = 1e-6
ROPE_THETA = 10000.0

A_HEADS = 4
A_DK = 64
A_DV = 128
A_WIDTH = A_HEADS * A_DV
B_WIDTH = 512
C_HEADS = 8
C_KV_HEADS = 2
C_GROUP = C_HEADS // C_KV_HEADS
C_DH = 128
C_WIDTH = C_HEADS * C_DH
C_KV_COLS = C_KV_HEADS * C_DH
WINDOW = 128
MIX_WIDTH = A_WIDTH + B_WIDTH + C_WIDTH
IN_COLS = 3 * A_WIDTH + 3 * B_WIDTH + C_WIDTH + 2 * C_KV_COLS
D_FF = 5632
N_EXPERTS = 8
D_FF_EXPERT = 7168

COL_AQ, COL_AK, COL_AV = 0, 512, 1024
COL_BB, COL_BC, COL_BX = 1536, 2048, 2560
COL_CQ, COL_CK, COL_CV = 3072, 4096, 4352

LANES = 128
N_TOK = CTX_LEN + SEQ
TM = 256
N_TILES = N_TOK // TM
IN_TN = 512
CBLK = 128
N_CBLK = N_TOK // CBLK
A_CHUNK = 1024
A_SPLIT = 4
CAST_STEPS = 128
LOG2E = 1.4426950408889634
FFN_TM = N_TOK // 4
FFN_TF = 512
IN_CAST_STEPS = 32
FFN_SPLIT = 4
MOE_SUB = 256
MOE_NSUB = 10
MOE_FAST = 8
MOE_TF = 512
MOE_ROWS = 2 * BATCH * SEQ + N_EXPERTS * MOE_SUB
MOE_PAIRS = (MOE_ROWS // MOE_SUB + N_EXPERTS * (MOE_NSUB - 1)) // MOE_NSUB
ROUTE_TM = 512
COMB_TM = 256
VMEM_LIMIT = 56 * 1024 * 1024
NEG = -1e30

NT_DIMS = (((1,), (1,)), ((), ()))


def _cparams(*sem):
    return pltpu.CompilerParams(dimension_semantics=sem, vmem_limit_bytes=VMEM_LIMIT)


def _dot(a, b):
    return jnp.dot(a, b, preferred_element_type=F32)


def _rms(x, width):
    return x * lax.rsqrt(jnp.sum(x * x, axis=-1, keepdims=True) * (1.0 / width) + EPS)


def _slab_spec(a, n_steps, step_of, layer=None):
    rows = a.shape[-2] // n_steps
    assert rows * n_steps == a.shape[-2] and rows % 16 == 0
    step = lambda *g: jnp.minimum(step_of(*g), n_steps - 1)
    if layer is None:
        return pl.BlockSpec((rows, a.shape[-1]), lambda *g: (step(*g), 0))
    return pl.BlockSpec((1, rows, a.shape[-1]), lambda *g: (layer, step(*g), 0))


def _cast_slabs(step, n_steps, srcs, dsts):
    @pl.when(step < n_steps)
    def _():
        for src, dst in zip(srcs, dsts):
            dst[...] = (src[0] if len(src.shape) == 3 else src[...]).astype(BF16)


ADA_TN = 1024


def _ada_kernel(c_ref, w_ref, b_ref, o_ref):
    a = c_ref[...]
    a = a / (1.0 + jnp.exp(-a))
    o_ref[0] = _dot(a.astype(BF16), w_ref[0].astype(BF16)) + b_ref[0]


def _ada_call(cond, ada_w, ada_b):
    n = 6 * D_MODEL
    return pl.pallas_call(
        _ada_kernel,
        grid=(DEPTH, n // ADA_TN),
        in_specs=[
            pl.BlockSpec((8, D_MODEL), lambda l, j: (0, 0)),
            pl.BlockSpec((1, D_MODEL, ADA_TN), lambda l, j: (l, 0, j)),
            pl.BlockSpec((1, 1, ADA_TN), lambda l, j: (l, 0, j)),
        ],
        out_specs=pl.BlockSpec((1, 8, ADA_TN), lambda l, j: (l, 0, j)),
        out_shape=jax.ShapeDtypeStruct((DEPTH, 8, n), F32),
        compiler_params=_cparams("arbitrary", "arbitrary"),
        name="ada_mod",
    )(cond, ada_w, ada_b.reshape(DEPTH, 1, n))


def _rope128(n, cos, s_odd, s_even, quarter):
    return n * cos + pltpu.roll(n, quarter, 1) * s_odd + pltpu.roll(n, LANES - quarter, 1) * s_even


def _in_kernel(*refs, with_y, n_cast):
    if with_y:
        x_ref, y_ref, gf_ref = refs[:3]
    else:
        ctx_ref, x_ref = refs[:2]
    refs = refs[3 if with_y else 2:]
    (g_ref, sc_ref, sh_ref, w_ref, aqn_ref, akn_ref, cqn_ref, ckn_ref,
     ca_ref, sa1_ref, sa2_ref, cc_ref, sc1_ref, sc2_ref) = refs[:14]
    cast_src = refs[14:14 + n_cast]
    outs = refs[14 + n_cast:]
    if with_y:
        xo_ref, outs = outs[0], outs[1:]
    qa_ref, ka_ref, va_ref, bb_ref, u_ref, qc_ref, kc_ref, vc_ref = outs[:8]
    _cast_slabs(pl.program_id(0) * pl.num_programs(1) + pl.program_id(1), IN_CAST_STEPS, cast_src, outs[8:])

    if with_y:
        x = x_ref[0] + gf_ref[0] * y_ref[0].astype(F32)
        xo_ref[0] = x
    else:
        x = jnp.where(pl.program_id(1) == 0, ctx_ref[0], x_ref[0])
    h = _rms(x, D_MODEL) * g_ref[...]
    h = h * (1.0 + sc_ref[0]) + sh_ref[0]
    hb = h.astype(BF16)

    lane = lax.broadcasted_iota(jnp.int32, (TM, LANES), 1)
    lo_half = lane < A_DK

    def proj(col0, width):
        return _dot(hb, w_ref[:, col0:col0 + width])

    def head_a(p, gain_ref, scale):
        p2 = p * p
        s_all = jnp.sum(p2, axis=-1, keepdims=True)
        s_lo = jnp.sum(jnp.where(lo_half, p2, 0.0), axis=-1, keepdims=True)
        ms = jnp.where(lo_half, s_lo, s_all - s_lo) * (1.0 / A_DK)
        n = p * lax.rsqrt(ms + EPS) * gain_ref[...]
        r = _rope128(n, ca_ref[...], sa1_ref[...], sa2_ref[...], A_DK // 4)
        return (r * scale).astype(BF16)

    def head_c(p, gain_ref, scale):
        n = _rms(p, C_DH) * gain_ref[...]
        r = _rope128(n, cc_ref[...], sc1_ref[...], sc2_ref[...], C_DH // 4)
        return (r * scale).astype(BF16)

    def heads(o_ref, col0, width, fn, gain_ref, scale):
        for c in range(0, width, IN_TN):
            wd = min(IN_TN, width - c)
            p = proj(col0 + c, wd)
            for j in range(wd // LANES):
                o_ref[0, :, c + j * LANES:c + (j + 1) * LANES] = fn(
                    p[:, j * LANES:(j + 1) * LANES], gain_ref, scale)

    heads(qa_ref, COL_AQ, A_WIDTH, head_a, aqn_ref, A_DK ** -0.5 * LOG2E)
    heads(ka_ref, COL_AK, A_WIDTH, head_a, akn_ref, 1.0)
    va_ref[0] = proj(COL_AV, A_WIDTH).astype(BF16)
    bb_ref[0] = proj(COL_BB, B_WIDTH).astype(BF16)
    u_ref[0] = (proj(COL_BC, B_WIDTH) * proj(COL_BX, B_WIDTH)).astype(BF16)
    heads(qc_ref, COL_CQ, C_WIDTH, head_c, cqn_ref, C_DH ** -0.5 * LOG2E)
    heads(kc_ref, COL_CK, C_KV_COLS, head_c, ckn_ref, 1.0)
    vc_ref[0] = proj(COL_CV, C_KV_COLS).astype(BF16)


def _mod_row(b, t):
    return jnp.where(t == 0, BATCH, b)


def _split_specs():
    return [pl.BlockSpec((1, TM, D_MODEL), lambda b, t: (b, 0, 0)),
            pl.BlockSpec((1, TM, D_MODEL), lambda b, t: (b, jnp.maximum(t - 1, 0), 0))]


def _in_call(x, y, gf, g, sc, sh, w, gains, tables, cast):
    with_y = y is not None
    tile = lambda width: pl.BlockSpec((1, TM, width), lambda b, t: (b, t, 0))
    modv = pl.BlockSpec((1, 1, D_MODEL), lambda b, t: (_mod_row(b, t), 0, 0))
    full2 = lambda a: pl.BlockSpec(a.shape, lambda b, t: (0, 0))
    rows = pl.BlockSpec((TM, LANES), lambda b, t: (t, 0))
    slab = lambda a, layer=None: _slab_spec(a, IN_CAST_STEPS, lambda b, t: b * N_TILES + t, layer)
    if with_y:
        in_specs = [tile(D_MODEL), tile(D_MODEL), modv]
        args = [x, y, gf]
    else:
        in_specs = _split_specs()
        args = list(x)
    in_specs += [full2(g), modv, modv,
                 pl.BlockSpec(w.shape, lambda b, t: (0, 0), pipeline_mode=pl.Buffered(1))]
    args += [g, sc, sh, w]
    in_specs += [full2(a) for a in gains]
    args += list(gains)
    in_specs += [rows] * 6
    args += list(tables)
    in_specs += [slab(a, layer) for a, layer in cast]
    args += [a for a, _ in cast]
    widths = (A_WIDTH, A_WIDTH, A_WIDTH, B_WIDTH, B_WIDTH, C_WIDTH, C_KV_COLS, C_KV_COLS)
    cast_out = [jax.ShapeDtypeStruct(a.shape[-2:], BF16) for a, _ in cast]
    out_specs = [tile(wd) for wd in widths] + [slab(o) for o in cast_out]
    out_shape = [jax.ShapeDtypeStruct((BATCH, N_TOK, wd), BF16) for wd in widths] + cast_out
    if with_y:
        out_specs = [tile(D_MODEL)] + out_specs
        out_shape = [jax.ShapeDtypeStruct((BATCH, N_TOK, D_MODEL), F32)] + out_shape
    assert BATCH * N_TILES >= IN_CAST_STEPS
    return pl.pallas_call(
        functools.partial(_in_kernel, with_y=with_y, n_cast=len(cast)),
        grid=(BATCH, N_TILES),
        in_specs=in_specs,
        out_specs=out_specs,
        out_shape=out_shape,
        compiler_params=_cparams("arbitrary", "arbitrary"),
        name="in_proj",
    )(*args)


def _attn_a_kernel(*refs, lam_init, t_off, n_cast):
    q_ref, k_ref, v_ref, lq1_ref, lk1_ref, lq2_ref, lk2_ref, g_ref = refs[:8]
    cast_src = refs[8:8 + n_cast]
    o_ref = refs[8 + n_cast]
    cast_dst = refs[9 + n_cast:9 + 2 * n_cast]
    m_sc, acc_sc = refs[9 + 2 * n_cast:]
    t = pl.program_id(2) + t_off

    step_idx = (pl.program_id(0) * pl.num_programs(1) + pl.program_id(1)) * pl.num_programs(2) + pl.program_id(2)
    _cast_slabs(step_idx, CAST_STEPS, cast_src, cast_dst)

    q = q_ref[0]
    lane = lax.broadcasted_iota(jnp.int32, (TM, LANES), 1)
    zero = jnp.zeros_like(q)
    qs = jnp.concatenate([jnp.where(lane < A_DK, q, zero), jnp.where(lane >= A_DK, q, zero)], axis=0)

    m_sc[...] = jnp.full(m_sc.shape, NEG, F32)
    acc_sc[...] = jnp.zeros(acc_sc.shape, F32)

    def step(k, v):
        v1 = jnp.concatenate([v, jnp.ones_like(v)], axis=1)
        grp = 2 * TM // A_SPLIT
        for g in range(A_SPLIT):
            rows = slice(g * grp, (g + 1) * grp)
            s = lax.dot_general(qs[rows], k, NT_DIMS, preferred_element_type=F32)
            m_prev = m_sc[rows, :]
            m_new = jnp.maximum(m_prev, jnp.max(s, axis=-1, keepdims=True))
            alpha = jnp.exp2(m_prev - m_new)
            p = jnp.exp2(s - jnp.tile(m_new, (1, s.shape[1] // LANES)))
            acc_sc[rows, :] = jnp.tile(alpha, (1, 2)) * acc_sc[rows, :] + _dot(p.astype(BF16), v1)
            m_sc[rows, :] = m_new

    def keys(bounds):
        for lo, hi in zip(bounds[:-1], bounds[1:]):
            step(k_ref[0, lo:hi, :], v_ref[0, lo:hi, :])

    all_keys = functools.partial(keys, [0] + list(range(CTX_LEN + A_CHUNK, N_TOK + 1, A_CHUNK)))
    if t_off == 0:
        pl.when(t == 0)(functools.partial(keys, [0, CTX_LEN]))
        pl.when(t > 0)(all_keys)
    else:
        all_keys()

    lam = (jnp.exp(jnp.sum(lq1_ref[...] * lk1_ref[...], keepdims=True))
           - jnp.exp(jnp.sum(lq2_ref[...] * lk2_ref[...], keepdims=True)) + lam_init)
    acc = acc_sc[...]
    o = acc[:, :A_DV] / acc[:, A_DV:]
    o = o[:TM] - lam * o[TM:]
    o_ref[0] = (_rms(o, A_DV) * g_ref[...] * (1.0 - lam_init)).astype(BF16)


def _attn_a_call(qa, ka, va, lam_vecs, subln_g, lam_init, with_ctx, cast):
    t_off = 0 if with_ctx else 1
    nt = N_TILES - t_off
    assert BATCH * A_HEADS * nt >= CAST_STEPS
    vec = pl.BlockSpec((1, A_DK), lambda b, h, t: (0, 0))

    def slab(a):
        return _slab_spec(a, CAST_STEPS, lambda b, h, t: (b * A_HEADS + h) * nt + t)

    res = pl.pallas_call(
        functools.partial(_attn_a_kernel, lam_init=lam_init, t_off=t_off, n_cast=len(cast)),
        grid=(BATCH, A_HEADS, nt),
        in_specs=[
            pl.BlockSpec((1, TM, LANES), lambda b, h, t: (b, t + t_off, h)),
            pl.BlockSpec((1, N_TOK, LANES), lambda b, h, t: (b, 0, h)),
            pl.BlockSpec((1, N_TOK, LANES), lambda b, h, t: (b, 0, h)),
            vec, vec, vec, vec,
            pl.BlockSpec((1, A_DV), lambda b, h, t: (0, 0)),
        ] + [slab(a) for a in cast],
        out_specs=[pl.BlockSpec((1, TM, LANES), lambda b, h, t: (b, t, h))] + [slab(a) for a in cast],
        out_shape=[jax.ShapeDtypeStruct((BATCH, nt * TM, A_WIDTH), BF16)]
        + [jax.ShapeDtypeStruct(a.shape, BF16) for a in cast],
        scratch_shapes=[pltpu.VMEM((2 * TM, LANES), F32), pltpu.VMEM((2 * TM, 2 * A_DV), F32)],
        compiler_params=_cparams("arbitrary", "arbitrary", "arbitrary"),
        name="attn_diff",
    )(qa, ka, va, *lam_vecs, subln_g, *cast)
    return res[0], res[1:]


def _attn_c_kernel(sink_ref, q_ref, kp_ref, k0_ref, kn_ref, kx_ref, vp_ref, v0_ref, vn_ref, vx_ref,
                   o_ref, *, t_off):
    t = pl.program_id(1) + t_off
    rows = C_GROUP * CBLK
    i = lax.broadcasted_iota(jnp.int32, (rows, CBLK), 0) & (CBLK - 1)
    j = lax.broadcasted_iota(jnp.int32, (rows, CBLK), 1)
    n = t - CTX_LEN // CBLK
    n_last = SEQ // CBLK - 1
    far = 4 * CBLK
    off_p = jnp.where(n >= 1, 0, far)
    off_0 = jnp.where(n >= 0, 0, far)
    off_n = jnp.where((n >= 0) & (n < n_last), 0, far)
    g_row = lax.broadcasted_iota(jnp.int32, (rows, 1), 0) // CBLK
    rmax = lambda s: jnp.max(s, axis=-1, keepdims=True)

    for kvh in range(C_KV_HEADS):
        col = slice(kvh * C_DH, (kvh + 1) * C_DH)
        q0 = kvh * C_GROUP * C_DH
        q4 = jnp.concatenate([q_ref[0, :, q0 + g * LANES:q0 + (g + 1) * LANES] for g in range(C_GROUP)], axis=0)
        k_loc = jnp.concatenate([kp_ref[0, :, col], k0_ref[0, :, col], kn_ref[0, :, col]], axis=0)
        s_loc = lax.dot_general(q4, k_loc, NT_DIMS, preferred_element_type=F32)
        s_p = jnp.where(j >= i + off_p, s_loc[:, :CBLK], NEG)
        s_0 = jnp.where(j >= off_0, s_loc[:, CBLK:2 * CBLK], NEG)
        s_n = jnp.where(j + off_n <= i, s_loc[:, 2 * CBLK:], NEG)
        s_x = lax.dot_general(q4, kx_ref[0, :, col], NT_DIMS, preferred_element_type=F32)

        sink = jnp.zeros((rows, 1), F32)
        for g in range(C_GROUP):
            sink = jnp.where(g_row == g, sink_ref[kvh * C_GROUP + g] * LOG2E, sink)

        m = jnp.maximum(jnp.maximum(jnp.maximum(rmax(s_p), rmax(s_0)), jnp.maximum(rmax(s_n), rmax(s_x))), sink)
        p = jnp.concatenate([jnp.exp2(s - m) for s in (s_p, s_0, s_n, s_x)], axis=1).astype(BF16)
        v = jnp.concatenate([vp_ref[0, :, col], v0_ref[0, :, col], vn_ref[0, :, col], vx_ref[0, :, col]], axis=0)
        ol = _dot(p, jnp.concatenate([v, jnp.ones_like(v)], axis=1))
        o = ol[:, :C_DH] / (ol[:, C_DH:] + jnp.exp2(sink - m))
        for g in range(C_GROUP):
            o_ref[0, :, q0 + g * LANES:q0 + (g + 1) * LANES] = o[g * CBLK:(g + 1) * CBLK].astype(BF16)


def _attn_c_call(sink, qc, kc, vc, with_ctx):
    t_off = 0 if with_ctx else CTX_LEN // CBLK
    nb = N_CBLK - t_off
    prev = pl.BlockSpec((1, CBLK, C_KV_COLS), lambda b, t, s: (b, jnp.maximum(t + t_off - 1, 0), 0))
    own = pl.BlockSpec((1, CBLK, C_KV_COLS), lambda b, t, s: (b, t + t_off, 0))
    nxt = pl.BlockSpec((1, CBLK, C_KV_COLS), lambda b, t, s: (b, jnp.minimum(t + t_off + 1, N_CBLK - 1), 0))
    ctx = pl.BlockSpec((1, CTX_LEN, C_KV_COLS), lambda b, t, s: (b, 0, 0))
    qi = pl.BlockSpec((1, CBLK, C_WIDTH), lambda b, t, s: (b, t + t_off, 0))
    qo = pl.BlockSpec((1, CBLK, C_WIDTH), lambda b, t, s: (b, t, 0))
    return pl.pallas_call(
        functools.partial(_attn_c_kernel, t_off=t_off),
        grid_spec=pltpu.PrefetchScalarGridSpec(
            num_scalar_prefetch=1,
            grid=(BATCH, nb),
            in_specs=[qi, prev, own, nxt, ctx, prev, own, nxt, ctx],
            out_specs=qo,
        ),
        out_shape=jax.ShapeDtypeStruct((BATCH, nb * CBLK, C_WIDTH), BF16),
        compiler_params=_cparams("arbitrary", "arbitrary"),
        name="attn_window",
    )(sink, qc, kc, kc, kc, kc, vc, vc, vc, vc)


HALO = 16


def _pack_bf16_pairs(h):
    hb = h.astype(BF16).astype(F32)
    half = h.shape[-1] // 2
    lo = lax.bitcast_convert_type(hb[:, :half], U32)
    hi = lax.bitcast_convert_type(hb[:, half:], U32)
    return (hi & jnp.uint32(0xFFFF0000)) | (lo >> 16)


def _unpack_bf16_pairs(p):
    lo = lax.bitcast_convert_type(p << 16, F32)
    hi = lax.bitcast_convert_type(p & jnp.uint32(0xFFFF0000), F32)
    return lo, hi


def _out_kernel(*refs, t_off, route):
    if route:
        x_ref = refs[0]
    else:
        ctx_ref, x_ref = refs[:2]
        refs = refs[1:]
    (oa_ref, bb_ref, u_ref, up_ref, un_ref, oc_ref, w_ref, ga_ref, bg_ref, cg_ref, cw_ref,
     gf_ref, scf_ref, shf_ref) = refs[1:15]
    refs = refs[15:]
    if route:
        wrh_ref, wrl_ref, xo_ref, h2_ref, lg_ref = refs
    else:
        xo_ref, h2_ref = refs
    t = pl.program_id(1) + t_off

    u = u_ref[0].astype(F32)
    row = lax.broadcasted_iota(jnp.int32, (TM, 1), 0)
    prev_ok = jnp.where(t >= 2, 1.0, 0.0)
    next_ok = jnp.where((t >= 1) & (t < N_TILES - 1), 1.0, 0.0)
    prev_row = up_ref[0, HALO - 1:HALO, :].astype(F32) * prev_ok
    next_row = un_ref[0, 0:1, :].astype(F32) * next_ok
    u_prev = jnp.where(row == 0, prev_row, pltpu.roll(u, 1, 0))
    u_next = jnp.where(row == TM - 1, next_row, pltpu.roll(u, TM - 1, 0))
    conv = u_prev * cw_ref[0:1, :] + u * cw_ref[1:2, :] + u_next * cw_ref[2:3, :]
    yb = _rms(bb_ref[0].astype(F32) * conv, B_WIDTH) * bg_ref[...]
    yc = _rms(oc_ref[0].astype(F32), C_WIDTH) * cg_ref[...]
    y = (_dot(oa_ref[0], w_ref[0:A_WIDTH, :])
         + _dot(yb.astype(BF16), w_ref[A_WIDTH:A_WIDTH + B_WIDTH, :])
         + _dot(yc.astype(BF16), w_ref[A_WIDTH + B_WIDTH:, :]))
    x_in = x_ref[0] if route else jnp.where(t == 0, ctx_ref[0], x_ref[0])
    x = x_in + ga_ref[0] * y
    xo_ref[0] = x
    h2 = _rms(x, D_MODEL) * gf_ref[...]
    h2 = h2 * (1.0 + scf_ref[0]) + shf_ref[0]
    if route:
        h2_ref[0] = _pack_bf16_pairs(h2)
        h_hi = h2.astype(BF16)
        h_lo = (h2 - h_hi.astype(F32)).astype(BF16)
        lg_ref[0] = _dot(h_hi, wrh_ref[...]) + (_dot(h_lo, wrh_ref[...]) + _dot(h_hi, wrl_ref[...]))
    else:
        h2_ref[0] = h2.astype(BF16)


def _out_call(x, oa, bb, u, oc, w, ga, bg, cg, cw, gfn, scf, shf, wr):
    route = wr is not None
    t_off = 1 if route else 0
    nt = N_TILES - t_off
    hpt = TM // HALO
    tile = lambda width: pl.BlockSpec((1, TM, width), lambda b, t: (b, t + t_off, 0))
    otile = lambda width: pl.BlockSpec((1, TM, width), lambda b, t: (b, t, 0))
    modv = pl.BlockSpec((1, 1, D_MODEL), lambda b, t: (_mod_row(b, t + t_off), 0, 0))
    full2 = lambda a: pl.BlockSpec(a.shape, lambda b, t: (0, 0))
    halo_p = pl.BlockSpec((1, HALO, B_WIDTH), lambda b, t: (b, jnp.maximum((t + t_off) * hpt - 1, 0), 0))
    halo_n = pl.BlockSpec((1, HALO, B_WIDTH),
                          lambda b, t: (b, jnp.minimum((t + t_off + 1) * hpt, N_TOK // HALO - 1), 0))
    x_specs, x_args = ([tile(D_MODEL)], [x]) if route else (_split_specs(), list(x))
    in_specs = x_specs + [otile(A_WIDTH), tile(B_WIDTH), tile(B_WIDTH), halo_p, halo_n, otile(C_WIDTH),
                          pl.BlockSpec(w.shape, lambda b, t: (0, 0), pipeline_mode=pl.Buffered(1)),
                          modv, full2(bg), full2(cg), full2(cw), full2(gfn), modv, modv]
    args = x_args + [oa, bb, u, u, u, oc, w, ga, bg, cg, cw, gfn, scf, shf]
    rows_out = nt * TM
    if route:
        wr_hi = wr.astype(BF16)
        wr_lo = (wr - wr_hi.astype(F32)).astype(BF16)
        in_specs += [full2(wr_hi), full2(wr_lo)]
        args += [wr_hi, wr_lo]
        out_specs = [otile(D_MODEL), otile(D_MODEL // 2), otile(LANES)]
        out_shape = [jax.ShapeDtypeStruct((BATCH, rows_out, D_MODEL), F32),
                     jax.ShapeDtypeStruct((BATCH, rows_out, D_MODEL // 2), U32),
                     jax.ShapeDtypeStruct((BATCH, rows_out, LANES), F32)]
    else:
        out_specs = [otile(D_MODEL), otile(D_MODEL)]
        out_shape = [jax.ShapeDtypeStruct((BATCH, rows_out, D_MODEL), F32),
                     jax.ShapeDtypeStruct((BATCH, rows_out, D_MODEL), BF16)]
    return pl.pallas_call(
        functools.partial(_out_kernel, t_off=t_off, route=route),
        grid=(BATCH, nt),
        in_specs=in_specs,
        out_specs=out_specs,
        out_shape=out_shape,
        compiler_params=_cparams("arbitrary", "arbitrary"),
        name="mix_out",
    )(*args)


def _swiglu_accumulate(h_ref, acc_ref, row_blocks, w1, w3, w2, packed=False):
    for rows in row_blocks:
        if packed:
            x_lo, x_hi = _unpack_bf16_pairs(h_ref[rows, :])
            h = jnp.concatenate([x_lo.astype(BF16), x_hi.astype(BF16)], axis=1)
        else:
            h = h_ref[rows, :]
        a = _dot(h, w1)
        b = _dot(h, w3)
        z = a / (1.0 + jnp.exp(-a)) * b
        acc_ref[rows, :] += _dot(z.astype(BF16), w2)


def _ffn_kernel(h_ref, w1_ref, w3_ref, w2_ref, y_ref, acc_ref):
    f = pl.program_id(1)

    @pl.when(f == 0)
    def _():
        acc_ref[...] = jnp.zeros(acc_ref.shape, F32)

    sub = FFN_TM // FFN_SPLIT
    _swiglu_accumulate(h_ref, acc_ref, [slice(i * sub, (i + 1) * sub) for i in range(FFN_SPLIT)],
                       w1_ref[...], w3_ref[...], w2_ref[...])

    @pl.when(f == pl.num_programs(1) - 1)
    def _():
        y_ref[...] = acc_ref[...].astype(BF16)


def _ffn_call(h, w1, w3, w2):
    rows = h.shape[0]
    return pl.pallas_call(
        _ffn_kernel,
        grid=(rows // FFN_TM, D_FF // FFN_TF),
        in_specs=[
            pl.BlockSpec((FFN_TM, D_MODEL), lambda i, f: (i, 0)),
            pl.BlockSpec((D_MODEL, FFN_TF), lambda i, f: (0, f)),
            pl.BlockSpec((D_MODEL, FFN_TF), lambda i, f: (0, f)),
            pl.BlockSpec((FFN_TF, D_MODEL), lambda i, f: (f, 0)),
        ],
        out_specs=pl.BlockSpec((FFN_TM, D_MODEL), lambda i, f: (i, 0)),
        out_shape=jax.ShapeDtypeStruct((rows, D_MODEL), BF16),
        scratch_shapes=[pltpu.VMEM((FFN_TM, D_MODEL), F32)],
        compiler_params=_cparams("arbitrary", "arbitrary"),
        name="ffn_dense",
    )(h, w1, w3, w2)


def _route_kernel(lg_ref, meta_ref, cnt_ref, carry_sc):
    @pl.when(pl.program_id(0) == 0)
    def _():
        carry_sc[...] = jnp.zeros(carry_sc.shape, F32)

    lane = lax.broadcasted_iota(jnp.int32, (ROUTE_TM, LANES), 1)
    lane_f = lane.astype(F32)
    lg = jnp.where(lane < N_EXPERTS, lg_ref[...], -jnp.inf)
    m1 = jnp.max(lg, axis=-1, keepdims=True)
    i1 = jnp.min(jnp.where(lg == m1, lane_f, float(LANES)), axis=-1, keepdims=True)
    oh1 = lane_f == i1
    lg2 = jnp.where(oh1, -jnp.inf, lg)
    m2 = jnp.max(lg2, axis=-1, keepdims=True)
    i2 = jnp.min(jnp.where(lg2 == m2, lane_f, float(LANES)), axis=-1, keepdims=True)
    oh2 = lane_f == i2
    e = jnp.exp(m2 - m1)
    g1 = 1.0 / (1.0 + e)
    g2 = e / (1.0 + e)
    oh = jnp.where(oh1, 1.0, 0.0) + jnp.where(oh2, 1.0, 0.0)
    r = lax.broadcasted_iota(jnp.int32, (ROUTE_TM, ROUTE_TM), 0)
    c = lax.broadcasted_iota(jnp.int32, (ROUTE_TM, ROUTE_TM), 1)
    tri = jnp.where(c < r, 1.0, 0.0).astype(BF16)
    before = _dot(tri, oh.astype(BF16)) + carry_sc[...]
    rank1 = jnp.sum(jnp.where(oh1, before, 0.0), axis=-1, keepdims=True)
    rank2 = jnp.sum(jnp.where(oh2, before, 0.0), axis=-1, keepdims=True)
    meta = jnp.zeros((ROUTE_TM, LANES), F32)
    for k, val in enumerate((i1, i2, g1, g2, rank1, rank2)):
        meta = jnp.where(lane == k, val, meta)
    meta_ref[...] = meta
    carry_sc[...] += jnp.sum(oh, axis=0, keepdims=True)
    cnt_ref[...] = carry_sc[...]


def _route_call(logits):
    rows = logits.shape[0]
    return pl.pallas_call(
        _route_kernel,
        grid=(rows // ROUTE_TM,),
        in_specs=[pl.BlockSpec((ROUTE_TM, LANES), lambda i: (i, 0))],
        out_specs=[pl.BlockSpec((ROUTE_TM, LANES), lambda i: (i, 0)),
                   pl.BlockSpec((1, LANES), lambda i: (0, 0))],
        out_shape=[jax.ShapeDtypeStruct((rows, LANES), F32), jax.ShapeDtypeStruct((1, LANES), F32)],
        scratch_shapes=[pltpu.VMEM((1, LANES), F32)],
        compiler_params=_cparams("arbitrary"),
        name="moe_route",
    )(logits)


INV_TM = 512


def _invert_kernel(pos_ref, zeros_ref, inv_ref, sem):
    @pl.when(pl.program_id(0) == 0)
    def _():
        fill = pltpu.make_async_copy(zeros_ref, inv_ref, sem)
        fill.start()
        fill.wait()

    base = pl.program_id(0) * INV_TM

    def scatter(r, carry):
        inv_ref[pos_ref[base + r]] = (base + r) >> 1
        return carry
    lax.fori_loop(0, INV_TM, scatter, 0, unroll=8)


def _invert_call(pos):
    return pl.pallas_call(
        _invert_kernel,
        grid_spec=pltpu.PrefetchScalarGridSpec(
            num_scalar_prefetch=1,
            grid=(pos.shape[0] // INV_TM,),
            in_specs=[pl.BlockSpec(memory_space=pl.ANY)],
            out_specs=pl.BlockSpec(memory_space=pltpu.SMEM),
            scratch_shapes=[pltpu.SemaphoreType.DMA(())],
        ),
        out_shape=jax.ShapeDtypeStruct((MOE_ROWS,), jnp.int32),
        compiler_params=_cparams("arbitrary"),
        name="moe_invert",
    )(pos, jnp.zeros((MOE_ROWS,), jnp.int32))


def _moe_kernel(inv_ref, ps_ref, pn_ref, pd_ref, h_ref, w1_ref, w3_ref, w2_ref, y_ref,
                xg_sc, acc_sc, st_sc, gsem, osem):
    p = pl.program_id(0)
    f = pl.program_id(1)
    n_f = pl.num_programs(1)
    start = ps_ref[p]
    n = pn_ref[p]
    subs = [slice(i * MOE_SUB, (i + 1) * MOE_SUB) for i in range(MOE_NSUB)]

    def row_copy(src_row, r):
        return pltpu.make_async_copy(h_ref.at[pl.ds(src_row, 1)], xg_sc.at[pl.ds(r, 1)], gsem)

    @pl.when((p == 0) & (f == 0))
    def _():
        st_sc[0] = jnp.zeros(st_sc.shape[1:], U32)
        tail = [pltpu.make_async_copy(st_sc.at[0], y_ref.at[pl.ds(r0, MOE_SUB)], osem.at[0])
                for r0 in range(2 * BATCH * SEQ, MOE_ROWS, MOE_SUB)]
        for copy in tail:
            copy.start()
        for copy in tail:
            copy.wait()

    @pl.when(f == 0)
    def _():
        def per_sub(fn):
            def body(sub, carry):
                lax.fori_loop(0, MOE_SUB, functools.partial(fn, sub * MOE_SUB), 0, unroll=8)
                return carry
            lax.fori_loop(0, n, body, 0)

        def issue(r0, r, carry):
            row_copy(inv_ref[start * MOE_SUB + r0 + r], r0 + r).start()
            return carry

        def wait(r0, r, carry):
            row_copy(0, r0 + r).wait()
            return carry

        per_sub(issue)
        for sub, rows in enumerate(subs):
            @pl.when(sub < n)
            def _():
                acc_sc[rows, :] = jnp.zeros((MOE_SUB, D_MODEL), F32)
        per_sub(wait)

    def weights():
        return w1_ref[0, 0], w3_ref[0, 0], w2_ref[0, 0]

    def accumulate(row_blocks, w):
        _swiglu_accumulate(xg_sc, acc_sc, row_blocks, *w, packed=True)

    @pl.when(n >= MOE_FAST)
    def _():
        w = weights()
        accumulate(subs[:MOE_FAST], w)
        for sub in range(MOE_FAST, MOE_NSUB):
            @pl.when(sub < n)
            def _():
                accumulate([subs[sub]], w)

    @pl.when((n > 0) & (n < MOE_FAST))
    def _():
        w = weights()
        for sub in range(MOE_FAST):
            @pl.when(sub < n)
            def _():
                accumulate([subs[sub]], w)

    @pl.when(f == n_f - 1)
    def _():
        def out_copy(sub):
            row0 = pl.multiple_of((start + sub) * MOE_SUB, MOE_SUB)
            return pltpu.make_async_copy(st_sc.at[sub % 2], y_ref.at[pl.ds(row0, MOE_SUB)], osem.at[sub % 2])

        for sub, rows in enumerate(subs):
            @pl.when(sub < n)
            def _():
                if sub >= 2:
                    out_copy(sub - 2).wait()
                st_sc[sub % 2] = _pack_bf16_pairs(acc_sc[rows, :])
                out_copy(sub).start()

        for sub in range(MOE_NSUB):
            @pl.when((sub < n) & (sub + 2 >= n))
            def _():
                out_copy(sub).wait()


def _moe_call(inv, pair_tables, h_packed, w1, w3, w2):
    n_f = D_FF_EXPERT // MOE_TF

    def f_eff(p, f, pn):
        return jnp.where(pn[p] > 0, f, n_f - 1)

    w13 = pl.BlockSpec((1, 1, D_MODEL, MOE_TF), lambda p, f, inv, ps, pn, pd: (0, pd[p], 0, f_eff(p, f, pn)))
    w2s = pl.BlockSpec((1, 1, MOE_TF, D_MODEL), lambda p, f, inv, ps, pn, pd: (0, pd[p], f_eff(p, f, pn), 0))
    hbm = pl.BlockSpec(memory_space=pl.ANY)
    win = MOE_NSUB * MOE_SUB
    n_prefetch = 4
    return pl.pallas_call(
        _moe_kernel,
        grid_spec=pltpu.PrefetchScalarGridSpec(
            num_scalar_prefetch=n_prefetch,
            grid=(MOE_PAIRS, n_f),
            in_specs=[hbm, w13, w13, w2s],
            out_specs=hbm,
            scratch_shapes=[pltpu.VMEM((win, D_MODEL // 2), U32),
                            pltpu.VMEM((win, D_MODEL), F32),
                            pltpu.VMEM((2, MOE_SUB, D_MODEL // 2), U32),
                            pltpu.SemaphoreType.DMA(()),
                            pltpu.SemaphoreType.DMA((2,))],
        ),
        out_shape=jax.ShapeDtypeStruct((MOE_ROWS, D_MODEL // 2), U32),
        compiler_params=_cparams("arbitrary", "arbitrary"),
        name="moe_experts",
    )(inv, *pair_tables, h_packed, w1, w3, w2)


def _pair_schedule(counts):
    nsub = (counts + MOE_SUB - 1) // MOE_SUB
    seg_start = jnp.cumsum(nsub) - nsub
    nchunk = (nsub + MOE_NSUB - 1) // MOE_NSUB
    chunk_end = jnp.cumsum(nchunk)
    idx = jnp.arange(MOE_PAIRS, dtype=jnp.int32)
    expert = jnp.sum(chunk_end[None, :] <= idx[:, None], axis=1).astype(jnp.int32)
    active = expert < N_EXPERTS
    onehot = expert[:, None] == jnp.arange(N_EXPERTS, dtype=jnp.int32)[None, :]
    pick = lambda a: jnp.sum(jnp.where(onehot, a[None, :], 0), axis=1)
    c = idx - (pick(chunk_end) - pick(nchunk))
    first = jnp.where(active, pick(seg_start) + MOE_NSUB * c, 0)
    count = jnp.where(active, jnp.minimum(MOE_NSUB, pick(nsub) - MOE_NSUB * c), 0)
    dma_expert = jnp.where(active, expert, jnp.max(jnp.where(active, expert, 0)))
    i32 = lambda a: a.astype(jnp.int32)
    return seg_start * MOE_SUB, tuple(i32(a) for a in (first, count, dma_expert))


def _combine_kernel(pos_ref, x_ref, meta_ref, gf_ref, ys_ref, o_ref, buf, sem):
    base = pl.program_id(0) * COMB_TM

    def copy(r, k, src_row):
        return pltpu.make_async_copy(ys_ref.at[pl.ds(src_row, 1)], buf.at[k, pl.ds(r, 1)], sem)

    def start(r, carry):
        for k in range(2):
            copy(r, k, pos_ref[2 * (base + r) + k]).start()
        return carry

    lax.fori_loop(0, COMB_TM, start, 0, unroll=8)

    def wait(r, carry):
        for k in range(2):
            copy(r, k, 0).wait()
        return carry

    lax.fori_loop(0, COMB_TM, wait, 0, unroll=8)

    half = D_MODEL // 2
    lo0, hi0 = _unpack_bf16_pairs(buf[0])
    lo1, hi1 = _unpack_bf16_pairs(buf[1])
    g1 = meta_ref[:, 2:3]
    g2 = meta_ref[:, 3:4]
    gf = gf_ref[0]
    o_ref[:, :half] = x_ref[:, :half] + gf[:, :half] * (g1 * lo0 + g2 * lo1)
    o_ref[:, half:] = x_ref[:, half:] + gf[:, half:] * (g1 * hi0 + g2 * hi1)


def _combine_call(pos, x, meta, gf, ys):
    rows = x.shape[0]
    tiles_per_batch = SEQ // COMB_TM
    return pl.pallas_call(
        _combine_kernel,
        grid_spec=pltpu.PrefetchScalarGridSpec(
            num_scalar_prefetch=1,
            grid=(rows // COMB_TM,),
            in_specs=[
                pl.BlockSpec((COMB_TM, D_MODEL), lambda i, p: (i, 0)),
                pl.BlockSpec((COMB_TM, LANES), lambda i, p: (i, 0)),
                pl.BlockSpec((1, 1, D_MODEL), lambda i, p: (i // tiles_per_batch, 0, 0)),
                pl.BlockSpec(memory_space=pl.ANY),
            ],
            out_specs=pl.BlockSpec((COMB_TM, D_MODEL), lambda i, p: (i, 0)),
            scratch_shapes=[pltpu.VMEM((2, COMB_TM, D_MODEL // 2), U32), pltpu.SemaphoreType.DMA(())],
        ),
        out_shape=jax.ShapeDtypeStruct((rows, D_MODEL), F32),
        compiler_params=_cparams("arbitrary"),
        name="moe_combine",
    )(pos, x, meta, gf, ys)


def _moe_layer(x, h_packed, logits, gf, w1, w3, w2):
    meta, cnt = _route_call(logits)
    expert = meta[:, 0:2].astype(jnp.int32)
    rank = meta[:, 4:6].astype(jnp.int32)
    counts = cnt[0, :N_EXPERTS].astype(jnp.int32)
    seg_start, pair_tables = _pair_schedule(counts)
    ids = jnp.arange(N_EXPERTS, dtype=jnp.int32)
    pos = rank + jnp.sum(jnp.where(expert[..., None] == ids, seg_start, 0), axis=-1)
    pos = pos.reshape(-1).astype(jnp.int32)
    ys = _moe_call(_invert_call(pos), pair_tables, h_packed, w1, w3, w2)
    return _combine_call(pos, x, meta, gf, ys)


def _rope_tables(dh):
    rows = SEQ // GRID_W
    row = jnp.repeat(jnp.arange(rows, dtype=F32), GRID_W)
    col = jnp.tile(jnp.arange(GRID_W, dtype=F32), rows)
    half = dh // 2
    inv = ROPE_THETA ** (-jnp.arange(0, half, 2, dtype=F32) / half)
    ar = row[:, None] * inv[None, :]
    ac = col[:, None] * inv[None, :]
    ang = jnp.concatenate([ar, ar, ac, ac], axis=-1)
    ang = jnp.concatenate([jnp.zeros((CTX_LEN, dh), F32), ang], axis=0)
    ang = jnp.tile(ang, (1, LANES // dh))
    odd = ((jnp.arange(LANES) // (dh // 4)) % 2 == 1)[None, :]
    sin = jnp.sin(ang)
    return jnp.cos(ang), jnp.where(odd, sin, 0.0), jnp.where(odd, 0.0, -sin)


def kernel(x, c, ctx, c_ctx, ada_w, ada_b, norm_mix_g, norm_ffn_g, w_in, w_out, a_q_norm, a_k_norm,
           a_lam_q1, a_lam_k1, a_lam_q2, a_lam_k2, a_subln_g, b_conv_w, b_out_g, c_q_norm, c_k_norm,
           c_sink, c_out_g, ffn_w1, ffn_w3, ffn_w2, router_w, moe_w1, moe_w3, moe_w2):
    assert x.shape == (BATCH, SEQ, D_MODEL) and ctx.shape == (BATCH, CTX_LEN, D_MODEL)
    cond = jnp.concatenate([c, c_ctx[None, :], jnp.zeros((8 - BATCH - 1, D_MODEL), F32)], axis=0)
    mod = _ada_call(cond, ada_w, ada_b)
    tables = _rope_tables(A_DK) + _rope_tables(C_DH)
    xs = (ctx, x)
    y_prev = gf_prev = None
    out = None
    moe_bf16 = []
    assert DEPTH == 2 and moe_w1.shape[0] == 1
    for l in range(DEPTH):
        last = l == DEPTH - 1
        sh_a, sc_a, g_a, sh_f, sc_f, g_f = [
            mod[l, :BATCH + 1, k * D_MODEL:(k + 1) * D_MODEL].reshape(BATCH + 1, 1, D_MODEL) for k in range(6)]
        gains = (jnp.tile(a_q_norm[l], LANES // A_DK)[None, :], jnp.tile(a_k_norm[l], LANES // A_DK)[None, :],
                 c_q_norm[l][None, :], c_k_norm[l][None, :])
        side = [] if l else [(ffn_w1[0], None), (ffn_w3[0], None), (ffn_w2[0], None),
                             (w_in, 1), (w_out, 0), (w_out, 1)]
        res = _in_call(xs, y_prev, gf_prev, norm_mix_g[l][None, :], sc_a, sh_a,
                       w_in_bf16[1] if l else w_in[0].astype(BF16), gains, tables, side)
        if y_prev is not None:
            xs, res = res[0], res[1:]
        qa, ka, va, bb, u, qc, kc, vc = res[:8]
        if not l:
            ffn_bf16, w_in_bf16, w_out_bf16 = res[8:11], (None, res[11]), res[12:14]
        lam_init = 0.8 - 0.6 * math.exp(-0.3 * l)
        lam_vecs = [v[l][None, :] for v in (a_lam_q1, a_lam_k1, a_lam_q2, a_lam_k2)]
        cast_jobs = ((moe_w1[0], moe_w2[0]), (moe_w3[0],))[l]
        oa, cast_out = _attn_a_call(qa, ka, va, lam_vecs, a_subln_g[l][None, :], lam_init, with_ctx=not last,
                                    cast=[w.reshape(-1, w.shape[-1]) for w in cast_jobs])
        moe_bf16 += [c.reshape((1,) + w.shape) for c, w in zip(cast_out, cast_jobs)]
        oc = _attn_c_call(c_sink[l], qc, kc, vc, with_ctx=not last)
        wr = None
        if last:
            wr = jnp.pad(router_w[l // 2], ((0, 0), (0, LANES - N_EXPERTS)))
        res = _out_call(xs, oa, bb, u, oc, w_out_bf16[l], g_a, b_out_g[l][None, :],
                        c_out_g[l][None, :], b_conv_w[l], norm_ffn_g[l][None, :], sc_f, sh_f, wr)
        if not last:
            xs, h2 = res
            y = _ffn_call(h2.reshape(BATCH * N_TOK, D_MODEL), *ffn_bf16)
            y_prev, gf_prev = y.reshape(BATCH, N_TOK, D_MODEL), g_f
        else:
            x_lat, h_packed, logits = res
            w1b, w2b, w3b = moe_bf16
            out = _moe_layer(x_lat.reshape(BATCH * SEQ, D_MODEL), h_packed.reshape(BATCH * SEQ, D_MODEL // 2),
                             logits.reshape(BATCH * SEQ, LANES), g_f[:BATCH], w1b, w3b, w2b)
    return out.reshape(BATCH, SEQ, D_MODEL)
```

```python
import functools
import math

import jax
import jax.numpy as jnp
from jax import lax
from jax.experimental import pallas as pl
from jax.experimental.pallas import tpu as pltpu

F32 = jnp.float32
BF16 = jnp.bfloat16
U32 = jnp.uint32

D_MODEL = 2048
BATCH = 2
SEQ = 4096
DEPTH = 2
GRID_W = 64
CTX_LEN = 256
EPS = 1e-6
ROPE_THETA = 10000.0

A_HEADS = 4
A_DK = 64
A_DV = 128
A_WIDTH = A_HEADS * A_DV
B_WIDTH = 512
C_HEADS = 8
C_KV_HEADS = 2
C_GROUP = C_HEADS // C_KV_HEADS
C_DH = 128
C_WIDTH = C_HEADS * C_DH
C_KV_COLS = C_KV_HEADS * C_DH
WINDOW = 128
MIX_WIDTH = A_WIDTH + B_WIDTH + C_WIDTH
IN_COLS = 3 * A_WIDTH + 3 * B_WIDTH + C_WIDTH + 2 * C_KV_COLS
D_FF = 5632
N_EXPERTS = 8
D_FF_EXPERT = 7168

COL_AQ, COL_AK, COL_AV = 0, 512, 1024
COL_BB, COL_BC, COL_BX = 1536, 2048, 2560
COL_CQ, COL_CK, COL_CV = 3072, 4096, 4352

LANES = 128
N_TOK = CTX_LEN + SEQ
TM = 256
N_TILES = N_TOK // TM
IN_TN = 512
CBLK = 128
N_CBLK = N_TOK // CBLK
A_CHUNK = 1024
A_SPLIT = 4
CAST_STEPS = 128
LOG2E = 1.4426950408889634
FFN_TM = N_TOK // 4
FFN_TF = 512
IN_CAST_STEPS = 32
FFN_SPLIT = 4
MOE_SUB = 128
MOE_NSUB = 20
MOE_FAST = 16
MOE_FAST_ROWS = 256
MOE_TF = 512
MOE_ROWS = 2 * BATCH * SEQ + N_EXPERTS * MOE_SUB
MOE_PAIRS = (MOE_ROWS // MOE_SUB + N_EXPERTS * (MOE_NSUB - 1)) // MOE_NSUB
ROUTE_TM = 512
COMB_TM = 256
VMEM_LIMIT = 56 * 1024 * 1024
NEG = -1e30

NT_DIMS = (((1,), (1,)), ((), ()))


def _cparams(*sem):
    return pltpu.CompilerParams(dimension_semantics=sem, vmem_limit_bytes=VMEM_LIMIT)


def _dot(a, b):
    return jnp.dot(a, b, preferred_element_type=F32)


def _rms(x, width):
    return x * lax.rsqrt(jnp.sum(x * x, axis=-1, keepdims=True) * (1.0 / width) + EPS)


def _slab_spec(a, n_steps, step_of, layer=None):
    rows = a.shape[-2] // n_steps
    assert rows * n_steps == a.shape[-2] and rows % 16 == 0
    step = lambda *g: jnp.minimum(step_of(*g), n_steps - 1)
    if layer is None:
        return pl.BlockSpec((rows, a.shape[-1]), lambda *g: (step(*g), 0))
    return pl.BlockSpec((1, rows, a.shape[-1]), lambda *g: (layer, step(*g), 0))


def _cast_slabs(step, n_steps, srcs, dsts):
    @pl.when(step < n_steps)
    def _():
        for src, dst in zip(srcs, dsts):
            dst[...] = (src[0] if len(src.shape) == 3 else src[...]).astype(BF16)


ADA_TN = 1024


def _ada_kernel(c_ref, w_ref, b_ref, o_ref):
    a = c_ref[...]
    a = a / (1.0 + jnp.exp(-a))
    o_ref[0] = _dot(a.astype(BF16), w_ref[0].astype(BF16)) + b_ref[0]


def _ada_call(cond, ada_w, ada_b):
    n = 6 * D_MODEL
    return pl.pallas_call(
        _ada_kernel,
        grid=(DEPTH, n // ADA_TN),
        in_specs=[
            pl.BlockSpec((8, D_MODEL), lambda l, j: (0, 0)),
            pl.BlockSpec((1, D_MODEL, ADA_TN), lambda l, j: (l, 0, j)),
            pl.BlockSpec((1, 1, ADA_TN), lambda l, j: (l, 0, j)),
        ],
        out_specs=pl.BlockSpec((1, 8, ADA_TN), lambda l, j: (l, 0, j)),
        out_shape=jax.ShapeDtypeStruct((DEPTH, 8, n), F32),
        compiler_params=_cparams("arbitrary", "arbitrary"),
        name="ada_mod",
    )(cond, ada_w, ada_b.reshape(DEPTH, 1, n))


def _rope128(n, cos, s_odd, s_even, quarter):
    return n * cos + pltpu.roll(n, quarter, 1) * s_odd + pltpu.roll(n, LANES - quarter, 1) * s_even


def _in_kernel(*refs, with_y, n_cast):
    if with_y:
        x_ref, y_ref, gf_ref = refs[:3]
    else:
        ctx_ref, x_ref = refs[:2]
    refs = refs[3 if with_y else 2:]
    (g_ref, sc_ref, sh_ref, w_ref, aqn_ref, akn_ref, cqn_ref, ckn_ref,
     ca_ref, sa1_ref, sa2_ref, cc_ref, sc1_ref, sc2_ref) = refs[:14]
    cast_src = refs[14:14 + n_cast]
    outs = refs[14 + n_cast:]
    if with_y:
        xo_ref, outs = outs[0], outs[1:]
    qa_ref, ka_ref, va_ref, bb_ref, u_ref, qc_ref, kc_ref, vc_ref = outs[:8]
    _cast_slabs(pl.program_id(0) * pl.num_programs(1) + pl.program_id(1), IN_CAST_STEPS, cast_src, outs[8:])

    if with_y:
        x = x_ref[0] + gf_ref[0] * y_ref[0].astype(F32)
        xo_ref[0] = x
    else:
        x = jnp.where(pl.program_id(1) == 0, ctx_ref[0], x_ref[0])
    h = _rms(x, D_MODEL) * g_ref[...]
    h = h * (1.0 + sc_ref[0]) + sh_ref[0]
    hb = h.astype(BF16)

    lane = lax.broadcasted_iota(jnp.int32, (TM, LANES), 1)
    lo_half = lane < A_DK

    def proj(col0, width):
        return _dot(hb, w_ref[:, col0:col0 + width])

    def head_a(p, gain_ref, scale):
        p2 = p * p
        s_all = jnp.sum(p2, axis=-1, keepdims=True)
        s_lo = jnp.sum(jnp.where(lo_half, p2, 0.0), axis=-1, keepdims=True)
        ms = jnp.where(lo_half, s_lo, s_all - s_lo) * (1.0 / A_DK)
        n = p * lax.rsqrt(ms + EPS) * gain_ref[...]
        r = _rope128(n, ca_ref[...], sa1_ref[...], sa2_ref[...], A_DK // 4)
        return (r * scale).astype(BF16)

    def head_c(p, gain_ref, scale):
        n = _rms(p, C_DH) * gain_ref[...]
        r = _rope128(n, cc_ref[...], sc1_ref[...], sc2_ref[...], C_DH // 4)
        return (r * scale).astype(BF16)

    def heads(o_ref, col0, width, fn, gain_ref, scale):
        for c in range(0, width, IN_TN):
            wd = min(IN_TN, width - c)
            p = proj(col0 + c, wd)
            for j in range(wd // LANES):
                o_ref[0, :, c + j * LANES:c + (j + 1) * LANES] = fn(
                    p[:, j * LANES:(j + 1) * LANES], gain_ref, scale)

    heads(qa_ref, COL_AQ, A_WIDTH, head_a, aqn_ref, A_DK ** -0.5 * LOG2E)
    heads(ka_ref, COL_AK, A_WIDTH, head_a, akn_ref, 1.0)
    va_ref[0] = proj(COL_AV, A_WIDTH).astype(BF16)
    bb_ref[0] = proj(COL_BB, B_WIDTH).astype(BF16)
    u_ref[0] = (proj(COL_BC, B_WIDTH) * proj(COL_BX, B_WIDTH)).astype(BF16)
    heads(qc_ref, COL_CQ, C_WIDTH, head_c, cqn_ref, C_DH ** -0.5 * LOG2E)
    heads(kc_ref, COL_CK, C_KV_COLS, head_c, ckn_ref, 1.0)
    vc_ref[0] = proj(COL_CV, C_KV_COLS).astype(BF16)


def _mod_row(b, t):
    return jnp.where(t == 0, BATCH, b)


def _split_specs():
    return [pl.BlockSpec((1, TM, D_MODEL), lambda b, t: (b, 0, 0)),
            pl.BlockSpec((1, TM, D_MODEL), lambda b, t: (b, jnp.maximum(t - 1, 0), 0))]


def _in_call(x, y, gf, g, sc, sh, w, gains, tables, cast):
    with_y = y is not None
    tile = lambda width: pl.BlockSpec((1, TM, width), lambda b, t: (b, t, 0))
    modv = pl.BlockSpec((1, 1, D_MODEL), lambda b, t: (_mod_row(b, t), 0, 0))
    full2 = lambda a: pl.BlockSpec(a.shape, lambda b, t: (0, 0))
    rows = pl.BlockSpec((TM, LANES), lambda b, t: (t, 0))
    slab = lambda a, layer=None: _slab_spec(a, IN_CAST_STEPS, lambda b, t: b * N_TILES + t, layer)
    if with_y:
        in_specs = [tile(D_MODEL), tile(D_MODEL), modv]
        args = [x, y, gf]
    else:
        in_specs = _split_specs()
        args = list(x)
    in_specs += [full2(g), modv, modv,
                 pl.BlockSpec(w.shape, lambda b, t: (0, 0), pipeline_mode=pl.Buffered(1))]
    args += [g, sc, sh, w]
    in_specs += [full2(a) for a in gains]
    args += list(gains)
    in_specs += [rows] * 6
    args += list(tables)
    in_specs += [slab(a, layer) for a, layer in cast]
    args += [a for a, _ in cast]
    widths = (A_WIDTH, A_WIDTH, A_WIDTH, B_WIDTH, B_WIDTH, C_WIDTH, C_KV_COLS, C_KV_COLS)
    cast_out = [jax.ShapeDtypeStruct(a.shape[-2:], BF16) for a, _ in cast]
    out_specs = [tile(wd) for wd in widths] + [slab(o) for o in cast_out]
    out_shape = [jax.ShapeDtypeStruct((BATCH, N_TOK, wd), BF16) for wd in widths] + cast_out
    if with_y:
        out_specs = [tile(D_MODEL)] + out_specs
        out_shape = [jax.ShapeDtypeStruct((BATCH, N_TOK, D_MODEL), F32)] + out_shape
    assert BATCH * N_TILES >= IN_CAST_STEPS
    return pl.pallas_call(
        functools.partial(_in_kernel, with_y=with_y, n_cast=len(cast)),
        grid=(BATCH, N_TILES),
        in_specs=in_specs,
        out_specs=out_specs,
        out_shape=out_shape,
        compiler_params=_cparams("arbitrary", "arbitrary"),
        name="in_proj",
    )(*args)


def _attn_a_kernel(*refs, lam_init, t_off, n_cast):
    q_ref, k_ref, v_ref, lq1_ref, lk1_ref, lq2_ref, lk2_ref, g_ref = refs[:8]
    cast_src = refs[8:8 + n_cast]
    o_ref = refs[8 + n_cast]
    cast_dst = refs[9 + n_cast:9 + 2 * n_cast]
    m_sc, acc_sc = refs[9 + 2 * n_cast:]
    t = pl.program_id(2) + t_off

    step_idx = (pl.program_id(0) * pl.num_programs(1) + pl.program_id(1)) * pl.num_programs(2) + pl.program_id(2)
    _cast_slabs(step_idx, CAST_STEPS, cast_src, cast_dst)

    q = q_ref[0]
    lane = lax.broadcasted_iota(jnp.int32, (TM, LANES), 1)
    zero = jnp.zeros_like(q)
    qs = jnp.concatenate([jnp.where(lane < A_DK, q, zero), jnp.where(lane >= A_DK, q, zero)], axis=0)

    m_sc[...] = jnp.full(m_sc.shape, NEG, F32)
    acc_sc[...] = jnp.zeros(acc_sc.shape, F32)

    def step(k, v):
        v1 = jnp.concatenate([v, jnp.ones_like(v)], axis=1)
        grp = 2 * TM // A_SPLIT
        for g in range(A_SPLIT):
            rows = slice(g * grp, (g + 1) * grp)
            s = lax.dot_general(qs[rows], k, NT_DIMS, preferred_element_type=F32)
            m_prev = m_sc[rows, :]
            m_new = jnp.maximum(m_prev, jnp.max(s, axis=-1, keepdims=True))
            alpha = jnp.exp2(m_prev - m_new)
            p = jnp.exp2(s - jnp.tile(m_new, (1, s.shape[1] // LANES)))
            acc_sc[rows, :] = jnp.tile(alpha, (1, 2)) * acc_sc[rows, :] + _dot(p.astype(BF16), v1)
            m_sc[rows, :] = m_new

    def keys(bounds):
        for lo, hi in zip(bounds[:-1], bounds[1:]):
            step(k_ref[0, lo:hi, :], v_ref[0, lo:hi, :])

    all_keys = functools.partial(keys, [0] + list(range(CTX_LEN + A_CHUNK, N_TOK + 1, A_CHUNK)))
    if t_off == 0:
        pl.when(t == 0)(functools.partial(keys, [0, CTX_LEN]))
        pl.when(t > 0)(all_keys)
    else:
        all_keys()

    lam = (jnp.exp(jnp.sum(lq1_ref[...] * lk1_ref[...], keepdims=True))
           - jnp.exp(jnp.sum(lq2_ref[...] * lk2_ref[...], keepdims=True)) + lam_init)
    acc = acc_sc[...]
    o = acc[:, :A_DV] / acc[:, A_DV:]
    o = o[:TM] - lam * o[TM:]
    o_ref[0] = (_rms(o, A_DV) * g_ref[...] * (1.0 - lam_init)).astype(BF16)


def _attn_a_call(qa, ka, va, lam_vecs, subln_g, lam_init, with_ctx, cast):
    t_off = 0 if with_ctx else 1
    nt = N_TILES - t_off
    assert BATCH * A_HEADS * nt >= CAST_STEPS
    vec = pl.BlockSpec((1, A_DK), lambda b, h, t: (0, 0))

    def slab(a):
        return _slab_spec(a, CAST_STEPS, lambda b, h, t: (b * A_HEADS + h) * nt + t)

    res = pl.pallas_call(
        functools.partial(_attn_a_kernel, lam_init=lam_init, t_off=t_off, n_cast=len(cast)),
        grid=(BATCH, A_HEADS, nt),
        in_specs=[
            pl.BlockSpec((1, TM, LANES), lambda b, h, t: (b, t + t_off, h)),
            pl.BlockSpec((1, N_TOK, LANES), lambda b, h, t: (b, 0, h)),
            pl.BlockSpec((1, N_TOK, LANES), lambda b, h, t: (b, 0, h)),
            vec, vec, vec, vec,
            pl.BlockSpec((1, A_DV), lambda b, h, t: (0, 0)),
        ] + [slab(a) for a in cast],
        out_specs=[pl.BlockSpec((1, TM, LANES), lambda b, h, t: (b, t, h))] + [slab(a) for a in cast],
        out_shape=[jax.ShapeDtypeStruct((BATCH, nt * TM, A_WIDTH), BF16)]
        + [jax.ShapeDtypeStruct(a.shape, BF16) for a in cast],
        scratch_shapes=[pltpu.VMEM((2 * TM, LANES), F32), pltpu.VMEM((2 * TM, 2 * A_DV), F32)],
        compiler_params=_cparams("arbitrary", "arbitrary", "arbitrary"),
        name="attn_diff",
    )(qa, ka, va, *lam_vecs, subln_g, *cast)
    return res[0], res[1:]


def _attn_c_kernel(sink_ref, q_ref, k0_ref, k1_ref, k2_ref, k3_ref, kx_ref,
                   v0_ref, v1_ref, v2_ref, v3_ref, vx_ref, o_ref, *, t_off):
    k_refs = (k0_ref, k1_ref, k2_ref, k3_ref)
    v_refs = (v0_ref, v1_ref, v2_ref, v3_ref)
    rows = C_GROUP * CBLK
    i = lax.broadcasted_iota(jnp.int32, (rows, CBLK), 0) & (CBLK - 1)
    j = lax.broadcasted_iota(jnp.int32, (rows, CBLK), 1)
    n_last = SEQ // CBLK - 1
    far = 4 * CBLK
    g_row = lax.broadcasted_iota(jnp.int32, (rows, 1), 0) // CBLK
    rmax = lambda s: jnp.max(s, axis=-1, keepdims=True)

    for a in range(2):
        n = 2 * pl.program_id(1) + a + t_off - CTX_LEN // CBLK
        off_p = jnp.where(n >= 1, 0, far)
        off_0 = jnp.where(n >= 0, 0, far)
        off_n = jnp.where((n >= 0) & (n < n_last), 0, far)
        qrow = slice(a * CBLK, (a + 1) * CBLK)
        for kvh in range(C_KV_HEADS):
            col = slice(kvh * C_DH, (kvh + 1) * C_DH)
            q0 = kvh * C_GROUP * C_DH
            q4 = jnp.concatenate(
                [q_ref[0, qrow, q0 + g * LANES:q0 + (g + 1) * LANES] for g in range(C_GROUP)], axis=0)
            k_loc = jnp.concatenate([r[0, :, col] for r in k_refs[a:a + 3]], axis=0)
            s_loc = lax.dot_general(q4, k_loc, NT_DIMS, preferred_element_type=F32)
            s_p = jnp.where(j >= i + off_p, s_loc[:, :CBLK], NEG)
            s_0 = jnp.where(j >= off_0, s_loc[:, CBLK:2 * CBLK], NEG)
            s_n = jnp.where(j + off_n <= i, s_loc[:, 2 * CBLK:], NEG)
            s_x = lax.dot_general(q4, kx_ref[0, :, col], NT_DIMS, preferred_element_type=F32)

            sink = jnp.zeros((rows, 1), F32)
            for g in range(C_GROUP):
                sink = jnp.where(g_row == g, sink_ref[kvh * C_GROUP + g] * LOG2E, sink)

            m = jnp.maximum(jnp.maximum(jnp.maximum(rmax(s_p), rmax(s_0)), jnp.maximum(rmax(s_n), rmax(s_x))),
                            sink)
            p = jnp.concatenate([jnp.exp2(s - m) for s in (s_p, s_0, s_n, s_x)], axis=1).astype(BF16)
            v = jnp.concatenate([r[0, :, col] for r in v_refs[a:a + 3]] + [vx_ref[0, :, col]], axis=0)
            ol = _dot(p, jnp.concatenate([v, jnp.ones_like(v)], axis=1))
            o = ol[:, :C_DH] / (ol[:, C_DH:] + jnp.exp2(sink - m))
            for g in range(C_GROUP):
                o_ref[0, qrow, q0 + g * LANES:q0 + (g + 1) * LANES] = o[g * CBLK:(g + 1) * CBLK].astype(BF16)


def _attn_c_call(sink, qc, kc, vc, with_ctx):
    t_off = 0 if with_ctx else CTX_LEN // CBLK
    nb = N_CBLK - t_off
    assert nb % 2 == 0 and t_off % 2 == 0

    def key_block(d):
        return pl.BlockSpec(
            (1, CBLK, C_KV_COLS),
            lambda b, t, s: (b, jnp.clip(2 * t + t_off + d - 1, 0, N_CBLK - 1), 0))

    keys = [key_block(d) for d in range(4)]
    ctx = pl.BlockSpec((1, CTX_LEN, C_KV_COLS), lambda b, t, s: (b, 0, 0))
    qi = pl.BlockSpec((1, 2 * CBLK, C_WIDTH), lambda b, t, s: (b, t + t_off // 2, 0))
    qo = pl.BlockSpec((1, 2 * CBLK, C_WIDTH), lambda b, t, s: (b, t, 0))
    return pl.pallas_call(
        functools.partial(_attn_c_kernel, t_off=t_off),
        grid_spec=pltpu.PrefetchScalarGridSpec(
            num_scalar_prefetch=1,
            grid=(BATCH, nb // 2),
            in_specs=[qi] + keys + [ctx] + keys + [ctx],
            out_specs=qo,
        ),
        out_shape=jax.ShapeDtypeStruct((BATCH, nb * CBLK, C_WIDTH), BF16),
        compiler_params=_cparams("arbitrary", "arbitrary"),
        name="attn_window",
    )(sink, qc, *([kc] * 5), *([vc] * 5))


HALO = 16


def _pack_bf16_pairs(h):
    hb = h.astype(BF16).astype(F32)
    half = h.shape[-1] // 2
    lo = lax.bitcast_convert_type(hb[:, :half], U32)
    hi = lax.bitcast_convert_type(hb[:, half:], U32)
    return (hi & jnp.uint32(0xFFFF0000)) | (lo >> 16)


def _unpack_bf16_pairs(p):
    lo = lax.bitcast_convert_type(p << 16, F32)
    hi = lax.bitcast_convert_type(p & jnp.uint32(0xFFFF0000), F32)
    return lo, hi


def _out_kernel(*refs, t_off, route):
    if route:
        x_ref = refs[0]
    else:
        ctx_ref, x_ref = refs[:2]
        refs = refs[1:]
    (oa_ref, bb_ref, u_ref, up_ref, un_ref, oc_ref, w_ref, ga_ref, bg_ref, cg_ref, cw_ref,
     gf_ref, scf_ref, shf_ref) = refs[1:15]
    refs = refs[15:]
    if route:
        wrh_ref, wrl_ref, xo_ref, h2_ref, lg_ref = refs
    else:
        xo_ref, h2_ref = refs
    t = pl.program_id(1) + t_off

    u = u_ref[0].astype(F32)
    row = lax.broadcasted_iota(jnp.int32, (TM, 1), 0)
    prev_ok = jnp.where(t >= 2, 1.0, 0.0)
    next_ok = jnp.where((t >= 1) & (t < N_TILES - 1), 1.0, 0.0)
    prev_row = up_ref[0, HALO - 1:HALO, :].astype(F32) * prev_ok
    next_row = un_ref[0, 0:1, :].astype(F32) * next_ok
    u_prev = jnp.where(row == 0, prev_row, pltpu.roll(u, 1, 0))
    u_next = jnp.where(row == TM - 1, next_row, pltpu.roll(u, TM - 1, 0))
    conv = u_prev * cw_ref[0:1, :] + u * cw_ref[1:2, :] + u_next * cw_ref[2:3, :]
    yb = _rms(bb_ref[0].astype(F32) * conv, B_WIDTH) * bg_ref[...]
    yc = _rms(oc_ref[0].astype(F32), C_WIDTH) * cg_ref[...]
    y = (_dot(oa_ref[0], w_ref[0:A_WIDTH, :])
         + _dot(yb.astype(BF16), w_ref[A_WIDTH:A_WIDTH + B_WIDTH, :])
         + _dot(yc.astype(BF16), w_ref[A_WIDTH + B_WIDTH:, :]))
    x_in = x_ref[0] if route else jnp.where(t == 0, ctx_ref[0], x_ref[0])
    x = x_in + ga_ref[0] * y
    xo_ref[0] = x
    h2 = _rms(x, D_MODEL) * gf_ref[...]
    h2 = h2 * (1.0 + scf_ref[0]) + shf_ref[0]
    if route:
        h2_ref[0] = _pack_bf16_pairs(h2)
        h_hi = h2.astype(BF16)
        h_lo = (h2 - h_hi.astype(F32)).astype(BF16)
        lg_ref[0] = _dot(h_hi, wrh_ref[...]) + (_dot(h_lo, wrh_ref[...]) + _dot(h_hi, wrl_ref[...]))
    else:
        h2_ref[0] = h2.astype(BF16)


def _out_call(x, oa, bb, u, oc, w, ga, bg, cg, cw, gfn, scf, shf, wr):
    route = wr is not None
    t_off = 1 if route else 0
    nt = N_TILES - t_off
    hpt = TM // HALO
    tile = lambda width: pl.BlockSpec((1, TM, width), lambda b, t: (b, t + t_off, 0))
    otile = lambda width: pl.BlockSpec((1, TM, width), lambda b, t: (b, t, 0))
    modv = pl.BlockSpec((1, 1, D_MODEL), lambda b, t: (_mod_row(b, t + t_off), 0, 0))
    full2 = lambda a: pl.BlockSpec(a.shape, lambda b, t: (0, 0))
    halo_p = pl.BlockSpec((1, HALO, B_WIDTH), lambda b, t: (b, jnp.maximum((t + t_off) * hpt - 1, 0), 0))
    halo_n = pl.BlockSpec((1, HALO, B_WIDTH),
                          lambda b, t: (b, jnp.minimum((t + t_off + 1) * hpt, N_TOK // HALO - 1), 0))
    x_specs, x_args = ([tile(D_MODEL)], [x]) if route else (_split_specs(), list(x))
    in_specs = x_specs + [otile(A_WIDTH), tile(B_WIDTH), tile(B_WIDTH), halo_p, halo_n, otile(C_WIDTH),
                          pl.BlockSpec(w.shape, lambda b, t: (0, 0), pipeline_mode=pl.Buffered(1)),
                          modv, full2(bg), full2(cg), full2(cw), full2(gfn), modv, modv]
    args = x_args + [oa, bb, u, u, u, oc, w, ga, bg, cg, cw, gfn, scf, shf]
    rows_out = nt * TM
    if route:
        wr_hi = wr.astype(BF16)
        wr_lo = (wr - wr_hi.astype(F32)).astype(BF16)
        in_specs += [full2(wr_hi), full2(wr_lo)]
        args += [wr_hi, wr_lo]
        out_specs = [otile(D_MODEL), otile(D_MODEL // 2), otile(LANES)]
        out_shape = [jax.ShapeDtypeStruct((BATCH, rows_out, D_MODEL), F32),
                     jax.ShapeDtypeStruct((BATCH, rows_out, D_MODEL // 2), U32),
                     jax.ShapeDtypeStruct((BATCH, rows_out, LANES), F32)]
    else:
        out_specs = [otile(D_MODEL), otile(D_MODEL)]
        out_shape = [jax.ShapeDtypeStruct((BATCH, rows_out, D_MODEL), F32),
                     jax.ShapeDtypeStruct((BATCH, rows_out, D_MODEL), BF16)]
    return pl.pallas_call(
        functools.partial(_out_kernel, t_off=t_off, route=route),
        grid=(BATCH, nt),
        in_specs=in_specs,
        out_specs=out_specs,
        out_shape=out_shape,
        compiler_params=_cparams("arbitrary", "arbitrary"),
        name="mix_out",
    )(*args)


def _swiglu_accumulate(h_ref, acc_ref, row_blocks, w1, w3, w2, packed=False):
    for rows in row_blocks:
        if packed:
            x_lo, x_hi = _unpack_bf16_pairs(h_ref[rows, :])
            h = jnp.concatenate([x_lo.astype(BF16), x_hi.astype(BF16)], axis=1)
        else:
            h = h_ref[rows, :]
        a = _dot(h, w1)
        b = _dot(h, w3)
        z = a / (1.0 + jnp.exp(-a)) * b
        acc_ref[rows, :] += _dot(z.astype(BF16), w2)


def _ffn_kernel(h_ref, w1_ref, w3_ref, w2_ref, y_ref, acc_ref):
    f = pl.program_id(1)

    @pl.when(f == 0)
    def _():
        acc_ref[...] = jnp.zeros(acc_ref.shape, F32)

    sub = FFN_TM // FFN_SPLIT
    _swiglu_accumulate(h_ref, acc_ref, [slice(i * sub, (i + 1) * sub) for i in range(FFN_SPLIT)],
                       w1_ref[...], w3_ref[...], w2_ref[...])

    @pl.when(f == pl.num_programs(1) - 1)
    def _():
        y_ref[...] = acc_ref[...].astype(BF16)


def _ffn_call(h, w1, w3, w2):
    rows = h.shape[0]
    return pl.pallas_call(
        _ffn_kernel,
        grid=(rows // FFN_TM, D_FF // FFN_TF),
        in_specs=[
            pl.BlockSpec((FFN_TM, D_MODEL), lambda i, f: (i, 0)),
            pl.BlockSpec((D_MODEL, FFN_TF), lambda i, f: (0, f)),
            pl.BlockSpec((D_MODEL, FFN_TF), lambda i, f: (0, f)),
            pl.BlockSpec((FFN_TF, D_MODEL), lambda i, f: (f, 0)),
        ],
        out_specs=pl.BlockSpec((FFN_TM, D_MODEL), lambda i, f: (i, 0)),
        out_shape=jax.ShapeDtypeStruct((rows, D_MODEL), BF16),
        scratch_shapes=[pltpu.VMEM((FFN_TM, D_MODEL), F32)],
        compiler_params=_cparams("arbitrary", "arbitrary"),
        name="ffn_dense",
    )(h, w1, w3, w2)


def _route_kernel(lg_ref, meta_ref, cnt_ref, carry_sc):
    @pl.when(pl.program_id(0) == 0)
    def _():
        carry_sc[...] = jnp.zeros(carry_sc.shape, F32)

    lane = lax.broadcasted_iota(jnp.int32, (ROUTE_TM, LANES), 1)
    lane_f = lane.astype(F32)
    lg = jnp.where(lane < N_EXPERTS, lg_ref[...], -jnp.inf)
    m1 = jnp.max(lg, axis=-1, keepdims=True)
    i1 = jnp.min(jnp.where(lg == m1, lane_f, float(LANES)), axis=-1, keepdims=True)
    oh1 = lane_f == i1
    lg2 = jnp.where(oh1, -jnp.inf, lg)
    m2 = jnp.max(lg2, axis=-1, keepdims=True)
    i2 = jnp.min(jnp.where(lg2 == m2, lane_f, float(LANES)), axis=-1, keepdims=True)
    oh2 = lane_f == i2
    e = jnp.exp(m2 - m1)
    g1 = 1.0 / (1.0 + e)
    g2 = e / (1.0 + e)
    oh = jnp.where(oh1, 1.0, 0.0) + jnp.where(oh2, 1.0, 0.0)
    r = lax.broadcasted_iota(jnp.int32, (ROUTE_TM, ROUTE_TM), 0)
    c = lax.broadcasted_iota(jnp.int32, (ROUTE_TM, ROUTE_TM), 1)
    tri = jnp.where(c < r, 1.0, 0.0).astype(BF16)
    before = _dot(tri, oh.astype(BF16)) + carry_sc[...]
    rank1 = jnp.sum(jnp.where(oh1, before, 0.0), axis=-1, keepdims=True)
    rank2 = jnp.sum(jnp.where(oh2, before, 0.0), axis=-1, keepdims=True)
    meta = jnp.zeros((ROUTE_TM, LANES), F32)
    for k, val in enumerate((i1, i2, g1, g2, rank1, rank2)):
        meta = jnp.where(lane == k, val, meta)
    meta_ref[...] = meta
    carry_sc[...] += jnp.sum(oh, axis=0, keepdims=True)
    cnt_ref[...] = carry_sc[...]


def _route_call(logits):
    rows = logits.shape[0]
    return pl.pallas_call(
        _route_kernel,
        grid=(rows // ROUTE_TM,),
        in_specs=[pl.BlockSpec((ROUTE_TM, LANES), lambda i: (i, 0))],
        out_specs=[pl.BlockSpec((ROUTE_TM, LANES), lambda i: (i, 0)),
                   pl.BlockSpec((1, LANES), lambda i: (0, 0))],
        out_shape=[jax.ShapeDtypeStruct((rows, LANES), F32), jax.ShapeDtypeStruct((1, LANES), F32)],
        scratch_shapes=[pltpu.VMEM((1, LANES), F32)],
        compiler_params=_cparams("arbitrary"),
        name="moe_route",
    )(logits)


INV_TM = 512


def _invert_kernel(pos_ref, zeros_ref, inv_ref, sem):
    @pl.when(pl.program_id(0) == 0)
    def _():
        fill = pltpu.make_async_copy(zeros_ref, inv_ref, sem)
        fill.start()
        fill.wait()

    base = pl.program_id(0) * INV_TM

    def scatter(r, carry):
        inv_ref[pos_ref[base + r]] = (base + r) >> 1
        return carry
    lax.fori_loop(0, INV_TM, scatter, 0, unroll=8)


def _invert_call(pos):
    return pl.pallas_call(
        _invert_kernel,
        grid_spec=pltpu.PrefetchScalarGridSpec(
            num_scalar_prefetch=1,
            grid=(pos.shape[0] // INV_TM,),
            in_specs=[pl.BlockSpec(memory_space=pl.ANY)],
            out_specs=pl.BlockSpec(memory_space=pltpu.SMEM),
            scratch_shapes=[pltpu.SemaphoreType.DMA(())],
        ),
        out_shape=jax.ShapeDtypeStruct((MOE_ROWS,), jnp.int32),
        compiler_params=_cparams("arbitrary"),
        name="moe_invert",
    )(pos, jnp.zeros((MOE_ROWS,), jnp.int32))


def _moe_kernel(inv_ref, ps_ref, pn_ref, pd_ref, h_ref, w1_ref, w3_ref, w2_ref, y_ref,
                xg_sc, acc_sc, st_sc, gsem, osem):
    p = pl.program_id(0)
    f = pl.program_id(1)
    n_f = pl.num_programs(1)
    start = ps_ref[p]
    n = pn_ref[p]
    subs = [slice(i * MOE_SUB, (i + 1) * MOE_SUB) for i in range(MOE_NSUB)]

    def row_copy(src_row, r):
        return pltpu.make_async_copy(h_ref.at[pl.ds(src_row, 1)], xg_sc.at[pl.ds(r, 1)], gsem)

    @pl.when((p == 0) & (f == 0))
    def _():
        st_sc[0] = jnp.zeros(st_sc.shape[1:], U32)
        tail = [pltpu.make_async_copy(st_sc.at[0], y_ref.at[pl.ds(r0, MOE_SUB)], osem.at[0])
                for r0 in range(2 * BATCH * SEQ, MOE_ROWS, MOE_SUB)]
        for copy in tail:
            copy.start()
        for copy in tail:
            copy.wait()

    @pl.when(f == 0)
    def _():
        def per_sub(fn):
            def body(sub, carry):
                lax.fori_loop(0, MOE_SUB, functools.partial(fn, sub * MOE_SUB), 0, unroll=8)
                return carry
            lax.fori_loop(0, n, body, 0)

        def issue(r0, r, carry):
            row_copy(inv_ref[start * MOE_SUB + r0 + r], r0 + r).start()
            return carry

        def wait(r0, r, carry):
            row_copy(0, r0 + r).wait()
            return carry

        per_sub(issue)
        for sub, rows in enumerate(subs):
            @pl.when(sub < n)
            def _():
                acc_sc[rows, :] = jnp.zeros((MOE_SUB, D_MODEL), F32)
        per_sub(wait)

    def weights():
        return w1_ref[0, 0], w3_ref[0, 0], w2_ref[0, 0]

    def accumulate(row_blocks, w):
        _swiglu_accumulate(xg_sc, acc_sc, row_blocks, *w, packed=True)

    @pl.when(n >= MOE_FAST)
    def _():
        w = weights()
        fast_rows = MOE_FAST * MOE_SUB
        accumulate([slice(r, r + MOE_FAST_ROWS) for r in range(0, fast_rows, MOE_FAST_ROWS)], w)
        for sub in range(MOE_FAST, MOE_NSUB):
            @pl.when(sub < n)
            def _():
                accumulate([subs[sub]], w)

    @pl.when((n > 0) & (n < MOE_FAST))
    def _():
        w = weights()
        for sub in range(MOE_FAST):
            @pl.when(sub < n)
            def _():
                accumulate([subs[sub]], w)

    @pl.when(f == n_f - 1)
    def _():
        def out_copy(sub):
            row0 = pl.multiple_of((start + sub) * MOE_SUB, MOE_SUB)
            return pltpu.make_async_copy(st_sc.at[sub % 2], y_ref.at[pl.ds(row0, MOE_SUB)], osem.at[sub % 2])

        for sub, rows in enumerate(subs):
            @pl.when(sub < n)
            def _():
                if sub >= 2:
                    out_copy(sub - 2).wait()
                st_sc[sub % 2] = _pack_bf16_pairs(acc_sc[rows, :])
                out_copy(sub).start()

        for sub in range(MOE_NSUB):
            @pl.when((sub < n) & (sub + 2 >= n))
            def _():
                out_copy(sub).wait()


def _moe_call(inv, pair_tables, h_packed, w1, w3, w2):
    n_f = D_FF_EXPERT // MOE_TF

    def f_eff(p, f, pn):
        return jnp.where(pn[p] > 0, f, n_f - 1)

    w13 = pl.BlockSpec((1, 1, D_MODEL, MOE_TF), lambda p, f, inv, ps, pn, pd: (0, pd[p], 0, f_eff(p, f, pn)))
    w2s = pl.BlockSpec((1, 1, MOE_TF, D_MODEL), lambda p, f, inv, ps, pn, pd: (0, pd[p], f_eff(p, f, pn), 0))
    hbm = pl.BlockSpec(memory_space=pl.ANY)
    win = MOE_NSUB * MOE_SUB
    n_prefetch = 4
    return pl.pallas_call(
        _moe_kernel,
        grid_spec=pltpu.PrefetchScalarGridSpec(
            num_scalar_prefetch=n_prefetch,
            grid=(MOE_PAIRS, n_f),
            in_specs=[hbm, w13, w13, w2s],
            out_specs=hbm,
            scratch_shapes=[pltpu.VMEM((win, D_MODEL // 2), U32),
                            pltpu.VMEM((win, D_MODEL), F32),
                            pltpu.VMEM((2, MOE_SUB, D_MODEL // 2), U32),
                            pltpu.SemaphoreType.DMA(()),
                            pltpu.SemaphoreType.DMA((2,))],
        ),
        out_shape=jax.ShapeDtypeStruct((MOE_ROWS, D_MODEL // 2), U32),
        compiler_params=_cparams("arbitrary", "arbitrary"),
        name="moe_experts",
    )(inv, *pair_tables, h_packed, w1, w3, w2)


def _pair_schedule(counts):
    nsub = (counts + MOE_SUB - 1) // MOE_SUB
    seg_start = jnp.cumsum(nsub) - nsub
    nchunk = (nsub + MOE_NSUB - 1) // MOE_NSUB
    chunk_end = jnp.cumsum(nchunk)
    idx = jnp.arange(MOE_PAIRS, dtype=jnp.int32)
    expert = jnp.sum(chunk_end[None, :] <= idx[:, None], axis=1).astype(jnp.int32)
    active = expert < N_EXPERTS
    onehot = expert[:, None] == jnp.arange(N_EXPERTS, dtype=jnp.int32)[None, :]
    pick = lambda a: jnp.sum(jnp.where(onehot, a[None, :], 0), axis=1)
    c = idx - (pick(chunk_end) - pick(nchunk))
    first = jnp.where(active, pick(seg_start) + MOE_NSUB * c, 0)
    count = jnp.where(active, jnp.minimum(MOE_NSUB, pick(nsub) - MOE_NSUB * c), 0)
    dma_expert = jnp.where(active, expert, jnp.max(jnp.where(active, expert, 0)))
    i32 = lambda a: a.astype(jnp.int32)
    return seg_start * MOE_SUB, tuple(i32(a) for a in (first, count, dma_expert))


def _combine_kernel(pos_ref, x_ref, meta_ref, gf_ref, ys_ref, o_ref, buf, sem):
    base = pl.program_id(0) * COMB_TM

    def copy(r, k, src_row):
        return pltpu.make_async_copy(ys_ref.at[pl.ds(src_row, 1)], buf.at[k, pl.ds(r, 1)], sem)

    def start(r, carry):
        for k in range(2):
            copy(r, k, pos_ref[2 * (base + r) + k]).start()
        return carry

    lax.fori_loop(0, COMB_TM, start, 0, unroll=8)

    def wait(r, carry):
        for k in range(2):
            copy(r, k, 0).wait()
        return carry

    lax.fori_loop(0, COMB_TM, wait, 0, unroll=8)

    half = D_MODEL // 2
    lo0, hi0 = _unpack_bf16_pairs(buf[0])
    lo1, hi1 = _unpack_bf16_pairs(buf[1])
    g1 = meta_ref[:, 2:3]
    g2 = meta_ref[:, 3:4]
    gf = gf_ref[0]
    o_ref[:, :half] = x_ref[:, :half] + gf[:, :half] * (g1 * lo0 + g2 * lo1)
    o_ref[:, half:] = x_ref[:, half:] + gf[:, half:] * (g1 * hi0 + g2 * hi1)


def _combine_call(pos, x, meta, gf, ys):
    rows = x.shape[0]
    tiles_per_batch = SEQ // COMB_TM
    return pl.pallas_call(
        _combine_kernel,
        grid_spec=pltpu.PrefetchScalarGridSpec(
            num_scalar_prefetch=1,
            grid=(rows // COMB_TM,),
            in_specs=[
                pl.BlockSpec((COMB_TM, D_MODEL), lambda i, p: (i, 0)),
                pl.BlockSpec((COMB_TM, LANES), lambda i, p: (i, 0)),
                pl.BlockSpec((1, 1, D_MODEL), lambda i, p: (i // tiles_per_batch, 0, 0)),
                pl.BlockSpec(memory_space=pl.ANY),
            ],
            out_specs=pl.BlockSpec((COMB_TM, D_MODEL), lambda i, p: (i, 0)),
            scratch_shapes=[pltpu.VMEM((2, COMB_TM, D_MODEL // 2), U32), pltpu.SemaphoreType.DMA(())],
        ),
        out_shape=jax.ShapeDtypeStruct((rows, D_MODEL), F32),
        compiler_params=_cparams("arbitrary"),
        name="moe_combine",
    )(pos, x, meta, gf, ys)


def _moe_layer(x, h_packed, logits, gf, w1, w3, w2):
    meta, cnt = _route_call(logits)
    expert = meta[:, 0:2].astype(jnp.int32)
    rank = meta[:, 4:6].astype(jnp.int32)
    counts = cnt[0, :N_EXPERTS].astype(jnp.int32)
    seg_start, pair_tables = _pair_schedule(counts)
    ids = jnp.arange(N_EXPERTS, dtype=jnp.int32)
    pos = rank + jnp.sum(jnp.where(expert[..., None] == ids, seg_start, 0), axis=-1)
    pos = pos.reshape(-1).astype(jnp.int32)
    ys = _moe_call(_invert_call(pos), pair_tables, h_packed, w1, w3, w2)
    return _combine_call(pos, x, meta, gf, ys)


def _rope_tables(dh):
    rows = SEQ // GRID_W
    row = jnp.repeat(jnp.arange(rows, dtype=F32), GRID_W)
    col = jnp.tile(jnp.arange(GRID_W, dtype=F32), rows)
    half = dh // 2
    inv = ROPE_THETA ** (-jnp.arange(0, half, 2, dtype=F32) / half)
    ar = row[:, None] * inv[None, :]
    ac = col[:, None] * inv[None, :]
    ang = jnp.concatenate([ar, ar, ac, ac], axis=-1)
    ang = jnp.concatenate([jnp.zeros((CTX_LEN, dh), F32), ang], axis=0)
    ang = jnp.tile(ang, (1, LANES // dh))
    odd = ((jnp.arange(LANES) // (dh // 4)) % 2 == 1)[None, :]
    sin = jnp.sin(ang)
    return jnp.cos(ang), jnp.where(odd, sin, 0.0), jnp.where(odd, 0.0, -sin)


def kernel(x, c, ctx, c_ctx, ada_w, ada_b, norm_mix_g, norm_ffn_g, w_in, w_out, a_q_norm, a_k_norm,
           a_lam_q1, a_lam_k1, a_lam_q2, a_lam_k2, a_subln_g, b_conv_w, b_out_g, c_q_norm, c_k_norm,
           c_sink, c_out_g, ffn_w1, ffn_w3, ffn_w2, router_w, moe_w1, moe_w3, moe_w2):
    assert x.shape == (BATCH, SEQ, D_MODEL) and ctx.shape == (BATCH, CTX_LEN, D_MODEL)
    cond = jnp.concatenate([c, c_ctx[None, :], jnp.zeros((8 - BATCH - 1, D_MODEL), F32)], axis=0)
    mod = _ada_call(cond, ada_w, ada_b)
    tables = _rope_tables(A_DK) + _rope_tables(C_DH)
    xs = (ctx, x)
    y_prev = gf_prev = None
    out = None
    moe_bf16 = []
    assert DEPTH == 2 and moe_w1.shape[0] == 1
    for l in range(DEPTH):
        last = l == DEPTH - 1
        sh_a, sc_a, g_a, sh_f, sc_f, g_f = [
            mod[l, :BATCH + 1, k * D_MODEL:(k + 1) * D_MODEL].reshape(BATCH + 1, 1, D_MODEL) for k in range(6)]
        gains = (jnp.tile(a_q_norm[l], LANES // A_DK)[None, :], jnp.tile(a_k_norm[l], LANES // A_DK)[None, :],
                 c_q_norm[l][None, :], c_k_norm[l][None, :])
        side = [] if l else [(ffn_w1[0], None), (ffn_w3[0], None), (ffn_w2[0], None),
                             (w_in, 1), (w_out, 0), (w_out, 1)]
        res = _in_call(xs, y_prev, gf_prev, norm_mix_g[l][None, :], sc_a, sh_a,
                       w_in_bf16[1] if l else w_in[0].astype(BF16), gains, tables, side)
        if y_prev is not None:
            xs, res = res[0], res[1:]
        qa, ka, va, bb, u, qc, kc, vc = res[:8]
        if not l:
            ffn_bf16, w_in_bf16, w_out_bf16 = res[8:11], (None, res[11]), res[12:14]
        lam_init = 0.8 - 0.6 * math.exp(-0.3 * l)
        lam_vecs = [v[l][None, :] for v in (a_lam_q1, a_lam_k1, a_lam_q2, a_lam_k2)]
        cast_jobs = ((moe_w1[0], moe_w2[0]), (moe_w3[0],))[l]
        oa, cast_out = _attn_a_call(qa, ka, va, lam_vecs, a_subln_g[l][None, :], lam_init, with_ctx=not last,
                                    cast=[w.reshape(-1, w.shape[-1]) for w in cast_jobs])
        moe_bf16 += [c.reshape((1,) + w.shape) for c, w in zip(cast_out, cast_jobs)]
        oc = _attn_c_call(c_sink[l], qc, kc, vc, with_ctx=not last)
        wr = None
        if last:
            wr = jnp.pad(router_w[l // 2], ((0, 0), (0, LANES - N_EXPERTS)))
        res = _out_call(xs, oa, bb, u, oc, w_out_bf16[l], g_a, b_out_g[l][None, :],
                        c_out_g[l][None, :], b_conv_w[l], norm_ffn_g[l][None, :], sc_f, sh_f, wr)
        if not last:
            xs, h2 = res
            y = _ffn_call(h2.reshape(BATCH * N_TOK, D_MODEL), *ffn_bf16)
            y_prev, gf_prev = y.reshape(BATCH, N_TOK, D_MODEL), g_f
        else:
            x_lat, h_packed, logits = res
            w1b, w2b, w3b = moe_bf16
            out = _moe_layer(x_lat.reshape(BATCH * SEQ, D_MODEL), h_packed.reshape(BATCH * SEQ, D_MODEL // 2),
                             logits.reshape(BATCH * SEQ, LANES), g_f[:BATCH], w1b, w3b, w2b)
    return out.reshape(BATCH, SEQ, D_MODEL)
```

```python
import functools
import math

import jax
import jax.numpy as jnp
from jax import lax
from jax.experimental import pallas as pl
from jax.experimental.pallas import tpu as pltpu

F32 = jnp.float32
BF16 = jnp.bfloat16
U32 = jnp.uint32

D_MODEL = 2048
BATCH = 2
SEQ = 4096
DEPTH = 2
GRID_W = 64
CTX_LEN = 256
EPS = 1e-6
ROPE_THETA = 10000.0

A_HEADS = 4
A_DK = 64
A_DV = 128
A_WIDTH = A_HEADS * A_DV
B_WIDTH = 512
C_HEADS = 8
C_KV_HEADS = 2
C_GROUP = C_HEADS // C_KV_HEADS
C_DH = 128
C_WIDTH = C_HEADS * C_DH
C_KV_COLS = C_KV_HEADS * C_DH
WINDOW = 128
MIX_WIDTH = A_WIDTH + B_WIDTH + C_WIDTH
IN_COLS = 3 * A_WIDTH + 3 * B_WIDTH + C_WIDTH + 2 * C_KV_COLS
D_FF = 5632
N_EXPERTS = 8
D_FF_EXPERT = 7168

COL_AQ, COL_AK, COL_AV = 0, 512, 1024
COL_BB, COL_BC, COL_BX = 1536, 2048, 2560
COL_CQ, COL_CK, COL_CV = 3072, 4096, 4352

LANES = 128
N_TOK = CTX_LEN + SEQ
TM = 256
N_TILES = N_TOK // TM
IN_TN = 512
CBLK = 128
N_CBLK = N_TOK // CBLK
A_CHUNK = 1024
A_SPLIT = 4
CAST_STEPS = 128
LOG2E = 1.4426950408889634
FFN_TM = N_TOK // 4
FFN_TF = 512
IN_CAST_STEPS = 32
FFN_SPLIT = 4
MOE_SUB = 256
MOE_NSUB = 10
MOE_FAST = 8
MOE_FAST_ROWS = 256
MOE_TF = 512
MOE_ROWS = 2 * BATCH * SEQ + N_EXPERTS * MOE_SUB
MOE_PAIRS = (MOE_ROWS // MOE_SUB + N_EXPERTS * (MOE_NSUB - 1)) // MOE_NSUB
ROUTE_TM = 512
COMB_TM = 256
VMEM_LIMIT = 56 * 1024 * 1024
NEG = -1e30

NT_DIMS = (((1,), (1,)), ((), ()))


def _cparams(*sem):
    return pltpu.CompilerParams(dimension_semantics=sem, vmem_limit_bytes=VMEM_LIMIT)


def _dot(a, b):
    return jnp.dot(a, b, preferred_element_type=F32)


def _rms(x, width):
    return x * lax.rsqrt(jnp.sum(x * x, axis=-1, keepdims=True) * (1.0 / width) + EPS)


def _slab_spec(a, n_steps, step_of, layer=None):
    rows = a.shape[-2] // n_steps
    assert rows * n_steps == a.shape[-2] and rows % 16 == 0
    step = lambda *g: jnp.minimum(step_of(*g), n_steps - 1)
    if layer is None:
        return pl.BlockSpec((rows, a.shape[-1]), lambda *g: (step(*g), 0))
    return pl.BlockSpec((1, rows, a.shape[-1]), lambda *g: (layer, step(*g), 0))


def _cast_slabs(step, n_steps, srcs, dsts):
    @pl.when(step < n_steps)
    def _():
        for src, dst in zip(srcs, dsts):
            dst[...] = (src[0] if len(src.shape) == 3 else src[...]).astype(BF16)


ADA_TN = 1024


def _ada_kernel(c_ref, w_ref, b_ref, o_ref):
    a = c_ref[...]
    a = a / (1.0 + jnp.exp(-a))
    o_ref[0] = _dot(a.astype(BF16), w_ref[0].astype(BF16)) + b_ref[0]


def _ada_call(cond, ada_w, ada_b):
    n = 6 * D_MODEL
    return pl.pallas_call(
        _ada_kernel,
        grid=(DEPTH, n // ADA_TN),
        in_specs=[
            pl.BlockSpec((8, D_MODEL), lambda l, j: (0, 0)),
            pl.BlockSpec((1, D_MODEL, ADA_TN), lambda l, j: (l, 0, j)),
            pl.BlockSpec((1, 1, ADA_TN), lambda l, j: (l, 0, j)),
        ],
        out_specs=pl.BlockSpec((1, 8, ADA_TN), lambda l, j: (l, 0, j)),
        out_shape=jax.ShapeDtypeStruct((DEPTH, 8, n), F32),
        compiler_params=_cparams("arbitrary", "arbitrary"),
        name="ada_mod",
    )(cond, ada_w, ada_b.reshape(DEPTH, 1, n))


def _rope128(n, cos, s_odd, s_even, quarter):
    return n * cos + pltpu.roll(n, quarter, 1) * s_odd + pltpu.roll(n, LANES - quarter, 1) * s_even


def _in_kernel(*refs, with_y, n_cast):
    if with_y:
        x_ref, y_ref, gf_ref = refs[:3]
    else:
        ctx_ref, x_ref = refs[:2]
    refs = refs[3 if with_y else 2:]
    (g_ref, sc_ref, sh_ref, w_ref, aqn_ref, akn_ref, cqn_ref, ckn_ref,
     ca_ref, sa1_ref, sa2_ref, cc_ref, sc1_ref, sc2_ref) = refs[:14]
    cast_src = refs[14:14 + n_cast]
    outs = refs[14 + n_cast:]
    if with_y:
        xo_ref, outs = outs[0], outs[1:]
    qa_ref, ka_ref, va_ref, bb_ref, u_ref, qc_ref, kc_ref, vc_ref = outs[:8]
    _cast_slabs(pl.program_id(0) * pl.num_programs(1) + pl.program_id(1), IN_CAST_STEPS, cast_src, outs[8:])

    if with_y:
        x = x_ref[0] + gf_ref[0] * y_ref[0].astype(F32)
        xo_ref[0] = x
    else:
        x = jnp.where(pl.program_id(1) == 0, ctx_ref[0], x_ref[0])
    h = _rms(x, D_MODEL) * g_ref[...]
    h = h * (1.0 + sc_ref[0]) + sh_ref[0]
    hb = h.astype(BF16)

    lane = lax.broadcasted_iota(jnp.int32, (TM, LANES), 1)
    lo_half = lane < A_DK

    def proj(col0, width):
        return _dot(hb, w_ref[:, col0:col0 + width])

    def head_a(p, gain_ref, scale):
        p2 = p * p
        s_all = jnp.sum(p2, axis=-1, keepdims=True)
        s_lo = jnp.sum(jnp.where(lo_half, p2, 0.0), axis=-1, keepdims=True)
        ms = jnp.where(lo_half, s_lo, s_all - s_lo) * (1.0 / A_DK)
        n = p * lax.rsqrt(ms + EPS) * gain_ref[...]
        r = _rope128(n, ca_ref[...], sa1_ref[...], sa2_ref[...], A_DK // 4)
        return (r * scale).astype(BF16)

    def head_c(p, gain_ref, scale):
        n = _rms(p, C_DH) * gain_ref[...]
        r = _rope128(n, cc_ref[...], sc1_ref[...], sc2_ref[...], C_DH // 4)
        return (r * scale).astype(BF16)

    def heads(o_ref, col0, width, fn, gain_ref, scale):
        for c in range(0, width, IN_TN):
            wd = min(IN_TN, width - c)
            p = proj(col0 + c, wd)
            for j in range(wd // LANES):
                o_ref[0, :, c + j * LANES:c + (j + 1) * LANES] = fn(
                    p[:, j * LANES:(j + 1) * LANES], gain_ref, scale)

    heads(qa_ref, COL_AQ, A_WIDTH, head_a, aqn_ref, A_DK ** -0.5 * LOG2E)
    heads(ka_ref, COL_AK, A_WIDTH, head_a, akn_ref, 1.0)
    va_ref[0] = proj(COL_AV, A_WIDTH).astype(BF16)
    bb_ref[0] = proj(COL_BB, B_WIDTH).astype(BF16)
    u_ref[0] = (proj(COL_BC, B_WIDTH) * proj(COL_BX, B_WIDTH)).astype(BF16)
    heads(qc_ref, COL_CQ, C_WIDTH, head_c, cqn_ref, C_DH ** -0.5 * LOG2E)
    heads(kc_ref, COL_CK, C_KV_COLS, head_c, ckn_ref, 1.0)
    vc_ref[0] = proj(COL_CV, C_KV_COLS).astype(BF16)


def _mod_row(b, t):
    return jnp.where(t == 0, BATCH, b)


def _split_specs():
    return [pl.BlockSpec((1, TM, D_MODEL), lambda b, t: (b, 0, 0)),
            pl.BlockSpec((1, TM, D_MODEL), lambda b, t: (b, jnp.maximum(t - 1, 0), 0))]


def _in_call(x, y, gf, g, sc, sh, w, gains, tables, cast):
    with_y = y is not None
    tile = lambda width: pl.BlockSpec((1, TM, width), lambda b, t: (b, t, 0))
    modv = pl.BlockSpec((1, 1, D_MODEL), lambda b, t: (_mod_row(b, t), 0, 0))
    full2 = lambda a: pl.BlockSpec(a.shape, lambda b, t: (0, 0))
    rows = pl.BlockSpec((TM, LANES), lambda b, t: (t, 0))
    slab = lambda a, layer=None: _slab_spec(a, IN_CAST_STEPS, lambda b, t: b * N_TILES + t, layer)
    if with_y:
        in_specs = [tile(D_MODEL), tile(D_MODEL), modv]
        args = [x, y, gf]
    else:
        in_specs = _split_specs()
        args = list(x)
    in_specs += [full2(g), modv, modv,
                 pl.BlockSpec(w.shape, lambda b, t: (0, 0), pipeline_mode=pl.Buffered(1))]
    args += [g, sc, sh, w]
    in_specs += [full2(a) for a in gains]
    args += list(gains)
    in_specs += [rows] * 6
    args += list(tables)
    in_specs += [slab(a, layer) for a, layer in cast]
    args += [a for a, _ in cast]
    widths = (A_WIDTH, A_WIDTH, A_WIDTH, B_WIDTH, B_WIDTH, C_WIDTH, C_KV_COLS, C_KV_COLS)
    cast_out = [jax.ShapeDtypeStruct(a.shape[-2:], BF16) for a, _ in cast]
    out_specs = [tile(wd) for wd in widths] + [slab(o) for o in cast_out]
    out_shape = [jax.ShapeDtypeStruct((BATCH, N_TOK, wd), BF16) for wd in widths] + cast_out
    if with_y:
        out_specs = [tile(D_MODEL)] + out_specs
        out_shape = [jax.ShapeDtypeStruct((BATCH, N_TOK, D_MODEL), F32)] + out_shape
    assert BATCH * N_TILES >= IN_CAST_STEPS
    return pl.pallas_call(
        functools.partial(_in_kernel, with_y=with_y, n_cast=len(cast)),
        grid=(BATCH, N_TILES),
        in_specs=in_specs,
        out_specs=out_specs,
        out_shape=out_shape,
        compiler_params=_cparams("arbitrary", "arbitrary"),
        name="in_proj",
    )(*args)


def _attn_a_kernel(*refs, lam_init, t_off, n_cast):
    q_ref, k_ref, v_ref, lq1_ref, lk1_ref, lq2_ref, lk2_ref, g_ref = refs[:8]
    cast_src = refs[8:8 + n_cast]
    o_ref = refs[8 + n_cast]
    cast_dst = refs[9 + n_cast:9 + 2 * n_cast]
    m_sc, acc_sc = refs[9 + 2 * n_cast:]
    t = pl.program_id(2) + t_off

    step_idx = (pl.program_id(0) * pl.num_programs(1) + pl.program_id(1)) * pl.num_programs(2) + pl.program_id(2)
    _cast_slabs(step_idx, CAST_STEPS, cast_src, cast_dst)

    q = q_ref[0]
    lane = lax.broadcasted_iota(jnp.int32, (TM, LANES), 1)
    zero = jnp.zeros_like(q)
    qs = jnp.concatenate([jnp.where(lane < A_DK, q, zero), jnp.where(lane >= A_DK, q, zero)], axis=0)

    m_sc[...] = jnp.full(m_sc.shape, NEG, F32)
    acc_sc[...] = jnp.zeros(acc_sc.shape, F32)

    def step(k, v):
        v1 = jnp.concatenate([v, jnp.ones_like(v)], axis=1)
        grp = 2 * TM // A_SPLIT
        for g in range(A_SPLIT):
            rows = slice(g * grp, (g + 1) * grp)
            s = lax.dot_general(qs[rows], k, NT_DIMS, preferred_element_type=F32)
            m_prev = m_sc[rows, :]
            m_new = jnp.maximum(m_prev, jnp.max(s, axis=-1, keepdims=True))
            alpha = jnp.exp2(m_prev - m_new)
            p = jnp.exp2(s - jnp.tile(m_new, (1, s.shape[1] // LANES)))
            acc_sc[rows, :] = jnp.tile(alpha, (1, 2)) * acc_sc[rows, :] + _dot(p.astype(BF16), v1)
            m_sc[rows, :] = m_new

    def keys(bounds):
        for lo, hi in zip(bounds[:-1], bounds[1:]):
            step(k_ref[0, lo:hi, :], v_ref[0, lo:hi, :])

    all_keys = functools.partial(keys, [0] + list(range(CTX_LEN + A_CHUNK, N_TOK + 1, A_CHUNK)))
    if t_off == 0:
        pl.when(t == 0)(functools.partial(keys, [0, CTX_LEN]))
        pl.when(t > 0)(all_keys)
    else:
        all_keys()

    lam = (jnp.exp(jnp.sum(lq1_ref[...] * lk1_ref[...], keepdims=True))
           - jnp.exp(jnp.sum(lq2_ref[...] * lk2_ref[...], keepdims=True)) + lam_init)
    acc = acc_sc[...]
    o = acc[:, :A_DV] / acc[:, A_DV:]
    o = o[:TM] - lam * o[TM:]
    o_ref[0] = (_rms(o, A_DV) * g_ref[...] * (1.0 - lam_init)).astype(BF16)


def _attn_a_call(qa, ka, va, lam_vecs, subln_g, lam_init, with_ctx, cast):
    t_off = 0 if with_ctx else 1
    nt = N_TILES - t_off
    assert BATCH * A_HEADS * nt >= CAST_STEPS
    vec = pl.BlockSpec((1, A_DK), lambda b, h, t: (0, 0))

    def slab(a):
        return _slab_spec(a, CAST_STEPS, lambda b, h, t: (b * A_HEADS + h) * nt + t)

    res = pl.pallas_call(
        functools.partial(_attn_a_kernel, lam_init=lam_init, t_off=t_off, n_cast=len(cast)),
        grid=(BATCH, A_HEADS, nt),
        in_specs=[
            pl.BlockSpec((1, TM, LANES), lambda b, h, t: (b, t + t_off, h)),
            pl.BlockSpec((1, N_TOK, LANES), lambda b, h, t: (b, 0, h)),
            pl.BlockSpec((1, N_TOK, LANES), lambda b, h, t: (b, 0, h)),
            vec, vec, vec, vec,
            pl.BlockSpec((1, A_DV), lambda b, h, t: (0, 0)),
        ] + [slab(a) for a in cast],
        out_specs=[pl.BlockSpec((1, TM, LANES), lambda b, h, t: (b, t, h))] + [slab(a) for a in cast],
        out_shape=[jax.ShapeDtypeStruct((BATCH, nt * TM, A_WIDTH), BF16)]
        + [jax.ShapeDtypeStruct(a.shape, BF16) for a in cast],
        scratch_shapes=[pltpu.VMEM((2 * TM, LANES), F32), pltpu.VMEM((2 * TM, 2 * A_DV), F32)],
        compiler_params=_cparams("arbitrary", "arbitrary", "arbitrary"),
        name="attn_diff",
    )(qa, ka, va, *lam_vecs, subln_g, *cast)
    return res[0], res[1:]


def _attn_c_kernel(sink_ref, q_ref, k0_ref, k1_ref, k2_ref, k3_ref, kx_ref,
                   v0_ref, v1_ref, v2_ref, v3_ref, vx_ref, o_ref, *, t_off):
    k_refs = (k0_ref, k1_ref, k2_ref, k3_ref)
    v_refs = (v0_ref, v1_ref, v2_ref, v3_ref)
    rows = C_GROUP * CBLK
    i = lax.broadcasted_iota(jnp.int32, (rows, CBLK), 0) & (CBLK - 1)
    j = lax.broadcasted_iota(jnp.int32, (rows, CBLK), 1)
    n_last = SEQ // CBLK - 1
    far = 4 * CBLK
    g_row = lax.broadcasted_iota(jnp.int32, (rows, 1), 0) // CBLK
    rmax = lambda s: jnp.max(s, axis=-1, keepdims=True)

    for a in range(2):
        n = 2 * pl.program_id(1) + a + t_off - CTX_LEN // CBLK
        off_p = jnp.where(n >= 1, 0, far)
        off_0 = jnp.where(n >= 0, 0, far)
        off_n = jnp.where((n >= 0) & (n < n_last), 0, far)
        qrow = slice(a * CBLK, (a + 1) * CBLK)
        for kvh in range(C_KV_HEADS):
            col = slice(kvh * C_DH, (kvh + 1) * C_DH)
            q0 = kvh * C_GROUP * C_DH
            q4 = jnp.concatenate(
                [q_ref[0, qrow, q0 + g * LANES:q0 + (g + 1) * LANES] for g in range(C_GROUP)], axis=0)
            k_loc = jnp.concatenate([r[0, :, col] for r in k_refs[a:a + 3]], axis=0)
            s_loc = lax.dot_general(q4, k_loc, NT_DIMS, preferred_element_type=F32)
            s_p = jnp.where(j >= i + off_p, s_loc[:, :CBLK], NEG)
            s_0 = jnp.where(j >= off_0, s_loc[:, CBLK:2 * CBLK], NEG)
            s_n = jnp.where(j + off_n <= i, s_loc[:, 2 * CBLK:], NEG)
            s_x = lax.dot_general(q4, kx_ref[0, :, col], NT_DIMS, preferred_element_type=F32)

            sink = jnp.zeros((rows, 1), F32)
            for g in range(C_GROUP):
                sink = jnp.where(g_row == g, sink_ref[kvh * C_GROUP + g] * LOG2E, sink)

            m = jnp.maximum(jnp.maximum(jnp.maximum(rmax(s_p), rmax(s_0)), jnp.maximum(rmax(s_n), rmax(s_x))),
                            sink)
            p = jnp.concatenate([jnp.exp2(s - m) for s in (s_p, s_0, s_n, s_x)], axis=1).astype(BF16)
            v = jnp.concatenate([r[0, :, col] for r in v_refs[a:a + 3]] + [vx_ref[0, :, col]], axis=0)
            ol = _dot(p, jnp.concatenate([v, jnp.ones_like(v)], axis=1))
            o = ol[:, :C_DH] / (ol[:, C_DH:] + jnp.exp2(sink - m))
            for g in range(C_GROUP):
                o_ref[0, qrow, q0 + g * LANES:q0 + (g + 1) * LANES] = o[g * CBLK:(g + 1) * CBLK].astype(BF16)


def _attn_c_call(sink, qc, kc, vc, with_ctx):
    t_off = 0 if with_ctx else CTX_LEN // CBLK
    nb = N_CBLK - t_off
    assert nb % 2 == 0 and t_off % 2 == 0

    def key_block(d):
        return pl.BlockSpec(
            (1, CBLK, C_KV_COLS),
            lambda b, t, s: (b, jnp.clip(2 * t + t_off + d - 1, 0, N_CBLK - 1), 0))

    keys = [key_block(d) for d in range(4)]
    ctx = pl.BlockSpec((1, CTX_LEN, C_KV_COLS), lambda b, t, s: (b, 0, 0))
    qi = pl.BlockSpec((1, 2 * CBLK, C_WIDTH), lambda b, t, s: (b, t + t_off // 2, 0))
    qo = pl.BlockSpec((1, 2 * CBLK, C_WIDTH), lambda b, t, s: (b, t, 0))
    return pl.pallas_call(
        functools.partial(_attn_c_kernel, t_off=t_off),
        grid_spec=pltpu.PrefetchScalarGridSpec(
            num_scalar_prefetch=1,
            grid=(BATCH, nb // 2),
            in_specs=[qi] + keys + [ctx] + keys + [ctx],
            out_specs=qo,
        ),
        out_shape=jax.ShapeDtypeStruct((BATCH, nb * CBLK, C_WIDTH), BF16),
        compiler_params=_cparams("arbitrary", "arbitrary"),
        name="attn_window",
    )(sink, qc, *([kc] * 5), *([vc] * 5))


HALO = 16


def _pack_bf16_pairs(h):
    hb = h.astype(BF16).astype(F32)
    half = h.shape[-1] // 2
    lo = lax.bitcast_convert_type(hb[:, :half], U32)
    hi = lax.bitcast_convert_type(hb[:, half:], U32)
    return (hi & jnp.uint32(0xFFFF0000)) | (lo >> 16)


def _unpack_bf16_pairs(p):
    lo = lax.bitcast_convert_type(p << 16, F32)
    hi = lax.bitcast_convert_type(p & jnp.uint32(0xFFFF0000), F32)
    return lo, hi


def _out_kernel(*refs, t_off, route):
    if route:
        x_ref = refs[0]
    else:
        ctx_ref, x_ref = refs[:2]
        refs = refs[1:]
    (oa_ref, bb_ref, u_ref, up_ref, un_ref, oc_ref, w_ref, ga_ref, bg_ref, cg_ref, cw_ref,
     gf_ref, scf_ref, shf_ref) = refs[1:15]
    refs = refs[15:]
    if route:
        wrh_ref, wrl_ref, xo_ref, h2_ref, lg_ref = refs
    else:
        xo_ref, h2_ref = refs
    t = pl.program_id(1) + t_off

    u = u_ref[0].astype(F32)
    row = lax.broadcasted_iota(jnp.int32, (TM, 1), 0)
    prev_ok = jnp.where(t >= 2, 1.0, 0.0)
    next_ok = jnp.where((t >= 1) & (t < N_TILES - 1), 1.0, 0.0)
    prev_row = up_ref[0, HALO - 1:HALO, :].astype(F32) * prev_ok
    next_row = un_ref[0, 0:1, :].astype(F32) * next_ok
    u_prev = jnp.where(row == 0, prev_row, pltpu.roll(u, 1, 0))
    u_next = jnp.where(row == TM - 1, next_row, pltpu.roll(u, TM - 1, 0))
    conv = u_prev * cw_ref[0:1, :] + u * cw_ref[1:2, :] + u_next * cw_ref[2:3, :]
    yb = _rms(bb_ref[0].astype(F32) * conv, B_WIDTH) * bg_ref[...]
    yc = _rms(oc_ref[0].astype(F32), C_WIDTH) * cg_ref[...]
    y = (_dot(oa_ref[0], w_ref[0:A_WIDTH, :])
         + _dot(yb.astype(BF16), w_ref[A_WIDTH:A_WIDTH + B_WIDTH, :])
         + _dot(yc.astype(BF16), w_ref[A_WIDTH + B_WIDTH:, :]))
    x_in = x_ref[0] if route else jnp.where(t == 0, ctx_ref[0], x_ref[0])
    x = x_in + ga_ref[0] * y
    xo_ref[0] = x
    h2 = _rms(x, D_MODEL) * gf_ref[...]
    h2 = h2 * (1.0 + scf_ref[0]) + shf_ref[0]
    if route:
        h2_ref[0] = _pack_bf16_pairs(h2)
        h_hi = h2.astype(BF16)
        h_lo = (h2 - h_hi.astype(F32)).astype(BF16)
        lg_ref[0] = _dot(h_hi, wrh_ref[...]) + (_dot(h_lo, wrh_ref[...]) + _dot(h_hi, wrl_ref[...]))
    else:
        h2_ref[0] = h2.astype(BF16)


def _out_call(x, oa, bb, u, oc, w, ga, bg, cg, cw, gfn, scf, shf, wr):
    route = wr is not None
    t_off = 1 if route else 0
    nt = N_TILES - t_off
    hpt = TM // HALO
    tile = lambda width: pl.BlockSpec((1, TM, width), lambda b, t: (b, t + t_off, 0))
    otile = lambda width: pl.BlockSpec((1, TM, width), lambda b, t: (b, t, 0))
    modv = pl.BlockSpec((1, 1, D_MODEL), lambda b, t: (_mod_row(b, t + t_off), 0, 0))
    full2 = lambda a: pl.BlockSpec(a.shape, lambda b, t: (0, 0))
    halo_p = pl.BlockSpec((1, HALO, B_WIDTH), lambda b, t: (b, jnp.maximum((t + t_off) * hpt - 1, 0), 0))
    halo_n = pl.BlockSpec((1, HALO, B_WIDTH),
                          lambda b, t: (b, jnp.minimum((t + t_off + 1) * hpt, N_TOK // HALO - 1), 0))
    x_specs, x_args = ([tile(D_MODEL)], [x]) if route else (_split_specs(), list(x))
    in_specs = x_specs + [otile(A_WIDTH), tile(B_WIDTH), tile(B_WIDTH), halo_p, halo_n, otile(C_WIDTH),
                          pl.BlockSpec(w.shape, lambda b, t: (0, 0), pipeline_mode=pl.Buffered(1)),
                          modv, full2(bg), full2(cg), full2(cw), full2(gfn), modv, modv]
    args = x_args + [oa, bb, u, u, u, oc, w, ga, bg, cg, cw, gfn, scf, shf]
    rows_out = nt * TM
    if route:
        wr_hi = wr.astype(BF16)
        wr_lo = (wr - wr_hi.astype(F32)).astype(BF16)
        in_specs += [full2(wr_hi), full2(wr_lo)]
        args += [wr_hi, wr_lo]
        out_specs = [otile(D_MODEL), otile(D_MODEL // 2), otile(LANES)]
        out_shape = [jax.ShapeDtypeStruct((BATCH, rows_out, D_MODEL), F32),
                     jax.ShapeDtypeStruct((BATCH, rows_out, D_MODEL // 2), U32),
                     jax.ShapeDtypeStruct((BATCH, rows_out, LANES), F32)]
    else:
        out_specs = [otile(D_MODEL), otile(D_MODEL)]
        out_shape = [jax.ShapeDtypeStruct((BATCH, rows_out, D_MODEL), F32),
                     jax.ShapeDtypeStruct((BATCH, rows_out, D_MODEL), BF16)]
    return pl.pallas_call(
        functools.partial(_out_kernel, t_off=t_off, route=route),
        grid=(BATCH, nt),
        in_specs=in_specs,
        out_specs=out_specs,
        out_shape=out_shape,
        compiler_params=_cparams("arbitrary", "arbitrary"),
        name="mix_out",
    )(*args)


def _swiglu_accumulate(h_ref, acc_ref, row_blocks, w1, w3, w2, packed=False):
    for rows in row_blocks:
        if packed:
            x_lo, x_hi = _unpack_bf16_pairs(h_ref[rows, :])
            h = jnp.concatenate([x_lo.astype(BF16), x_hi.astype(BF16)], axis=1)
        else:
            h = h_ref[rows, :]
        a = _dot(h, w1)
        b = _dot(h, w3)
        z = a / (1.0 + jnp.exp(-a)) * b
        acc_ref[rows, :] += _dot(z.astype(BF16), w2)


def _ffn_kernel(h_ref, w1_ref, w3_ref, w2_ref, y_ref, acc_ref):
    f = pl.program_id(1)

    @pl.when(f == 0)
    def _():
        acc_ref[...] = jnp.zeros(acc_ref.shape, F32)

    sub = FFN_TM // FFN_SPLIT
    _swiglu_accumulate(h_ref, acc_ref, [slice(i * sub, (i + 1) * sub) for i in range(FFN_SPLIT)],
                       w1_ref[...], w3_ref[...], w2_ref[...])

    @pl.when(f == pl.num_programs(1) - 1)
    def _():
        y_ref[...] = acc_ref[...].astype(BF16)


def _ffn_call(h, w1, w3, w2):
    rows = h.shape[0]
    return pl.pallas_call(
        _ffn_kernel,
        grid=(rows // FFN_TM, D_FF // FFN_TF),
        in_specs=[
            pl.BlockSpec((FFN_TM, D_MODEL), lambda i, f: (i, 0)),
            pl.BlockSpec((D_MODEL, FFN_TF), lambda i, f: (0, f)),
            pl.BlockSpec((D_MODEL, FFN_TF), lambda i, f: (0, f)),
            pl.BlockSpec((FFN_TF, D_MODEL), lambda i, f: (f, 0)),
        ],
        out_specs=pl.BlockSpec((FFN_TM, D_MODEL), lambda i, f: (i, 0)),
        out_shape=jax.ShapeDtypeStruct((rows, D_MODEL), BF16),
        scratch_shapes=[pltpu.VMEM((FFN_TM, D_MODEL), F32)],
        compiler_params=_cparams("arbitrary", "arbitrary"),
        name="ffn_dense",
    )(h, w1, w3, w2)


def _route_kernel(lg_ref, meta_ref, cnt_ref, carry_sc):
    @pl.when(pl.program_id(0) == 0)
    def _():
        carry_sc[...] = jnp.zeros(carry_sc.shape, F32)

    lane = lax.broadcasted_iota(jnp.int32, (ROUTE_TM, LANES), 1)
    lane_f = lane.astype(F32)
    lg = jnp.where(lane < N_EXPERTS, lg_ref[...], -jnp.inf)
    m1 = jnp.max(lg, axis=-1, keepdims=True)
    i1 = jnp.min(jnp.where(lg == m1, lane_f, float(LANES)), axis=-1, keepdims=True)
    oh1 = lane_f == i1
    lg2 = jnp.where(oh1, -jnp.inf, lg)
    m2 = jnp.max(lg2, axis=-1, keepdims=True)
    i2 = jnp.min(jnp.where(lg2 == m2, lane_f, float(LANES)), axis=-1, keepdims=True)
    oh2 = lane_f == i2
    e = jnp.exp(m2 - m1)
    g1 = 1.0 / (1.0 + e)
    g2 = e / (1.0 + e)
    oh = jnp.where(oh1, 1.0, 0.0) + jnp.where(oh2, 1.0, 0.0)
    r = lax.broadcasted_iota(jnp.int32, (ROUTE_TM, ROUTE_TM), 0)
    c = lax.broadcasted_iota(jnp.int32, (ROUTE_TM, ROUTE_TM), 1)
    tri = jnp.where(c < r, 1.0, 0.0).astype(BF16)
    before = _dot(tri, oh.astype(BF16)) + carry_sc[...]
    rank1 = jnp.sum(jnp.where(oh1, before, 0.0), axis=-1, keepdims=True)
    rank2 = jnp.sum(jnp.where(oh2, before, 0.0), axis=-1, keepdims=True)
    meta = jnp.zeros((ROUTE_TM, LANES), F32)
    for k, val in enumerate((i1, i2, g1, g2, rank1, rank2)):
        meta = jnp.where(lane == k, val, meta)
    meta_ref[...] = meta
    carry_sc[...] += jnp.sum(oh, axis=0, keepdims=True)
    cnt_ref[...] = carry_sc[...]


def _route_call(logits):
    rows = logits.shape[0]
    return pl.pallas_call(
        _route_kernel,
        grid=(rows // ROUTE_TM,),
        in_specs=[pl.BlockSpec((ROUTE_TM, LANES), lambda i: (i, 0))],
        out_specs=[pl.BlockSpec((ROUTE_TM, LANES), lambda i: (i, 0)),
                   pl.BlockSpec((1, LANES), lambda i: (0, 0))],
        out_shape=[jax.ShapeDtypeStruct((rows, LANES), F32), jax.ShapeDtypeStruct((1, LANES), F32)],
        scratch_shapes=[pltpu.VMEM((1, LANES), F32)],
        compiler_params=_cparams("arbitrary"),
        name="moe_route",
    )(logits)


INV_TM = 512


def _invert_kernel(pos_ref, zeros_ref, inv_ref, sem):
    @pl.when(pl.program_id(0) == 0)
    def _():
        fill = pltpu.make_async_copy(zeros_ref, inv_ref, sem)
        fill.start()
        fill.wait()

    base = pl.program_id(0) * INV_TM

    def scatter(r, carry):
        inv_ref[pos_ref[base + r]] = (base + r) >> 1
        return carry
    lax.fori_loop(0, INV_TM, scatter, 0, unroll=8)


def _invert_call(pos):
    return pl.pallas_call(
        _invert_kernel,
        grid_spec=pltpu.PrefetchScalarGridSpec(
            num_scalar_prefetch=1,
            grid=(pos.shape[0] // INV_TM,),
            in_specs=[pl.BlockSpec(memory_space=pl.ANY)],
            out_specs=pl.BlockSpec(memory_space=pltpu.SMEM),
            scratch_shapes=[pltpu.SemaphoreType.DMA(())],
        ),
        out_shape=jax.ShapeDtypeStruct((MOE_ROWS,), jnp.int32),
        compiler_params=_cparams("arbitrary"),
        name="moe_invert",
    )(pos, jnp.zeros((MOE_ROWS,), jnp.int32))


def _moe_kernel(inv_ref, ps_ref, pn_ref, pd_ref, h_ref, w1_ref, w3_ref, w2_ref, y_ref,
                xg_sc, acc_sc, st_sc, gsem, osem):
    p = pl.program_id(0)
    f = pl.program_id(1)
    n_f = pl.num_programs(1)
    start = ps_ref[p]
    n = pn_ref[p]
    subs = [slice(i * MOE_SUB, (i + 1) * MOE_SUB) for i in range(MOE_NSUB)]

    def row_copy(src_row, r):
        return pltpu.make_async_copy(h_ref.at[pl.ds(src_row, 1)], xg_sc.at[pl.ds(r, 1)], gsem)

    @pl.when((p == 0) & (f == 0))
    def _():
        st_sc[0] = jnp.zeros(st_sc.shape[1:], U32)
        tail = [pltpu.make_async_copy(st_sc.at[0], y_ref.at[pl.ds(r0, MOE_SUB)], osem.at[0])
                for r0 in range(2 * BATCH * SEQ, MOE_ROWS, MOE_SUB)]
        for copy in tail:
            copy.start()
        for copy in tail:
            copy.wait()

    @pl.when(f == 0)
    def _():
        def per_sub(fn):
            def body(sub, carry):
                lax.fori_loop(0, MOE_SUB, functools.partial(fn, sub * MOE_SUB), 0, unroll=8)
                return carry
            lax.fori_loop(0, n, body, 0)

        def issue(r0, r, carry):
            row_copy(inv_ref[start * MOE_SUB + r0 + r], r0 + r).start()
            return carry

        def wait(r0, r, carry):
            row_copy(0, r0 + r).wait()
            return carry

        per_sub(issue)
        for sub, rows in enumerate(subs):
            @pl.when(sub < n)
            def _():
                acc_sc[rows, :] = jnp.zeros((MOE_SUB, D_MODEL), F32)
        per_sub(wait)

    def weights():
        return w1_ref[0, 0], w3_ref[0, 0], w2_ref[0, 0]

    def accumulate(row_blocks, w):
        _swiglu_accumulate(xg_sc, acc_sc, row_blocks, *w, packed=True)

    @pl.when(n >= MOE_FAST)
    def _():
        w = weights()
        fast_rows = MOE_FAST * MOE_SUB
        accumulate([slice(r, r + MOE_FAST_ROWS) for r in range(0, fast_rows, MOE_FAST_ROWS)], w)
        for sub in range(MOE_FAST, MOE_NSUB):
            @pl.when(sub < n)
            def _():
                accumulate([subs[sub]], w)

    @pl.when((n > 0) & (n < MOE_FAST))
    def _():
        w = weights()
        for sub in range(MOE_FAST):
            @pl.when(sub < n)
            def _():
                accumulate([subs[sub]], w)

    @pl.when(f == n_f - 1)
    def _():
        def out_copy(sub):
            row0 = pl.multiple_of((start + sub) * MOE_SUB, MOE_SUB)
            return pltpu.make_async_copy(st_sc.at[sub % 2], y_ref.at[pl.ds(row0, MOE_SUB)], osem.at[sub % 2])

        for sub, rows in enumerate(subs):
            @pl.when(sub < n)
            def _():
                if sub >= 2:
                    out_copy(sub - 2).wait()
                st_sc[sub % 2] = _pack_bf16_pairs(acc_sc[rows, :])
                out_copy(sub).start()

        for sub in range(MOE_NSUB):
            @pl.when((sub < n) & (sub + 2 >= n))
            def _():
                out_copy(sub).wait()


def _moe_call(inv, pair_tables, h_packed, w1, w3, w2):
    n_f = D_FF_EXPERT // MOE_TF

    def f_eff(p, f, pn):
        return jnp.where(pn[p] > 0, f, n_f - 1)

    w13 = pl.BlockSpec((1, 1, D_MODEL, MOE_TF), lambda p, f, inv, ps, pn, pd: (0, pd[p], 0, f_eff(p, f, pn)))
    w2s = pl.BlockSpec((1, 1, MOE_TF, D_MODEL), lambda p, f, inv, ps, pn, pd: (0, pd[p], f_eff(p, f, pn), 0))
    hbm = pl.BlockSpec(memory_space=pl.ANY)
    win = MOE_NSUB * MOE_SUB
    n_prefetch = 4
    return pl.pallas_call(
        _moe_kernel,
        grid_spec=pltpu.PrefetchScalarGridSpec(
            num_scalar_prefetch=n_prefetch,
            grid=(MOE_PAIRS, n_f),
            in_specs=[hbm, w13, w13, w2s],
            out_specs=hbm,
            scratch_shapes=[pltpu.VMEM((win, D_MODEL // 2), U32),
                            pltpu.VMEM((win, D_MODEL), F32),
                            pltpu.VMEM((2, MOE_SUB, D_MODEL // 2), U32),
                            pltpu.SemaphoreType.DMA(()),
                            pltpu.SemaphoreType.DMA((2,))],
        ),
        out_shape=jax.ShapeDtypeStruct((MOE_ROWS, D_MODEL // 2), U32),
        compiler_params=_cparams("arbitrary", "arbitrary"),
        name="moe_experts",
    )(inv, *pair_tables, h_packed, w1, w3, w2)


def _pair_schedule(counts):
    nsub = (counts + MOE_SUB - 1) // MOE_SUB
    seg_start = jnp.cumsum(nsub) - nsub
    nchunk = (nsub + MOE_NSUB - 1) // MOE_NSUB
    chunk_end = jnp.cumsum(nchunk)
    idx = jnp.arange(MOE_PAIRS, dtype=jnp.int32)
    expert = jnp.sum(chunk_end[None, :] <= idx[:, None], axis=1).astype(jnp.int32)
    active = expert < N_EXPERTS
    onehot = expert[:, None] == jnp.arange(N_EXPERTS, dtype=jnp.int32)[None, :]
    pick = lambda a: jnp.sum(jnp.where(onehot, a[None, :], 0), axis=1)
    c = idx - (pick(chunk_end) - pick(nchunk))
    first = jnp.where(active, pick(seg_start) + MOE_NSUB * c, 0)
    count = jnp.where(active, jnp.minimum(MOE_NSUB, pick(nsub) - MOE_NSUB * c), 0)
    dma_expert = jnp.where(active, expert, jnp.max(jnp.where(active, expert, 0)))
    i32 = lambda a: a.astype(jnp.int32)
    return seg_start * MOE_SUB, tuple(i32(a) for a in (first, count, dma_expert))


def _combine_kernel(pos_ref, x_ref, meta_ref, gf_ref, ys_ref, o_ref, buf, sem):
    base = pl.program_id(0) * COMB_TM

    def copy(r, k, src_row):
        return pltpu.make_async_copy(ys_ref.at[pl.ds(src_row, 1)], buf.at[k, pl.ds(r, 1)], sem)

    def start(r, carry):
        for k in range(2):
            copy(r, k, pos_ref[2 * (base + r) + k]).start()
        return carry

    lax.fori_loop(0, COMB_TM, start, 0, unroll=8)

    def wait(r, carry):
        for k in range(2):
            copy(r, k, 0).wait()
        return carry

    lax.fori_loop(0, COMB_TM, wait, 0, unroll=8)

    half = D_MODEL // 2
    lo0, hi0 = _unpack_bf16_pairs(buf[0])
    lo1, hi1 = _unpack_bf16_pairs(buf[1])
    g1 = meta_ref[:, 2:3]
    g2 = meta_ref[:, 3:4]
    gf = gf_ref[0]
    o_ref[:, :half] = x_ref[:, :half] + gf[:, :half] * (g1 * lo0 + g2 * lo1)
    o_ref[:, half:] = x_ref[:, half:] + gf[:, half:] * (g1 * hi0 + g2 * hi1)


def _combine_call(pos, x, meta, gf, ys):
    rows = x.shape[0]
    tiles_per_batch = SEQ // COMB_TM
    return pl.pallas_call(
        _combine_kernel,
        grid_spec=pltpu.PrefetchScalarGridSpec(
            num_scalar_prefetch=1,
            grid=(rows // COMB_TM,),
            in_specs=[
                pl.BlockSpec((COMB_TM, D_MODEL), lambda i, p: (i, 0)),
                pl.BlockSpec((COMB_TM, LANES), lambda i, p: (i, 0)),
                pl.BlockSpec((1, 1, D_MODEL), lambda i, p: (i // tiles_per_batch, 0, 0)),
                pl.BlockSpec(memory_space=pl.ANY),
            ],
            out_specs=pl.BlockSpec((COMB_TM, D_MODEL), lambda i, p: (i, 0)),
            scratch_shapes=[pltpu.VMEM((2, COMB_TM, D_MODEL // 2), U32), pltpu.SemaphoreType.DMA(())],
        ),
        out_shape=jax.ShapeDtypeStruct((rows, D_MODEL), F32),
        compiler_params=_cparams("arbitrary"),
        name="moe_combine",
    )(pos, x, meta, gf, ys)


def _moe_layer(x, h_packed, logits, gf, w1, w3, w2):
    meta, cnt = _route_call(logits)
    expert = meta[:, 0:2].astype(jnp.int32)
    rank = meta[:, 4:6].astype(jnp.int32)
    counts = cnt[0, :N_EXPERTS].astype(jnp.int32)
    seg_start, pair_tables = _pair_schedule(counts)
    ids = jnp.arange(N_EXPERTS, dtype=jnp.int32)
    pos = rank + jnp.sum(jnp.where(expert[..., None] == ids, seg_start, 0), axis=-1)
    pos = pos.reshape(-1).astype(jnp.int32)
    ys = _moe_call(_invert_call(pos), pair_tables, h_packed, w1, w3, w2)
    return _combine_call(pos, x, meta, gf, ys)


def _rope_tables(dh):
    rows = SEQ // GRID_W
    row = jnp.repeat(jnp.arange(rows, dtype=F32), GRID_W)
    col = jnp.tile(jnp.arange(GRID_W, dtype=F32), rows)
    half = dh // 2
    inv = ROPE_THETA ** (-jnp.arange(0, half, 2, dtype=F32) / half)
    ar = row[:, None] * inv[None, :]
    ac = col[:, None] * inv[None, :]
    ang = jnp.concatenate([ar, ar, ac, ac], axis=-1)
    ang = jnp.concatenate([jnp.zeros((CTX_LEN, dh), F32), ang], axis=0)
    ang = jnp.tile(ang, (1, LANES // dh))
    odd = ((jnp.arange(LANES) // (dh // 4)) % 2 == 1)[None, :]
    sin = jnp.sin(ang)
    return jnp.cos(ang), jnp.where(odd, sin, 0.0), jnp.where(odd, 0.0, -sin)


def kernel(x, c, ctx, c_ctx, ada_w, ada_b, norm_mix_g, norm_ffn_g, w_in, w_out, a_q_norm, a_k_norm,
           a_lam_q1, a_lam_k1, a_lam_q2, a_lam_k2, a_subln_g, b_conv_w, b_out_g, c_q_norm, c_k_norm,
           c_sink, c_out_g, ffn_w1, ffn_w3, ffn_w2, router_w, moe_w1, moe_w3, moe_w2):
    assert x.shape == (BATCH, SEQ, D_MODEL) and ctx.shape == (BATCH, CTX_LEN, D_MODEL)
    cond = jnp.concatenate([c, c_ctx[None, :], jnp.zeros((8 - BATCH - 1, D_MODEL), F32)], axis=0)
    mod = _ada_call(cond, ada_w, ada_b)
    tables = _rope_tables(A_DK) + _rope_tables(C_DH)
    xs = (ctx, x)
    y_prev = gf_prev = None
    out = None
    moe_bf16 = []
    assert DEPTH == 2 and moe_w1.shape[0] == 1
    for l in range(DEPTH):
        last = l == DEPTH - 1
        sh_a, sc_a, g_a, sh_f, sc_f, g_f = [
            mod[l, :BATCH + 1, k * D_MODEL:(k + 1) * D_MODEL].reshape(BATCH + 1, 1, D_MODEL) for k in range(6)]
        gains = (jnp.tile(a_q_norm[l], LANES // A_DK)[None, :], jnp.tile(a_k_norm[l], LANES // A_DK)[None, :],
                 c_q_norm[l][None, :], c_k_norm[l][None, :])
        side = [] if l else [(ffn_w1[0], None), (ffn_w3[0], None), (ffn_w2[0], None),
                             (w_in, 1), (w_out, 0), (w_out, 1)]
        res = _in_call(xs, y_prev, gf_prev, norm_mix_g[l][None, :], sc_a, sh_a,
                       w_in_bf16[1] if l else w_in[0].astype(BF16), gains, tables, side)
        if y_prev is not None:
            xs, res = res[0], res[1:]
        qa, ka, va, bb, u, qc, kc, vc = res[:8]
        if not l:
            ffn_bf16, w_in_bf16, w_out_bf16 = res[8:11], (None, res[11]), res[12:14]
        lam_init = 0.8 - 0.6 * math.exp(-0.3 * l)
        lam_vecs = [v[l][None, :] for v in (a_lam_q1, a_lam_k1, a_lam_q2, a_lam_k2)]
        cast_jobs = ((moe_w1[0], moe_w2[0]), (moe_w3[0],))[l]
        oa, cast_out = _attn_a_call(qa, ka, va, lam_vecs, a_subln_g[l][None, :], lam_init, with_ctx=not last,
                                    cast=[w.reshape(-1, w.shape[-1]) for w in cast_jobs])
        moe_bf16 += [c.reshape((1,) + w.shape) for c, w in zip(cast_out, cast_jobs)]
        oc = _attn_c_call(c_sink[l], qc, kc, vc, with_ctx=not last)
        wr = None
        if last:
            wr = jnp.pad(router_w[l // 2], ((0, 0), (0, LANES - N_EXPERTS)))
        res = _out_call(xs, oa, bb, u, oc, w_out_bf16[l], g_a, b_out_g[l][None, :],
                        c_out_g[l][None, :], b_conv_w[l], norm_ffn_g[l][None, :], sc_f, sh_f, wr)
        if not last:
            xs, h2 = res
            y = _ffn_call(h2.reshape(BATCH * N_TOK, D_MODEL), *ffn_bf16)
            y_prev, gf_prev = y.reshape(BATCH, N_TOK, D_MODEL), g_f
        else:
            x_lat, h_packed, logits = res
            w1b, w2b, w3b = moe_bf16
            out = _moe_layer(x_lat.reshape(BATCH * SEQ, D_MODEL), h_packed.reshape(BATCH * SEQ, D_MODEL // 2),
                             logits.reshape(BATCH * SEQ, LANES), g_f[:BATCH], w1b, w3b, w2b)
    return out.reshape(BATCH, SEQ, D_MODEL)
```

```python
import functools
import math

import jax
import jax.numpy as jnp
from jax import lax
from jax.experimental import pallas as pl
from jax.experimental.pallas import tpu as pltpu

F32 = jnp.float32
BF16 = jnp.bfloat16
U32 = jnp.uint32

D_MODEL = 2048
BATCH = 2
SEQ = 4096
DEPTH = 2
GRID_W = 64
CTX_LEN = 256
EPS = 1e-6
ROPE_THETA = 10000.0

A_HEADS = 4
A_DK = 64
A_DV = 128
A_WIDTH = A_HEADS * A_DV
B_WIDTH = 512
C_HEADS = 8
C_KV_HEADS = 2
C_GROUP = C_HEADS // C_KV_HEADS
C_DH = 128
C_WIDTH = C_HEADS * C_DH
C_KV_COLS = C_KV_HEADS * C_DH
WINDOW = 128
MIX_WIDTH = A_WIDTH + B_WIDTH + C_WIDTH
IN_COLS = 3 * A_WIDTH + 3 * B_WIDTH + C_WIDTH + 2 * C_KV_COLS
D_FF = 5632
N_EXPERTS = 8
D_FF_EXPERT = 7168

COL_AQ, COL_AK, COL_AV = 0, 512, 1024
COL_BB, COL_BC, COL_BX = 1536, 2048, 2560
COL_CQ, COL_CK, COL_CV = 3072, 4096, 4352

LANES = 128
N_TOK = CTX_LEN + SEQ
TM = 256
N_TILES = N_TOK // TM
IN_TN = 512
CBLK = 128
N_CBLK = N_TOK // CBLK
A_CHUNK = 1024
A_SPLIT = 4
CAST_STEPS = 128
LOG2E = 1.4426950408889634
FFN_TM = N_TOK // 4
FFN_TF = 512
IN_CAST_STEPS = 32
FFN_SPLIT = 4
MOE_SUB = 256
MOE_NSUB = 10
MOE_FAST = 8
MOE_FAST_ROWS = 256
MOE_TF = 512
MOE_ROWS = 2 * BATCH * SEQ + N_EXPERTS * MOE_SUB
MOE_PAIRS = (MOE_ROWS // MOE_SUB + N_EXPERTS * (MOE_NSUB - 1)) // MOE_NSUB
ROUTE_TM = 512
COMB_TM = 256
VMEM_LIMIT = 56 * 1024 * 1024
NEG = -1e30

NT_DIMS = (((1,), (1,)), ((), ()))


def _cparams(*sem):
    return pltpu.CompilerParams(dimension_semantics=sem, vmem_limit_bytes=VMEM_LIMIT)


def _dot(a, b):
    return jnp.dot(a, b, preferred_element_type=F32)


def _rms(x, width):
    return x * lax.rsqrt(jnp.sum(x * x, axis=-1, keepdims=True) * (1.0 / width) + EPS)


def _slab_spec(a, n_steps, step_of, layer=None):
    rows = a.shape[-2] // n_steps
    assert rows * n_steps == a.shape[-2] and rows % 16 == 0
    step = lambda *g: jnp.minimum(step_of(*g), n_steps - 1)
    if layer is None:
        return pl.BlockSpec((rows, a.shape[-1]), lambda *g: (step(*g), 0))
    return pl.BlockSpec((1, rows, a.shape[-1]), lambda *g: (layer, step(*g), 0))


def _cast_slabs(step, n_steps, srcs, dsts):
    @pl.when(step < n_steps)
    def _():
        for src, dst in zip(srcs, dsts):
            dst[...] = (src[0] if len(src.shape) == 3 else src[...]).astype(BF16)


ADA_TN = 1024


def _ada_kernel(c_ref, w_ref, b_ref, o_ref):
    a = c_ref[...]
    a = a / (1.0 + jnp.exp(-a))
    o_ref[0] = _dot(a.astype(BF16), w_ref[0].astype(BF16)) + b_ref[0]


def _ada_call(cond, ada_w, ada_b):
    n = 6 * D_MODEL
    return pl.pallas_call(
        _ada_kernel,
        grid=(DEPTH, n // ADA_TN),
        in_specs=[
            pl.BlockSpec((8, D_MODEL), lambda l, j: (0, 0)),
            pl.BlockSpec((1, D_MODEL, ADA_TN), lambda l, j: (l, 0, j)),
            pl.BlockSpec((1, 1, ADA_TN), lambda l, j: (l, 0, j)),
        ],
        out_specs=pl.BlockSpec((1, 8, ADA_TN), lambda l, j: (l, 0, j)),
        out_shape=jax.ShapeDtypeStruct((DEPTH, 8, n), F32),
        compiler_params=_cparams("arbitrary", "arbitrary"),
        name="ada_mod",
    )(cond, ada_w, ada_b.reshape(DEPTH, 1, n))


def _rope128(n, cos, s_odd, s_even, quarter):
    return n * cos + pltpu.roll(n, quarter, 1) * s_odd + pltpu.roll(n, LANES - quarter, 1) * s_even


def _in_kernel(*refs, with_y, n_cast):
    if with_y:
        x_ref, y_ref, gf_ref = refs[:3]
    else:
        ctx_ref, x_ref = refs[:2]
    refs = refs[3 if with_y else 2:]
    (g_ref, sc_ref, sh_ref, w_ref, aqn_ref, akn_ref, cqn_ref, ckn_ref,
     ca_ref, sa1_ref, sa2_ref, cc_ref, sc1_ref, sc2_ref) = refs[:14]
    cast_src = refs[14:14 + n_cast]
    outs = refs[14 + n_cast:]
    if with_y:
        xo_ref, outs = outs[0], outs[1:]
    qa_ref, ka_ref, va_ref, bb_ref, u_ref, qc_ref, kc_ref, vc_ref = outs[:8]
    _cast_slabs(pl.program_id(0) * pl.num_programs(1) + pl.program_id(1), IN_CAST_STEPS, cast_src, outs[8:])

    if with_y:
        x = x_ref[0] + gf_ref[0] * y_ref[0].astype(F32)
        xo_ref[0] = x
    else:
        x = jnp.where(pl.program_id(1) == 0, ctx_ref[0], x_ref[0])
    h = _rms(x, D_MODEL) * g_ref[...]
    h = h * (1.0 + sc_ref[0]) + sh_ref[0]
    hb = h.astype(BF16)

    lane = lax.broadcasted_iota(jnp.int32, (TM, LANES), 1)
    lo_half = lane < A_DK

    def proj(col0, width):
        return _dot(hb, w_ref[:, col0:col0 + width])

    def head_a(p, gain_ref, scale):
        p2 = p * p
        s_all = jnp.sum(p2, axis=-1, keepdims=True)
        s_lo = jnp.sum(jnp.where(lo_half, p2, 0.0), axis=-1, keepdims=True)
        ms = jnp.where(lo_half, s_lo, s_all - s_lo) * (1.0 / A_DK)
        n = p * lax.rsqrt(ms + EPS) * gain_ref[...]
        r = _rope128(n, ca_ref[...], sa1_ref[...], sa2_ref[...], A_DK // 4)
        return (r * scale).astype(BF16)

    def head_c(p, gain_ref, scale):
        n = _rms(p, C_DH) * gain_ref[...]
        r = _rope128(n, cc_ref[...], sc1_ref[...], sc2_ref[...], C_DH // 4)
        return (r * scale).astype(BF16)

    def heads(o_ref, col0, width, fn, gain_ref, scale):
        for c in range(0, width, IN_TN):
            wd = min(IN_TN, width - c)
            p = proj(col0 + c, wd)
            for j in range(wd // LANES):
                o_ref[0, :, c + j * LANES:c + (j + 1) * LANES] = fn(
                    p[:, j * LANES:(j + 1) * LANES], gain_ref, scale)

    heads(qa_ref, COL_AQ, A_WIDTH, head_a, aqn_ref, A_DK ** -0.5 * LOG2E)
    heads(ka_ref, COL_AK, A_WIDTH, head_a, akn_ref, 1.0)
    va_ref[0] = proj(COL_AV, A_WIDTH).astype(BF16)
    bb_ref[0] = proj(COL_BB, B_WIDTH).astype(BF16)
    u_ref[0] = (proj(COL_BC, B_WIDTH) * proj(COL_BX, B_WIDTH)).astype(BF16)
    heads(qc_ref, COL_CQ, C_WIDTH, head_c, cqn_ref, C_DH ** -0.5 * LOG2E)
    heads(kc_ref, COL_CK, C_KV_COLS, head_c, ckn_ref, 1.0)
    vc_ref[0] = proj(COL_CV, C_KV_COLS).astype(BF16)


def _mod_row(b, t):
    return jnp.where(t == 0, BATCH, b)


def _split_specs():
    return [pl.BlockSpec((1, TM, D_MODEL), lambda b, t: (b, 0, 0)),
            pl.BlockSpec((1, TM, D_MODEL), lambda b, t: (b, jnp.maximum(t - 1, 0), 0))]


def _in_call(x, y, gf, g, sc, sh, w, gains, tables, cast):
    with_y = y is not None
    tile = lambda width: pl.BlockSpec((1, TM, width), lambda b, t: (b, t, 0))
    modv = pl.BlockSpec((1, 1, D_MODEL), lambda b, t: (_mod_row(b, t), 0, 0))
    full2 = lambda a: pl.BlockSpec(a.shape, lambda b, t: (0, 0))
    rows = pl.BlockSpec((TM, LANES), lambda b, t: (t, 0))
    slab = lambda a, layer=None: _slab_spec(a, IN_CAST_STEPS, lambda b, t: b * N_TILES + t, layer)
    if with_y:
        in_specs = [tile(D_MODEL), tile(D_MODEL), modv]
        args = [x, y, gf]
    else:
        in_specs = _split_specs()
        args = list(x)
    in_specs += [full2(g), modv, modv,
                 pl.BlockSpec(w.shape, lambda b, t: (0, 0), pipeline_mode=pl.Buffered(1))]
    args += [g, sc, sh, w]
    in_specs += [full2(a) for a in gains]
    args += list(gains)
    in_specs += [rows] * 6
    args += list(tables)
    in_specs += [slab(a, layer) for a, layer in cast]
    args += [a for a, _ in cast]
    widths = (A_WIDTH, A_WIDTH, A_WIDTH, B_WIDTH, B_WIDTH, C_WIDTH, C_KV_COLS, C_KV_COLS)
    cast_out = [jax.ShapeDtypeStruct(a.shape[-2:], BF16) for a, _ in cast]
    out_specs = [tile(wd) for wd in widths] + [slab(o) for o in cast_out]
    out_shape = [jax.ShapeDtypeStruct((BATCH, N_TOK, wd), BF16) for wd in widths] + cast_out
    if with_y:
        out_specs = [tile(D_MODEL)] + out_specs
        out_shape = [jax.ShapeDtypeStruct((BATCH, N_TOK, D_MODEL), F32)] + out_shape
    assert BATCH * N_TILES >= IN_CAST_STEPS
    return pl.pallas_call(
        functools.partial(_in_kernel, with_y=with_y, n_cast=len(cast)),
        grid=(BATCH, N_TILES),
        in_specs=in_specs,
        out_specs=out_specs,
        out_shape=out_shape,
        compiler_params=_cparams("arbitrary", "arbitrary"),
        name="in_proj",
    )(*args)


def _attn_a_kernel(*refs, lam_init, t_off, n_cast):
    q_ref, k_ref, v_ref, lq1_ref, lk1_ref, lq2_ref, lk2_ref, g_ref = refs[:8]
    cast_src = refs[8:8 + n_cast]
    o_ref = refs[8 + n_cast]
    cast_dst = refs[9 + n_cast:9 + 2 * n_cast]
    m_sc, acc_sc = refs[9 + 2 * n_cast:]
    t = pl.program_id(2) + t_off

    step_idx = (pl.program_id(0) * pl.num_programs(1) + pl.program_id(1)) * pl.num_programs(2) + pl.program_id(2)
    _cast_slabs(step_idx, CAST_STEPS, cast_src, cast_dst)

    q = q_ref[0]
    lane = lax.broadcasted_iota(jnp.int32, (TM, LANES), 1)
    zero = jnp.zeros_like(q)
    qs = jnp.concatenate([jnp.where(lane < A_DK, q, zero), jnp.where(lane >= A_DK, q, zero)], axis=0)

    m_sc[...] = jnp.full(m_sc.shape, NEG, F32)
    acc_sc[...] = jnp.zeros(acc_sc.shape, F32)

    def step(k, v):
        v1 = jnp.concatenate([v, jnp.ones_like(v)], axis=1)
        grp = 2 * TM // A_SPLIT
        for g in range(A_SPLIT):
            rows = slice(g * grp, (g + 1) * grp)
            s = lax.dot_general(qs[rows], k, NT_DIMS, preferred_element_type=F32)
            m_prev = m_sc[rows, :]
            m_new = jnp.maximum(m_prev, jnp.max(s, axis=-1, keepdims=True))
            alpha = jnp.exp2(m_prev - m_new)
            p = jnp.exp2(s - jnp.tile(m_new, (1, s.shape[1] // LANES)))
            acc_sc[rows, :] = jnp.tile(alpha, (1, 2)) * acc_sc[rows, :] + _dot(p.astype(BF16), v1)
            m_sc[rows, :] = m_new

    def keys(bounds):
        for lo, hi in zip(bounds[:-1], bounds[1:]):
            step(k_ref[0, lo:hi, :], v_ref[0, lo:hi, :])

    all_keys = functools.partial(keys, [0] + list(range(CTX_LEN + A_CHUNK, N_TOK + 1, A_CHUNK)))
    if t_off == 0:
        pl.when(t == 0)(functools.partial(keys, [0, CTX_LEN]))
        pl.when(t > 0)(all_keys)
    else:
        all_keys()

    lam = (jnp.exp(jnp.sum(lq1_ref[...] * lk1_ref[...], keepdims=True))
           - jnp.exp(jnp.sum(lq2_ref[...] * lk2_ref[...], keepdims=True)) + lam_init)
    acc = acc_sc[...]
    o = acc[:, :A_DV] / acc[:, A_DV:]
    o = o[:TM] - lam * o[TM:]
    o_ref[0] = (_rms(o, A_DV) * g_ref[...] * (1.0 - lam_init)).astype(BF16)


def _attn_a_call(qa, ka, va, lam_vecs, subln_g, lam_init, with_ctx, cast):
    t_off = 0 if with_ctx else 1
    nt = N_TILES - t_off
    assert BATCH * A_HEADS * nt >= CAST_STEPS
    vec = pl.BlockSpec((1, A_DK), lambda b, h, t: (0, 0))

    def slab(a):
        return _slab_spec(a, CAST_STEPS, lambda b, h, t: (b * A_HEADS + h) * nt + t)

    res = pl.pallas_call(
        functools.partial(_attn_a_kernel, lam_init=lam_init, t_off=t_off, n_cast=len(cast)),
        grid=(BATCH, A_HEADS, nt),
        in_specs=[
            pl.BlockSpec((1, TM, LANES), lambda b, h, t: (b, t + t_off, h)),
            pl.BlockSpec((1, N_TOK, LANES), lambda b, h, t: (b, 0, h)),
            pl.BlockSpec((1, N_TOK, LANES), lambda b, h, t: (b, 0, h)),
            vec, vec, vec, vec,
            pl.BlockSpec((1, A_DV), lambda b, h, t: (0, 0)),
        ] + [slab(a) for a in cast],
        out_specs=[pl.BlockSpec((1, TM, LANES), lambda b, h, t: (b, t, h))] + [slab(a) for a in cast],
        out_shape=[jax.ShapeDtypeStruct((BATCH, nt * TM, A_WIDTH), BF16)]
        + [jax.ShapeDtypeStruct(a.shape, BF16) for a in cast],
        scratch_shapes=[pltpu.VMEM((2 * TM, LANES), F32), pltpu.VMEM((2 * TM, 2 * A_DV), F32)],
        compiler_params=_cparams("arbitrary", "arbitrary", "arbitrary"),
        name="attn_diff",
    )(qa, ka, va, *lam_vecs, subln_g, *cast)
    return res[0], res[1:]


def _attn_c_kernel(sink_ref, q_ref, k0_ref, k1_ref, k2_ref, k3_ref, kx_ref,
                   v0_ref, v1_ref, v2_ref, v3_ref, vx_ref, o_ref, *, t_off):
    k_refs = (k0_ref, k1_ref, k2_ref, k3_ref)
    v_refs = (v0_ref, v1_ref, v2_ref, v3_ref)
    rows = C_GROUP * CBLK
    i = lax.broadcasted_iota(jnp.int32, (rows, CBLK), 0) & (CBLK - 1)
    j = lax.broadcasted_iota(jnp.int32, (rows, CBLK), 1)
    n_last = SEQ // CBLK - 1
    far = 4 * CBLK
    g_row = lax.broadcasted_iota(jnp.int32, (rows, 1), 0) // CBLK
    rmax = lambda s: jnp.max(s, axis=-1, keepdims=True)

    for a in range(2):
        n = 2 * pl.program_id(1) + a + t_off - CTX_LEN // CBLK
        off_p = jnp.where(n >= 1, 0, far)
        off_0 = jnp.where(n >= 0, 0, far)
        off_n = jnp.where((n >= 0) & (n < n_last), 0, far)
        qrow = slice(a * CBLK, (a + 1) * CBLK)
        for kvh in range(C_KV_HEADS):
            col = slice(kvh * C_DH, (kvh + 1) * C_DH)
            q0 = kvh * C_GROUP * C_DH
            q4 = jnp.concatenate(
                [q_ref[0, qrow, q0 + g * LANES:q0 + (g + 1) * LANES] for g in range(C_GROUP)], axis=0)
            k_loc = jnp.concatenate([r[0, :, col] for r in k_refs[a:a + 3]], axis=0)
            s_loc = lax.dot_general(q4, k_loc, NT_DIMS, preferred_element_type=F32)
            s_p = jnp.where(j >= i + off_p, s_loc[:, :CBLK], NEG)
            s_0 = jnp.where(j >= off_0, s_loc[:, CBLK:2 * CBLK], NEG)
            s_n = jnp.where(j + off_n <= i, s_loc[:, 2 * CBLK:], NEG)
            s_x = lax.dot_general(q4, kx_ref[0, :, col], NT_DIMS, preferred_element_type=F32)

            sink = jnp.zeros((rows, 1), F32)
            for g in range(C_GROUP):
                sink = jnp.where(g_row == g, sink_ref[kvh * C_GROUP + g] * LOG2E, sink)

            m = jnp.maximum(jnp.maximum(jnp.maximum(rmax(s_p), rmax(s_0)), jnp.maximum(rmax(s_n), rmax(s_x))),
                            sink)
            p = jnp.concatenate([jnp.exp2(s - m) for s in (s_p, s_0, s_n, s_x)], axis=1).astype(BF16)
            v = jnp.concatenate([r[0, :, col] for r in v_refs[a:a + 3]] + [vx_ref[0, :, col]], axis=0)
            ol = _dot(p, jnp.concatenate([v, jnp.ones_like(v)], axis=1))
            o = ol[:, :C_DH] / (ol[:, C_DH:] + jnp.exp2(sink - m))
            for g in range(C_GROUP):
                o_ref[0, qrow, q0 + g * LANES:q0 + (g + 1) * LANES] = o[g * CBLK:(g + 1) * CBLK].astype(BF16)


def _attn_c_call(sink, qc, kc, vc, with_ctx):
    t_off = 0 if with_ctx else CTX_LEN // CBLK
    nb = N_CBLK - t_off
    assert nb % 2 == 0 and t_off % 2 == 0

    def key_block(d):
        return pl.BlockSpec(
            (1, CBLK, C_KV_COLS),
            lambda b, t, s: (b, jnp.clip(2 * t + t_off + d - 1, 0, N_CBLK - 1), 0))

    keys = [key_block(d) for d in range(4)]
    ctx = pl.BlockSpec((1, CTX_LEN, C_KV_COLS), lambda b, t, s: (b, 0, 0))
    qi = pl.BlockSpec((1, 2 * CBLK, C_WIDTH), lambda b, t, s: (b, t + t_off // 2, 0))
    qo = pl.BlockSpec((1, 2 * CBLK, C_WIDTH), lambda b, t, s: (b, t, 0))
    return pl.pallas_call(
        functools.partial(_attn_c_kernel, t_off=t_off),
        grid_spec=pltpu.PrefetchScalarGridSpec(
            num_scalar_prefetch=1,
            grid=(BATCH, nb // 2),
            in_specs=[qi] + keys + [ctx] + keys + [ctx],
            out_specs=qo,
        ),
        out_shape=jax.ShapeDtypeStruct((BATCH, nb * CBLK, C_WIDTH), BF16),
        compiler_params=_cparams("arbitrary", "arbitrary"),
        name="attn_window",
    )(sink, qc, *([kc] * 5), *([vc] * 5))


HALO = 16


def _pack_bf16_pairs(h):
    hb = h.astype(BF16).astype(F32)
    half = h.shape[-1] // 2
    lo = lax.bitcast_convert_type(hb[:, :half], U32)
    hi = lax.bitcast_convert_type(hb[:, half:], U32)
    return (hi & jnp.uint32(0xFFFF0000)) | (lo >> 16)


def _unpack_bf16_pairs(p):
    lo = lax.bitcast_convert_type(p << 16, F32)
    hi = lax.bitcast_convert_type(p & jnp.uint32(0xFFFF0000), F32)
    return lo, hi


def _out_kernel(*refs, t_off, route):
    if route:
        x_ref = refs[0]
    else:
        ctx_ref, x_ref = refs[:2]
        refs = refs[1:]
    (oa_ref, bb_ref, u_ref, up_ref, un_ref, oc_ref, w_ref, ga_ref, bg_ref, cg_ref, cw_ref,
     gf_ref, scf_ref, shf_ref) = refs[1:15]
    refs = refs[15:]
    if route:
        wrh_ref, wrl_ref, xo_ref, h2_ref, lg_ref = refs
    else:
        xo_ref, h2_ref = refs
    t = pl.program_id(1) + t_off

    u = u_ref[0].astype(F32)
    row = lax.broadcasted_iota(jnp.int32, (TM, 1), 0)
    prev_ok = jnp.where(t >= 2, 1.0, 0.0)
    next_ok = jnp.where((t >= 1) & (t < N_TILES - 1), 1.0, 0.0)
    prev_row = up_ref[0, HALO - 1:HALO, :].astype(F32) * prev_ok
    next_row = un_ref[0, 0:1, :].astype(F32) * next_ok
    u_prev = jnp.where(row == 0, prev_row, pltpu.roll(u, 1, 0))
    u_next = jnp.where(row == TM - 1, next_row, pltpu.roll(u, TM - 1, 0))
    conv = u_prev * cw_ref[0:1, :] + u * cw_ref[1:2, :] + u_next * cw_ref[2:3, :]
    yb = _rms(bb_ref[0].astype(F32) * conv, B_WIDTH) * bg_ref[...]
    yc = _rms(oc_ref[0].astype(F32), C_WIDTH) * cg_ref[...]
    y = (_dot(oa_ref[0], w_ref[0:A_WIDTH, :])
         + _dot(yb.astype(BF16), w_ref[A_WIDTH:A_WIDTH + B_WIDTH, :])
         + _dot(yc.astype(BF16), w_ref[A_WIDTH + B_WIDTH:, :]))
    x_in = x_ref[0] if route else jnp.where(t == 0, ctx_ref[0], x_ref[0])
    x = x_in + ga_ref[0] * y
    xo_ref[0] = x
    h2 = _rms(x, D_MODEL) * gf_ref[...]
    h2 = h2 * (1.0 + scf_ref[0]) + shf_ref[0]
    if route:
        h2_ref[0] = _pack_bf16_pairs(h2)
        h_hi = h2.astype(BF16)
        h_lo = (h2 - h_hi.astype(F32)).astype(BF16)
        lg_ref[0] = _dot(h_hi, wrh_ref[...]) + (_dot(h_lo, wrh_ref[...]) + _dot(h_hi, wrl_ref[...]))
    else:
        h2_ref[0] = h2.astype(BF16)


def _out_call(x, oa, bb, u, oc, w, ga, bg, cg, cw, gfn, scf, shf, wr):
    route = wr is not None
    t_off = 1 if route else 0
    nt = N_TILES - t_off
    hpt = TM // HALO
    tile = lambda width: pl.BlockSpec((1, TM, width), lambda b, t: (b, t + t_off, 0))
    otile = lambda width: pl.BlockSpec((1, TM, width), lambda b, t: (b, t, 0))
    modv = pl.BlockSpec((1, 1, D_MODEL), lambda b, t: (_mod_row(b, t + t_off), 0, 0))
    full2 = lambda a: pl.BlockSpec(a.shape, lambda b, t: (0, 0))
    halo_p = pl.BlockSpec((1, HALO, B_WIDTH), lambda b, t: (b, jnp.maximum((t + t_off) * hpt - 1, 0), 0))
    halo_n = pl.BlockSpec((1, HALO, B_WIDTH),
                          lambda b, t: (b, jnp.minimum((t + t_off + 1) * hpt, N_TOK // HALO - 1), 0))
    x_specs, x_args = ([tile(D_MODEL)], [x]) if route else (_split_specs(), list(x))
    in_specs = x_specs + [otile(A_WIDTH), tile(B_WIDTH), tile(B_WIDTH), halo_p, halo_n, otile(C_WIDTH),
                          pl.BlockSpec(w.shape, lambda b, t: (0, 0), pipeline_mode=pl.Buffered(1)),
                          modv, full2(bg), full2(cg), full2(cw), full2(gfn), modv, modv]
    args = x_args + [oa, bb, u, u, u, oc, w, ga, bg, cg, cw, gfn, scf, shf]
    rows_out = nt * TM
    if route:
        wr_hi = wr.astype(BF16)
        wr_lo = (wr - wr_hi.astype(F32)).astype(BF16)
        in_specs += [full2(wr_hi), full2(wr_lo)]
        args += [wr_hi, wr_lo]
        out_specs = [otile(D_MODEL), otile(D_MODEL // 2), otile(LANES)]
        out_shape = [jax.ShapeDtypeStruct((BATCH, rows_out, D_MODEL), F32),
                     jax.ShapeDtypeStruct((BATCH, rows_out, D_MODEL // 2), U32),
                     jax.ShapeDtypeStruct((BATCH, rows_out, LANES), F32)]
    else:
        out_specs = [otile(D_MODEL), otile(D_MODEL)]
        out_shape = [jax.ShapeDtypeStruct((BATCH, rows_out, D_MODEL), F32),
                     jax.ShapeDtypeStruct((BATCH, rows_out, D_MODEL), BF16)]
    return pl.pallas_call(
        functools.partial(_out_kernel, t_off=t_off, route=route),
        grid=(BATCH, nt),
        in_specs=in_specs,
        out_specs=out_specs,
        out_shape=out_shape,
        compiler_params=_cparams("arbitrary", "arbitrary"),
        name="mix_out",
    )(*args)


def _swiglu_accumulate(h_ref, acc_ref, row_blocks, w1, w3, w2, packed=False):
    for rows in row_blocks:
        if packed:
            x_lo, x_hi = _unpack_bf16_pairs(h_ref[rows, :])
            h = jnp.concatenate([x_lo.astype(BF16), x_hi.astype(BF16)], axis=1)
        else:
            h = h_ref[rows, :]
        a = _dot(h, w1)
        b = _dot(h, w3)
        z = a / (1.0 + jnp.exp(-a)) * b
        acc_ref[rows, :] += _dot(z.astype(BF16), w2)


def _ffn_kernel(h_ref, w1_ref, w3_ref, w2_ref, y_ref, acc_ref):
    f = pl.program_id(1)

    @pl.when(f == 0)
    def _():
        acc_ref[...] = jnp.zeros(acc_ref.shape, F32)

    sub = FFN_TM // FFN_SPLIT
    _swiglu_accumulate(h_ref, acc_ref, [slice(i * sub, (i + 1) * sub) for i in range(FFN_SPLIT)],
                       w1_ref[...], w3_ref[...], w2_ref[...])

    @pl.when(f == pl.num_programs(1) - 1)
    def _():
        y_ref[...] = acc_ref[...].astype(BF16)


def _ffn_call(h, w1, w3, w2):
    rows = h.shape[0]
    return pl.pallas_call(
        _ffn_kernel,
        grid=(rows // FFN_TM, D_FF // FFN_TF),
        in_specs=[
            pl.BlockSpec((FFN_TM, D_MODEL), lambda i, f: (i, 0)),
            pl.BlockSpec((D_MODEL, FFN_TF), lambda i, f: (0, f)),
            pl.BlockSpec((D_MODEL, FFN_TF), lambda i, f: (0, f)),
            pl.BlockSpec((FFN_TF, D_MODEL), lambda i, f: (f, 0)),
        ],
        out_specs=pl.BlockSpec((FFN_TM, D_MODEL), lambda i, f: (i, 0)),
        out_shape=jax.ShapeDtypeStruct((rows, D_MODEL), BF16),
        scratch_shapes=[pltpu.VMEM((FFN_TM, D_MODEL), F32)],
        compiler_params=_cparams("arbitrary", "arbitrary"),
        name="ffn_dense",
    )(h, w1, w3, w2)


def _route_kernel(lg_ref, meta_ref, cnt_ref, carry_sc):
    @pl.when(pl.program_id(0) == 0)
    def _():
        carry_sc[...] = jnp.zeros(carry_sc.shape, F32)

    lane = lax.broadcasted_iota(jnp.int32, (ROUTE_TM, LANES), 1)
    lane_f = lane.astype(F32)
    lg = jnp.where(lane < N_EXPERTS, lg_ref[...], -jnp.inf)
    m1 = jnp.max(lg, axis=-1, keepdims=True)
    i1 = jnp.min(jnp.where(lg == m1, lane_f, float(LANES)), axis=-1, keepdims=True)
    oh1 = lane_f == i1
    lg2 = jnp.where(oh1, -jnp.inf, lg)
    m2 = jnp.max(lg2, axis=-1, keepdims=True)
    i2 = jnp.min(jnp.where(lg2 == m2, lane_f, float(LANES)), axis=-1, keepdims=True)
    oh2 = lane_f == i2
    e = jnp.exp(m2 - m1)
    g1 = 1.0 / (1.0 + e)
    g2 = e / (1.0 + e)
    oh = jnp.where(oh1, 1.0, 0.0) + jnp.where(oh2, 1.0, 0.0)
    r = lax.broadcasted_iota(jnp.int32, (ROUTE_TM, ROUTE_TM), 0)
    c = lax.broadcasted_iota(jnp.int32, (ROUTE_TM, ROUTE_TM), 1)
    tri = jnp.where(c < r, 1.0, 0.0).astype(BF16)
    before = _dot(tri, oh.astype(BF16)) + carry_sc[...]
    rank1 = jnp.sum(jnp.where(oh1, before, 0.0), axis=-1, keepdims=True)
    rank2 = jnp.sum(jnp.where(oh2, before, 0.0), axis=-1, keepdims=True)
    meta = jnp.zeros((ROUTE_TM, LANES), F32)
    for k, val in enumerate((i1, i2, g1, g2, rank1, rank2)):
        meta = jnp.where(lane == k, val, meta)
    meta_ref[...] = meta
    carry_sc[...] += jnp.sum(oh, axis=0, keepdims=True)
    cnt_ref[...] = carry_sc[...]


def _route_call(logits):
    rows = logits.shape[0]
    return pl.pallas_call(
        _route_kernel,
        grid=(rows // ROUTE_TM,),
        in_specs=[pl.BlockSpec((ROUTE_TM, LANES), lambda i: (i, 0))],
        out_specs=[pl.BlockSpec((ROUTE_TM, LANES), lambda i: (i, 0)),
                   pl.BlockSpec((1, LANES), lambda i: (0, 0))],
        out_shape=[jax.ShapeDtypeStruct((rows, LANES), F32), jax.ShapeDtypeStruct((1, LANES), F32)],
        scratch_shapes=[pltpu.VMEM((1, LANES), F32)],
        compiler_params=_cparams("arbitrary"),
        name="moe_route",
    )(logits)


INV_TM = 512


def _invert_kernel(pos_ref, zeros_ref, inv_ref, sem):
    @pl.when(pl.program_id(0) == 0)
    def _():
        fill = pltpu.make_async_copy(zeros_ref, inv_ref, sem)
        fill.start()
        fill.wait()

    base = pl.program_id(0) * INV_TM

    def scatter(r, carry):
        inv_ref[pos_ref[base + r]] = (base + r) >> 1
        return carry
    lax.fori_loop(0, INV_TM, scatter, 0, unroll=8)


def _invert_call(pos):
    return pl.pallas_call(
        _invert_kernel,
        grid_spec=pltpu.PrefetchScalarGridSpec(
            num_scalar_prefetch=1,
            grid=(pos.shape[0] // INV_TM,),
            in_specs=[pl.BlockSpec(memory_space=pl.ANY)],
            out_specs=pl.BlockSpec(memory_space=pltpu.SMEM),
            scratch_shapes=[pltpu.SemaphoreType.DMA(())],
        ),
        out_shape=jax.ShapeDtypeStruct((MOE_ROWS,), jnp.int32),
        compiler_params=_cparams("arbitrary"),
        name="moe_invert",
    )(pos, jnp.zeros((MOE_ROWS,), jnp.int32))


def _moe_kernel(inv_ref, ps_ref, pn_ref, pd_ref, h_ref, w1_ref, w3_ref, w2_ref, y_ref,
                xg_sc, acc_sc, st_sc, gsem, osem):
    p = pl.program_id(0)
    f = pl.program_id(1)
    n_f = pl.num_programs(1)
    start = ps_ref[p]
    n = pn_ref[p]
    subs = [slice(i * MOE_SUB, (i + 1) * MOE_SUB) for i in range(MOE_NSUB)]

    def row_copy(src_row, r):
        return pltpu.make_async_copy(h_ref.at[pl.ds(src_row, 1)], xg_sc.at[pl.ds(r, 1)], gsem)

    @pl.when((p == 0) & (f == 0))
    def _():
        st_sc[0] = jnp.zeros(st_sc.shape[1:], U32)
        tail = [pltpu.make_async_copy(st_sc.at[0], y_ref.at[pl.ds(r0, MOE_SUB)], osem.at[0])
                for r0 in range(2 * BATCH * SEQ, MOE_ROWS, MOE_SUB)]
        for copy in tail:
            copy.start()
        for copy in tail:
            copy.wait()

    @pl.when(f == 0)
    def _():
        def per_sub(fn):
            def body(sub, carry):
                lax.fori_loop(0, MOE_SUB // 2, functools.partial(fn, sub * MOE_SUB), 0, unroll=4)
                return carry
            lax.fori_loop(0, n, body, 0)

        def issue(r0, i, carry):
            for k in range(2):
                r = r0 + 2 * i + k
                row_copy(inv_ref[start * MOE_SUB + r], r).start(priority=k)
            return carry

        def wait(r0, i, carry):
            for k in range(2):
                row_copy(0, r0 + 2 * i + k).wait()
            return carry

        per_sub(issue)
        for sub, rows in enumerate(subs):
            @pl.when(sub < n)
            def _():
                acc_sc[rows, :] = jnp.zeros((MOE_SUB, D_MODEL), F32)
        per_sub(wait)

    def weights():
        return w1_ref[0, 0], w3_ref[0, 0], w2_ref[0, 0]

    def accumulate(row_blocks, w):
        _swiglu_accumulate(xg_sc, acc_sc, row_blocks, *w, packed=True)

    @pl.when(n >= MOE_FAST)
    def _():
        w = weights()
        fast_rows = MOE_FAST * MOE_SUB
        accumulate([slice(r, r + MOE_FAST_ROWS) for r in range(0, fast_rows, MOE_FAST_ROWS)], w)
        for sub in range(MOE_FAST, MOE_NSUB):
            @pl.when(sub < n)
            def _():
                accumulate([subs[sub]], w)

    @pl.when((n > 0) & (n < MOE_FAST))
    def _():
        w = weights()
        for sub in range(MOE_FAST):
            @pl.when(sub < n)
            def _():
                accumulate([subs[sub]], w)

    @pl.when(f == n_f - 1)
    def _():
        def out_copy(sub):
            row0 = pl.multiple_of((start + sub) * MOE_SUB, MOE_SUB)
            return pltpu.make_async_copy(st_sc.at[sub % 2], y_ref.at[pl.ds(row0, MOE_SUB)], osem.at[sub % 2])

        for sub, rows in enumerate(subs):
            @pl.when(sub < n)
            def _():
                if sub >= 2:
                    out_copy(sub - 2).wait()
                st_sc[sub % 2] = _pack_bf16_pairs(acc_sc[rows, :])
                out_copy(sub).start()

        for sub in range(MOE_NSUB):
            @pl.when((sub < n) & (sub + 2 >= n))
            def _():
                out_copy(sub).wait()


def _moe_call(inv, pair_tables, h_packed, w1, w3, w2):
    n_f = D_FF_EXPERT // MOE_TF

    def f_eff(p, f, pn):
        return jnp.where(pn[p] > 0, f, n_f - 1)

    w13 = pl.BlockSpec((1, 1, D_MODEL, MOE_TF), lambda p, f, inv, ps, pn, pd: (0, pd[p], 0, f_eff(p, f, pn)))
    w2s = pl.BlockSpec((1, 1, MOE_TF, D_MODEL), lambda p, f, inv, ps, pn, pd: (0, pd[p], f_eff(p, f, pn), 0))
    hbm = pl.BlockSpec(memory_space=pl.ANY)
    win = MOE_NSUB * MOE_SUB
    n_prefetch = 4
    return pl.pallas_call(
        _moe_kernel,
        grid_spec=pltpu.PrefetchScalarGridSpec(
            num_scalar_prefetch=n_prefetch,
            grid=(MOE_PAIRS, n_f),
            in_specs=[hbm, w13, w13, w2s],
            out_specs=hbm,
            scratch_shapes=[pltpu.VMEM((win, D_MODEL // 2), U32),
                            pltpu.VMEM((win, D_MODEL), F32),
                            pltpu.VMEM((2, MOE_SUB, D_MODEL // 2), U32),
                            pltpu.SemaphoreType.DMA(()),
                            pltpu.SemaphoreType.DMA((2,))],
        ),
        out_shape=jax.ShapeDtypeStruct((MOE_ROWS, D_MODEL // 2), U32),
        compiler_params=_cparams("arbitrary", "arbitrary"),
        name="moe_experts",
    )(inv, *pair_tables, h_packed, w1, w3, w2)


def _pair_schedule(counts):
    nsub = (counts + MOE_SUB - 1) // MOE_SUB
    seg_start = jnp.cumsum(nsub) - nsub
    nchunk = (nsub + MOE_NSUB - 1) // MOE_NSUB
    chunk_end = jnp.cumsum(nchunk)
    idx = jnp.arange(MOE_PAIRS, dtype=jnp.int32)
    expert = jnp.sum(chunk_end[None, :] <= idx[:, None], axis=1).astype(jnp.int32)
    active = expert < N_EXPERTS
    onehot = expert[:, None] == jnp.arange(N_EXPERTS, dtype=jnp.int32)[None, :]
    pick = lambda a: jnp.sum(jnp.where(onehot, a[None, :], 0), axis=1)
    c = idx - (pick(chunk_end) - pick(nchunk))
    first = jnp.where(active, pick(seg_start) + MOE_NSUB * c, 0)
    count = jnp.where(active, jnp.minimum(MOE_NSUB, pick(nsub) - MOE_NSUB * c), 0)
    dma_expert = jnp.where(active, expert, jnp.max(jnp.where(active, expert, 0)))
    i32 = lambda a: a.astype(jnp.int32)
    return seg_start * MOE_SUB, tuple(i32(a) for a in (first, count, dma_expert))


def _combine_kernel(pos_ref, x_ref, meta_ref, gf_ref, ys_ref, o_ref, buf, sem):
    base = pl.program_id(0) * COMB_TM

    def copy(r, k, src_row):
        return pltpu.make_async_copy(ys_ref.at[pl.ds(src_row, 1)], buf.at[k, pl.ds(r, 1)], sem)

    def start(r, carry):
        for k in range(2):
            copy(r, k, pos_ref[2 * (base + r) + k]).start(priority=k)
        return carry

    lax.fori_loop(0, COMB_TM, start, 0, unroll=8)

    def wait(r, carry):
        for k in range(2):
            copy(r, k, 0).wait()
        return carry

    lax.fori_loop(0, COMB_TM, wait, 0, unroll=8)

    half = D_MODEL // 2
    lo0, hi0 = _unpack_bf16_pairs(buf[0])
    lo1, hi1 = _unpack_bf16_pairs(buf[1])
    g1 = meta_ref[:, 2:3]
    g2 = meta_ref[:, 3:4]
    gf = gf_ref[0]
    o_ref[:, :half] = x_ref[:, :half] + gf[:, :half] * (g1 * lo0 + g2 * lo1)
    o_ref[:, half:] = x_ref[:, half:] + gf[:, half:] * (g1 * hi0 + g2 * hi1)


def _combine_call(pos, x, meta, gf, ys):
    rows = x.shape[0]
    tiles_per_batch = SEQ // COMB_TM
    return pl.pallas_call(
        _combine_kernel,
        grid_spec=pltpu.PrefetchScalarGridSpec(
            num_scalar_prefetch=1,
            grid=(rows // COMB_TM,),
            in_specs=[
                pl.BlockSpec((COMB_TM, D_MODEL), lambda i, p: (i, 0)),
                pl.BlockSpec((COMB_TM, LANES), lambda i, p: (i, 0)),
                pl.BlockSpec((1, 1, D_MODEL), lambda i, p: (i // tiles_per_batch, 0, 0)),
                pl.BlockSpec(memory_space=pl.ANY),
            ],
            out_specs=pl.BlockSpec((COMB_TM, D_MODEL), lambda i, p: (i, 0)),
            scratch_shapes=[pltpu.VMEM((2, COMB_TM, D_MODEL // 2), U32), pltpu.SemaphoreType.DMA(())],
        ),
        out_shape=jax.ShapeDtypeStruct((rows, D_MODEL), F32),
        compiler_params=_cparams("arbitrary"),
        name="moe_combine",
    )(pos, x, meta, gf, ys)


def _moe_layer(x, h_packed, logits, gf, w1, w3, w2):
    meta, cnt = _route_call(logits)
    expert = meta[:, 0:2].astype(jnp.int32)
    rank = meta[:, 4:6].astype(jnp.int32)
    counts = cnt[0, :N_EXPERTS].astype(jnp.int32)
    seg_start, pair_tables = _pair_schedule(counts)
    ids = jnp.arange(N_EXPERTS, dtype=jnp.int32)
    pos = rank + jnp.sum(jnp.where(expert[..., None] == ids, seg_start, 0), axis=-1)
    pos = pos.reshape(-1).astype(jnp.int32)
    ys = _moe_call(_invert_call(pos), pair_tables, h_packed, w1, w3, w2)
    return _combine_call(pos, x, meta, gf, ys)


def _rope_tables(dh):
    rows = SEQ // GRID_W
    row = jnp.repeat(jnp.arange(rows, dtype=F32), GRID_W)
    col = jnp.tile(jnp.arange(GRID_W, dtype=F32), rows)
    half = dh // 2
    inv = ROPE_THETA ** (-jnp.arange(0, half, 2, dtype=F32) / half)
    ar = row[:, None] * inv[None, :]
    ac = col[:, None] * inv[None, :]
    ang = jnp.concatenate([ar, ar, ac, ac], axis=-1)
    ang = jnp.concatenate([jnp.zeros((CTX_LEN, dh), F32), ang], axis=0)
    ang = jnp.tile(ang, (1, LANES // dh))
    odd = ((jnp.arange(LANES) // (dh // 4)) % 2 == 1)[None, :]
    sin = jnp.sin(ang)
    return jnp.cos(ang), jnp.where(odd, sin, 0.0), jnp.where(odd, 0.0, -sin)


def kernel(x, c, ctx, c_ctx, ada_w, ada_b, norm_mix_g, norm_ffn_g, w_in, w_out, a_q_norm, a_k_norm,
           a_lam_q1, a_lam_k1, a_lam_q2, a_lam_k2, a_subln_g, b_conv_w, b_out_g, c_q_norm, c_k_norm,
           c_sink, c_out_g, ffn_w1, ffn_w3, ffn_w2, router_w, moe_w1, moe_w3, moe_w2):
    assert x.shape == (BATCH, SEQ, D_MODEL) and ctx.shape == (BATCH, CTX_LEN, D_MODEL)
    cond = jnp.concatenate([c, c_ctx[None, :], jnp.zeros((8 - BATCH - 1, D_MODEL), F32)], axis=0)
    mod = _ada_call(cond, ada_w, ada_b)
    tables = _rope_tables(A_DK) + _rope_tables(C_DH)
    xs = (ctx, x)
    y_prev = gf_prev = None
    out = None
    moe_bf16 = []
    assert DEPTH == 2 and moe_w1.shape[0] == 1
    for l in range(DEPTH):
        last = l == DEPTH - 1
        sh_a, sc_a, g_a, sh_f, sc_f, g_f = [
            mod[l, :BATCH + 1, k * D_MODEL:(k + 1) * D_MODEL].reshape(BATCH + 1, 1, D_MODEL) for k in range(6)]
        gains = (jnp.tile(a_q_norm[l], LANES // A_DK)[None, :], jnp.tile(a_k_norm[l], LANES // A_DK)[None, :],
                 c_q_norm[l][None, :], c_k_norm[l][None, :])
        side = [] if l else [(ffn_w1[0], None), (ffn_w3[0], None), (ffn_w2[0], None),
                             (w_in, 1), (w_out, 0), (w_out, 1)]
        res = _in_call(xs, y_prev, gf_prev, norm_mix_g[l][None, :], sc_a, sh_a,
                       w_in_bf16[1] if l else w_in[0].astype(BF16), gains, tables, side)
        if y_prev is not None:
            xs, res = res[0], res[1:]
        qa, ka, va, bb, u, qc, kc, vc = res[:8]
        if not l:
            ffn_bf16, w_in_bf16, w_out_bf16 = res[8:11], (None, res[11]), res[12:14]
        lam_init = 0.8 - 0.6 * math.exp(-0.3 * l)
        lam_vecs = [v[l][None, :] for v in (a_lam_q1, a_lam_k1, a_lam_q2, a_lam_k2)]
        cast_jobs = ((moe_w1[0], moe_w2[0]), (moe_w3[0],))[l]
        oa, cast_out = _attn_a_call(qa, ka, va, lam_vecs, a_subln_g[l][None, :], lam_init, with_ctx=not last,
                                    cast=[w.reshape(-1, w.shape[-1]) for w in cast_jobs])
        moe_bf16 += [c.reshape((1,) + w.shape) for c, w in zip(cast_out, cast_jobs)]
        oc = _attn_c_call(c_sink[l], qc, kc, vc, with_ctx=not last)
        wr = None
        if last:
            wr = jnp.pad(router_w[l // 2], ((0, 0), (0, LANES - N_EXPERTS)))
        res = _out_call(xs, oa, bb, u, oc, w_out_bf16[l], g_a, b_out_g[l][None, :],
                        c_out_g[l][None, :], b_conv_w[l], norm_ffn_g[l][None, :], sc_f, sh_f, wr)
        if not last:
            xs, h2 = res
            y = _ffn_call(h2.reshape(BATCH * N_TOK, D_MODEL), *ffn_bf16)
            y_prev, gf_prev = y.reshape(BATCH, N_TOK, D_MODEL), g_f
        else:
            x_lat, h_packed, logits = res
            w1b, w2b, w3b = moe_bf16
            out = _moe_layer(x_lat.reshape(BATCH * SEQ, D_MODEL), h_packed.reshape(BATCH * SEQ, D_MODEL // 2),
                             logits.reshape(BATCH * SEQ, LANES), g_f[:BATCH], w1b, w3b, w2b)
    return out.reshape(BATCH, SEQ, D_MODEL)
```
